```python
import math
import jax
import jax.numpy as jnp
from jax import lax
import numpy as np

D_MODEL = 2048
BATCH = 16
SEQ = 2048
DEPTH = 2

HEAD_DIM = 128
MIX_HEADS = D_MODEL // (2 * HEAD_DIM)

DN_HEADS = MIX_HEADS
DN_DK = HEAD_DIM
DN_DV = HEAD_DIM
DN_CONV = 4
DN_CHUNK = 64

NSA_HEADS = MIX_HEADS
NSA_GROUPS = 2
NSA_HPG = NSA_HEADS // NSA_GROUPS
NSA_DH = HEAD_DIM
CMP_LEN = 32
CMP_STRIDE = 16
CMP_HIDDEN = 256
SEL_BLOCK = 64
SEL_TOPK = 16
SEL_Q_CHUNK = 16
WIN = 512
Q_BLOCK = 128
ROPE_THETA = 10000.0

D_FF = -(-8 * D_MODEL // (3 * 256)) * 256
PLE_DIM = 256
ALPHA = (2.0 * DEPTH) ** 0.25
BETA = (8.0 * DEPTH) ** -0.25
LN_EPS = 1e-5
NORM_EPS = 1e-6
NEG_INF = -1e30

DN_QK = DN_HEADS * DN_DK
DN_VW = DN_HEADS * DN_DV
NSA_QW = NSA_HEADS * NSA_DH
NSA_KVW = NSA_GROUPS * NSA_DH
IN_SPLITS = (DN_QK, DN_QK, DN_VW, DN_VW, DN_HEADS, DN_HEADS,
             NSA_QW, NSA_KVW, NSA_KVW, NSA_KVW, NSA_KVW, NSA_KVW, NSA_KVW,
             3 * NSA_HEADS, D_MODEL, D_MODEL)
N_IN = sum(IN_SPLITS)

kernel_name = 'hybrid_deltanet_nsa_deepnorm_block'


def layer_norm(x, g, b):
    xf = x.astype(jnp.float32)
    mu = jnp.mean(xf, axis=-1, keepdims=True)
    var = jnp.mean(jnp.square(xf - mu), axis=-1, keepdims=True)
    y = (xf - mu) * lax.rsqrt(var + LN_EPS)
    return (y * g.astype(jnp.float32) + b.astype(jnp.float32)).astype(x.dtype)


def l2norm(x):
    xf = x.astype(jnp.float32)
    return xf * lax.rsqrt(jnp.sum(xf * xf, axis=-1, keepdims=True) + NORM_EPS)


def rope(x):
    s_len, dh = x.shape[1], x.shape[-1]
    half = dh // 2
    inv_freq = ROPE_THETA ** (-jnp.arange(half, dtype=jnp.float32) / half)
    ang = jnp.arange(s_len, dtype=jnp.float32)[:, None] * inv_freq[None, :]
    cos = jnp.cos(ang)[:, None, :]
    sin = jnp.sin(ang)[:, None, :]
    xf = x.astype(jnp.float32)
    x1, x2 = xf[..., :half], xf[..., half:]
    return jnp.concatenate([x1 * cos - x2 * sin, x1 * sin + x2 * cos], axis=-1).astype(x.dtype)


def masked_softmax(s, mask):
    p = jax.nn.softmax(jnp.where(mask, s.astype(jnp.float32), NEG_INF), axis=-1)
    return p * mask


def causal_conv_silu(u, w):
    c = u.shape[-1]
    y = lax.conv_general_dilated(
        u, w[:, None, :].astype(u.dtype), window_strides=(1,),
        padding=[(w.shape[0] - 1, 0)], dimension_numbers=('NWC', 'WIO', 'NWC'),
        feature_group_count=c)
    return jax.nn.silu(y)


def chunk_gated_delta_rule(q, k, v, g, beta):
    b_, s_len, h, dk = q.shape
    dv = v.shape[-1]
    c = DN_CHUNK
    n = s_len // c

    def chunks(t):
        return t.reshape(b_, n, c, h, t.shape[-1]).transpose(0, 3, 1, 2, 4)

    q, k, v = chunks(q), chunks(k), chunks(v)
    g = g.reshape(b_, n, c, h).transpose(0, 3, 1, 2)
    beta = beta.reshape(b_, n, c, h).transpose(0, 3, 1, 2)
    g = jnp.cumsum(g, axis=-1)
    idx = jnp.arange(c)
    causal = idx[:, None] >= idx[None, :]
    strict = idx[:, None] > idx[None, :]
    decay = jnp.exp(jnp.where(causal, g[..., :, None] - g[..., None, :], -jnp.inf))
    k_beta = k * beta[..., None]
    lower = jnp.where(strict, jnp.einsum('bhncd,bhnmd->bhncm', k_beta, k) * decay, 0.0)
    eye = jnp.broadcast_to(jnp.eye(c, dtype=jnp.float32), lower.shape)
    t_inv = lax.linalg.triangular_solve(lower + eye, eye, left_side=True, lower=True)
    u = jnp.einsum('bhncm,bhnmv->bhncv', t_inv, v * beta[..., None])
    w = jnp.einsum('bhncm,bhnmk->bhnck', t_inv, k_beta * jnp.exp(g)[..., None])
    qk = jnp.where(causal, jnp.einsum('bhncd,bhnmd->bhncm', q, k) * decay, 0.0)
    g_last = g[..., -1]
    q_dec = q * jnp.exp(g)[..., None]
    k_dec = k * jnp.exp(g_last[..., None] - g)[..., None]

    def step(state, xs):
        u_i, w_i, qk_i, qd_i, kd_i, gl_i = xs
        v_new = u_i - jnp.einsum('bhck,bhkv->bhcv', w_i, state)
        o_i = jnp.einsum('bhck,bhkv->bhcv', qd_i, state) + jnp.einsum('bhcm,bhmv->bhcv', qk_i, v_new)
        state = state * jnp.exp(gl_i)[..., None, None] + jnp.einsum('bhck,bhcv->bhkv', kd_i, v_new)
        return state, o_i

    xs = (jnp.moveaxis(u, 2, 0), jnp.moveaxis(w, 2, 0), jnp.moveaxis(qk, 2, 0),
          jnp.moveaxis(q_dec, 2, 0), jnp.moveaxis(k_dec, 2, 0), jnp.moveaxis(g_last, 2, 0))
    state0 = jnp.zeros((b_, h, dk, dv), jnp.float32)
    _, o = lax.scan(step, state0, xs)
    return o.transpose(1, 0, 3, 2, 4).reshape(b_, s_len, h, dv)


def gated_deltanet(q, k, v, z, b, a, conv_w, a_log, dt_bias, norm_w):
    b_, s_len, _ = q.shape
    f32 = jnp.float32
    qkv = causal_conv_silu(jnp.concatenate([q, k, v], axis=-1), conv_w)
    q, k, v = jnp.split(qkv, [DN_QK, 2 * DN_QK], axis=-1)
    q = l2norm(q.reshape(b_, s_len, DN_HEADS, DN_DK)) * (DN_DK ** -0.5)
    k = l2norm(k.reshape(b_, s_len, DN_HEADS, DN_DK))
    v = v.reshape(b_, s_len, DN_HEADS, DN_DV).astype(f32)
    beta = jax.nn.sigmoid(b.astype(f32))
    g = -jnp.exp(a_log.astype(f32)) * jax.nn.softplus(a.astype(f32) + dt_bias.astype(f32))
    o = chunk_gated_delta_rule(q, k, v, g, beta)
    o = o * lax.rsqrt(jnp.mean(o * o, axis=-1, keepdims=True) + NORM_EPS) * norm_w.astype(f32)
    o = o * jax.nn.silu(z.astype(f32).reshape(b_, s_len, DN_HEADS, DN_DV))
    return o.reshape(b_, s_len, DN_VW).astype(z.dtype)


def compress_blocks(t, pe, w1, w2):
    b_, g_, s_len, dh = t.shape
    r = CMP_LEN // CMP_STRIDE
    nch = s_len // CMP_STRIDE
    nc = nch - r + 1
    ch = t.reshape(b_, g_, nch, CMP_STRIDE, dh)
    blocks = jnp.concatenate([ch[:, :, i:i + nc] for i in range(r)], axis=3)
    blocks = (blocks + pe.astype(t.dtype)).reshape(b_, g_, nc, CMP_LEN * dh)
    return jax.nn.silu(blocks @ w1) @ w2


def native_sparse_attention(q, k_c, v_c, k_s, v_s, k_w, v_w, gate,
                            pe_k, w1_k, w2_k, pe_v, w1_v, w2_v):
    b_, s_len, _ = q.shape
    dt = q.dtype
    f32 = jnp.float32
    t = jnp.arange(s_len)
    q = rope(q.reshape(b_, s_len, NSA_HEADS, NSA_DH)) * (NSA_DH ** -0.5)
    q = q.reshape(b_, s_len, NSA_GROUPS, NSA_HPG, NSA_DH).transpose(0, 2, 3, 1, 4)

    def heads(u, rotate):
        u = u.reshape(b_, s_len, NSA_GROUPS, NSA_DH)
        if rotate:
            u = rope(u)
        return u.transpose(0, 2, 1, 3)

    k_c, k_s, k_w = heads(k_c, True), heads(k_s, True), heads(k_w, True)
    v_c, v_s, v_w = heads(v_c, False), heads(v_s, False), heads(v_w, False)

    kc = compress_blocks(k_c, pe_k, w1_k, w2_k)
    vc = compress_blocks(v_c, pe_v, w1_v, w2_v)
    nc = kc.shape[2]
    cmp_start = jnp.arange(nc) * CMP_STRIDE
    cmp_mask = (cmp_start + CMP_LEN - 1)[None, :] <= t[:, None]
    p_cmp = masked_softmax(jnp.einsum('bghsd,bgnd->bghsn', q, kc), cmp_mask)
    o_cmp = jnp.einsum('bghsn,bgnd->bghsd', p_cmp, vc.astype(f32))

    ns = s_len // SEL_BLOCK
    sel_start = jnp.arange(ns) * SEL_BLOCK
    overlap = ((cmp_start[:, None] <= sel_start[None, :] + SEL_BLOCK - 1)
               & (cmp_start[:, None] + CMP_LEN - 1 >= sel_start[None, :])).astype(f32)
    imp = jnp.einsum('bghsn,nj->bgsj', p_cmp, overlap)
    cur = t // SEL_BLOCK
    j = jnp.arange(ns)
    forced = (j[None, :] == 0) | (j[None, :] == cur[:, None]) | (j[None, :] == cur[:, None] - 1)
    causal_blk = j[None, :] <= cur[:, None]
    imp = jnp.where(forced, jnp.inf, jnp.where(causal_blk, imp, -jnp.inf))
    n_top = min(SEL_TOPK, ns)
    _, sel_idx = lax.top_k(imp, n_top)

    kb = k_s.reshape(b_, NSA_GROUPS, ns, SEL_BLOCK, NSA_DH)
    vb = v_s.reshape(b_, NSA_GROUPS, ns, SEL_BLOCK, NSA_DH)
    nq = s_len // SEL_Q_CHUNK
    q_ch = q.reshape(b_, NSA_GROUPS, NSA_HPG, nq, SEL_Q_CHUNK, NSA_DH).transpose(3, 0, 1, 2, 4, 5)
    idx_ch = sel_idx.reshape(b_, NSA_GROUPS, nq, SEL_Q_CHUNK, n_top).transpose(2, 0, 1, 3, 4)
    pos_ch = t.reshape(nq, SEL_Q_CHUNK)
    bi = jnp.arange(b_)[:, None, None, None]
    gi = jnp.arange(NSA_GROUPS)[None, :, None, None]
    offs = jnp.arange(SEL_BLOCK)

    def sel_chunk(args):
        qc, ic, pc = args
        kg = kb[bi, gi, ic].reshape(b_, NSA_GROUPS, SEL_Q_CHUNK, n_top * SEL_BLOCK, NSA_DH)
        vg = vb[bi, gi, ic].reshape(b_, NSA_GROUPS, SEL_Q_CHUNK, n_top * SEL_BLOCK, NSA_DH)
        kpos = (ic[..., None] * SEL_BLOCK + offs).reshape(b_, NSA_GROUPS, SEL_Q_CHUNK, n_top * SEL_BLOCK)
        mask = (kpos <= pc[None, None, :, None])[:, :, None]
        p = masked_softmax(jnp.einsum('bghqd,bgqmd->bghqm', qc, kg), mask)
        return jnp.einsum('bghqm,bgqmd->bghqd', p, vg.astype(f32))

    o_sel = lax.map(sel_chunk, (q_ch, idx_ch, pos_ch))
    o_sel = o_sel.transpose(1, 2, 3, 0, 4, 5).reshape(b_, NSA_GROUPS, NSA_HPG, s_len, NSA_DH)

    span = WIN + Q_BLOCK
    kp = jnp.pad(k_w, ((0, 0), (0, 0), (WIN, 0), (0, 0)))
    vp = jnp.pad(v_w, ((0, 0), (0, 0), (WIN, 0), (0, 0)))
    nqb = s_len // Q_BLOCK
    q_blk = q.reshape(b_, NSA_GROUPS, NSA_HPG, nqb, Q_BLOCK, NSA_DH).transpose(3, 0, 1, 2, 4, 5)

    def win_block(args):
        qb, jb = args
        start = jb * Q_BLOCK
        kw_ = lax.dynamic_slice_in_dim(kp, start, span, axis=2)
        vw_ = lax.dynamic_slice_in_dim(vp, start, span, axis=2)
        qpos = start + jnp.arange(Q_BLOCK)
        kpos = start - WIN + jnp.arange(span)
        diff = qpos[:, None] - kpos[None, :]
        mask = (diff >= 0) & (diff < WIN) & (kpos[None, :] >= 0)
        p = masked_softmax(jnp.einsum('bghqd,bgkd->bghqk', qb, kw_), mask)
        return jnp.einsum('bghqk,bgkd->bghqd', p, vw_.astype(f32))

    o_win = lax.map(win_block, (q_blk, jnp.arange(nqb)))
    o_win = o_win.transpose(1, 2, 3, 0, 4, 5).reshape(b_, NSA_GROUPS, NSA_HPG, s_len, NSA_DH)

    gt = jax.nn.sigmoid(gate.astype(f32)).reshape(b_, s_len, NSA_GROUPS, NSA_HPG, 3).transpose(0, 2, 3, 1, 4)
    o = gt[..., 0:1] * o_cmp + gt[..., 1:2] * o_sel + gt[..., 2:3] * o_win
    return o.transpose(0, 3, 1, 2, 4).reshape(b_, s_len, NSA_QW).astype(dt)


def hybrid_layer(x, p_i, w_in, conv_w, a_log, dt_bias, dn_norm_w,
                 pe_k, w1_k, w2_k, pe_v, w1_v, w2_v, w_a, w_b, w_out, ln1_g, ln1_b,
                 w_gate, w_up, w_down, w_ple, w_ple_gate, ln2_g, ln2_b):
    points = np.cumsum(IN_SPLITS)[:-1].tolist()
    h = x @ w_in
    (dn_q, dn_k, dn_v, dn_z, dn_b, dn_a, nsa_q, k_c, v_c, k_s, v_s, k_w, v_w,
     nsa_gate, merge_a, merge_b) = jnp.split(h, points, axis=-1)
    o_a = gated_deltanet(dn_q, dn_k, dn_v, dn_z, dn_b, dn_a, conv_w, a_log, dt_bias, dn_norm_w)
    o_b = native_sparse_attention(nsa_q, k_c, v_c, k_s, v_s, k_w, v_w, nsa_gate,
                                  pe_k, w1_k, w2_k, pe_v, w1_v, w2_v)
    mixed = jax.nn.sigmoid(merge_a) * (o_a @ w_a) + jax.nn.sigmoid(merge_b) * (o_b @ w_b)
    x = layer_norm(ALPHA * x + mixed @ w_out, ln1_g, ln1_b)
    ffn = (jax.nn.silu(x @ w_gate) * (x @ w_up)) @ w_down
    ple = (p_i @ w_ple) * jax.nn.sigmoid(x @ w_ple_gate)
    return layer_norm(ALPHA * x + ffn + ple, ln2_g, ln2_b)


def setup_inputs(seed: int = 0) -> dict:
    key = jax.random.key(seed)
    ks = jax.random.split(key, 32)
    f32 = jnp.float32
    L = DEPTH

    def nrm(k, shape, fan_in, scale=1.0):
        return jax.random.normal(k, shape, f32) * (scale * fan_in ** -0.5)

    def gain(k, shape):
        return 1.0 + 0.02 * jax.random.normal(k, shape, f32)

    def small(k, shape):
        return 0.02 * jax.random.normal(k, shape, f32)

    x = jax.random.normal(ks[0], (BATCH, SEQ, D_MODEL), f32)
    p = jax.random.normal(ks[1], (DEPTH, BATCH, SEQ, PLE_DIM), f32)
    w_in = nrm(ks[2], (L, D_MODEL, N_IN), D_MODEL)
    dn_conv_w = nrm(ks[3], (L, DN_CONV, 2 * DN_QK + DN_VW), DN_CONV)
    dn_a_log = jnp.log(jax.random.uniform(ks[4], (L, DN_HEADS), f32, 1.0, 16.0))
    dt = jnp.exp(jax.random.uniform(ks[5], (L, DN_HEADS), f32, math.log(1e-3), math.log(1e-1)))
    dn_dt_bias = dt + jnp.log(-jnp.expm1(-dt))
    dn_norm_w = gain(ks[6], (L, DN_DV))
    cmp_pe_k = small(ks[7], (L, CMP_LEN, NSA_DH))
    cmp_w1_k = nrm(ks[8], (L, CMP_LEN * NSA_DH, CMP_HIDDEN), CMP_LEN * NSA_DH)
    cmp_w2_k = nrm(ks[9], (L, CMP_HIDDEN, NSA_DH), CMP_HIDDEN)
    cmp_pe_v = small(ks[10], (L, CMP_LEN, NSA_DH))
    cmp_w1_v = nrm(ks[11], (L, CMP_LEN * NSA_DH, CMP_HIDDEN), CMP_LEN * NSA_DH)
    cmp_w2_v = nrm(ks[12], (L, CMP_HIDDEN, NSA_DH), CMP_HIDDEN)
    w_branch_a = nrm(ks[13], (L, DN_VW, D_MODEL), DN_VW, BETA)
    w_branch_b = nrm(ks[14], (L, NSA_QW, D_MODEL), NSA_QW, BETA)
    w_out = nrm(ks[15], (L, D_MODEL, D_MODEL), D_MODEL, BETA)
    ln1_g = gain(ks[16], (L, D_MODEL))
    ln1_b = small(ks[17], (L, D_MODEL))
    w_ffn_gate = nrm(ks[18], (L, D_MODEL, D_FF), D_MODEL)
    w_ffn_up = nrm(ks[19], (L, D_MODEL, D_FF), D_MODEL)
    w_ffn_down = nrm(ks[20], (L, D_FF, D_MODEL), D_FF, BETA)
    w_ple = nrm(ks[21], (L, PLE_DIM, D_MODEL), PLE_DIM, BETA)
    w_ple_gate = nrm(ks[22], (L, D_MODEL, D_MODEL), D_MODEL)
    ln2_g = gain(ks[23], (L, D_MODEL))
    ln2_b = small(ks[24], (L, D_MODEL))
    return {'x': x, 'p': p, 'w_in': w_in, 'dn_conv_w': dn_conv_w, 'dn_a_log': dn_a_log,
            'dn_dt_bias': dn_dt_bias, 'dn_norm_w': dn_norm_w,
            'cmp_pe_k': cmp_pe_k, 'cmp_w1_k': cmp_w1_k, 'cmp_w2_k': cmp_w2_k,
            'cmp_pe_v': cmp_pe_v, 'cmp_w1_v': cmp_w1_v, 'cmp_w2_v': cmp_w2_v,
            'w_branch_a': w_branch_a, 'w_branch_b': w_branch_b, 'w_out': w_out,
            'ln1_g': ln1_g, 'ln1_b': ln1_b, 'w_ffn_gate': w_ffn_gate, 'w_ffn_up': w_ffn_up,
            'w_ffn_down': w_ffn_down, 'w_ple': w_ple, 'w_ple_gate': w_ple_gate,
            'ln2_g': ln2_g, 'ln2_b': ln2_b}


def reference(x, p, w_in, dn_conv_w, dn_a_log, dn_dt_bias, dn_norm_w,
              cmp_pe_k, cmp_w1_k, cmp_w2_k, cmp_pe_v, cmp_w1_v, cmp_w2_v,
              w_branch_a, w_branch_b, w_out, ln1_g, ln1_b,
              w_ffn_gate, w_ffn_up, w_ffn_down, w_ple, w_ple_gate, ln2_g, ln2_b):
    for i in range(DEPTH):
        x = hybrid_layer(x, p[i], w_in[i], dn_conv_w[i], dn_a_log[i], dn_dt_bias[i], dn_norm_w[i],
                         cmp_pe_k[i], cmp_w1_k[i], cmp_w2_k[i], cmp_pe_v[i], cmp_w1_v[i], cmp_w2_v[i],
                         w_branch_a[i], w_branch_b[i], w_out[i], ln1_g[i], ln1_b[i],
                         w_ffn_gate[i], w_ffn_up[i], w_ffn_down[i], w_ple[i], w_ple_gate[i],
                         ln2_g[i], ln2_b[i])
    return x
```

```python
import functools
import math

import jax
import jax.numpy as jnp
from jax import lax
from jax.experimental import pallas as pl
from jax.experimental.pallas import tpu as pltpu

D_MODEL = 2048
DEPTH = 2
HEAD_DIM = 128
HEADS = 8
DN_CONV = 4
DN_CHUNK = 64
GROUPS = 2
HPG = HEADS // GROUPS
CMP_LEN = 32
CMP_STRIDE = 16
CMP_HIDDEN = 256
SEL_BLOCK = 64
SEL_TOPK = 16
WIN = 512
ROPE_THETA = 10000.0
D_FF = 5632
PLE_DIM = 256
ALPHA = (2.0 * DEPTH) ** 0.25
LN_EPS = 1e-5
NORM_EPS = 1e-6
NEG_INF = -1e30

LANES = 128
SUBLANES = 8
VMEM_LIMIT = 48 * 1024 * 1024

CB_DN_Q, CB_DN_K, CB_DN_V, CB_DN_Z = 0, 8, 16, 24
CB_NSA_Q = 32
CB_KV = 40
N_MAIN = 10752
COL_MERGE_A = 6656
COL_MERGE_B = 8704
SC_BETA, SC_DECAY, SC_GATE = 0, 8, 16

F32 = jnp.float32
BF16 = jnp.bfloat16
HI = lax.Precision.HIGHEST


def _cparams(sem):
    return pltpu.CompilerParams(dimension_semantics=sem, vmem_limit_bytes=VMEM_LIMIT)


def _dot(a, b):
    return jnp.dot(a, b, preferred_element_type=F32)


def _dot_nt(a, b):
    return lax.dot_general(a, b, (((1,), (1,)), ((), ())), preferred_element_type=F32)


def _dot_tn(a, b):
    return lax.dot_general(a, b, (((0,), (0,)), ((), ())), preferred_element_type=F32)


def _sigmoid(x):
    return 1.0 / (1.0 + jnp.exp(-x))


def _silu(x):
    return x * _sigmoid(x)


def _layer_norm(y, g, b):
    mu = jnp.mean(y, axis=-1, keepdims=True)
    d = y - mu
    var = jnp.mean(d * d, axis=-1, keepdims=True)
    return d * lax.rsqrt(var + LN_EPS) * g + b


def _lane_col(x, idx):
    lane = lax.broadcasted_iota(jnp.int32, x.shape, 1)
    return jnp.sum(jnp.where(lane == idx, x, 0.0), axis=1, keepdims=True)


def _mm_kernel(a_ref, w_ref, o_ref):
    o_ref[...] = _dot(a_ref[...], w_ref[...]).astype(o_ref.dtype)


def _matmul(a, w, out_dtype, tm, tn, name):
    m, k = a.shape
    n = w.shape[1]
    return pl.pallas_call(
        _mm_kernel,
        out_shape=jax.ShapeDtypeStruct((m, n), out_dtype),
        grid=(m // tm, n // tn),
        in_specs=[pl.BlockSpec((tm, k), lambda i, j: (i, 0)),
                  pl.BlockSpec((k, tn), lambda i, j: (0, j))],
        out_specs=pl.BlockSpec((tm, tn), lambda i, j: (i, j)),
        compiler_params=_cparams(("parallel", "arbitrary")),
        name=name,
    )(a, w)


def _merge_kernel(oa_ref, ob_ref, wa_ref, wb_ref, ma_ref, mb_ref, o_ref):
    ya = _dot(oa_ref[...], wa_ref[...])
    yb = _dot(ob_ref[...], wb_ref[...])
    o_ref[...] = (_sigmoid(ma_ref[...]) * ya + _sigmoid(mb_ref[...]) * yb).astype(o_ref.dtype)


def _merge(o_a, o_b, w_a, w_b, h_main, tm, tn):
    m, k = o_a.shape
    n = w_a.shape[1]
    ca, cb = COL_MERGE_A // tn, COL_MERGE_B // tn
    return pl.pallas_call(
        _merge_kernel,
        out_shape=jax.ShapeDtypeStruct((m, n), BF16),
        grid=(m // tm, n // tn),
        in_specs=[pl.BlockSpec((tm, k), lambda i, j: (i, 0)),
                  pl.BlockSpec((tm, k), lambda i, j: (i, 0)),
                  pl.BlockSpec((k, tn), lambda i, j: (0, j)),
                  pl.BlockSpec((k, tn), lambda i, j: (0, j)),
                  pl.BlockSpec((tm, tn), lambda i, j: (i, ca + j)),
                  pl.BlockSpec((tm, tn), lambda i, j: (i, cb + j))],
        out_specs=pl.BlockSpec((tm, tn), lambda i, j: (i, j)),
        compiler_params=_cparams(("parallel", "arbitrary")),
        name="merge",
    )(o_a, o_b, w_a, w_b, h_main, h_main)


def _outproj_ln_kernel(mx_ref, w_ref, x_ref, g_ref, b_ref, o_ref, ob_ref):
    y = ALPHA * x_ref[...] + _dot(mx_ref[...], w_ref[...])
    out = _layer_norm(y, g_ref[...], b_ref[...])
    o_ref[...] = out
    ob_ref[...] = out.astype(BF16)


def _outproj_ln(mixed, w_out, x, g, b, tm):
    m, d = x.shape
    return pl.pallas_call(
        _outproj_ln_kernel,
        out_shape=(jax.ShapeDtypeStruct((m, d), F32), jax.ShapeDtypeStruct((m, d), BF16)),
        grid=(m // tm,),
        in_specs=[pl.BlockSpec((tm, d), lambda i: (i, 0)),
                  pl.BlockSpec((d, d), lambda i: (0, 0)),
                  pl.BlockSpec((tm, d), lambda i: (i, 0)),
                  pl.BlockSpec((1, d), lambda i: (0, 0)),
                  pl.BlockSpec((1, d), lambda i: (0, 0))],
        out_specs=(pl.BlockSpec((tm, d), lambda i: (i, 0)),
                   pl.BlockSpec((tm, d), lambda i: (i, 0))),
        compiler_params=_cparams(("parallel",)),
        name="outproj_ln",
    )(mixed, w_out, x, g.reshape(1, d), b.reshape(1, d))


def _ffn_act_kernel(x_ref, wg_ref, wu_ref, o_ref):
    xv = x_ref[...]
    o_ref[...] = (_silu(_dot(xv, wg_ref[...])) * _dot(xv, wu_ref[...])).astype(o_ref.dtype)


def _ffn_act(xb, w_gate, w_up, tm, tn):
    m, k = xb.shape
    n = w_gate.shape[1]
    return pl.pallas_call(
        _ffn_act_kernel,
        out_shape=jax.ShapeDtypeStruct((m, n), BF16),
        grid=(m // tm, n // tn),
        in_specs=[pl.BlockSpec((tm, k), lambda i, j: (i, 0)),
                  pl.BlockSpec((k, tn), lambda i, j: (0, j)),
                  pl.BlockSpec((k, tn), lambda i, j: (0, j))],
        out_specs=pl.BlockSpec((tm, tn), lambda i, j: (i, j)),
        compiler_params=_cparams(("parallel", "arbitrary")),
        name="ffn_act",
    )(xb, w_gate, w_up)


def _resid_kernel(x_ref, xb_ref, p_ref, wp_ref, wpg_ref, o_ref):
    ple = _dot(p_ref[...], wp_ref[...]) * _sigmoid(_dot(xb_ref[...], wpg_ref[...]))
    o_ref[...] = ALPHA * x_ref[...] + ple


def _resid(x1, x1b, pb, w_ple, w_ple_gate, tm, tn):
    m, d = x1.shape
    kp = pb.shape[1]
    return pl.pallas_call(
        _resid_kernel,
        out_shape=jax.ShapeDtypeStruct((m, d), F32),
        grid=(m // tm, d // tn),
        in_specs=[pl.BlockSpec((tm, tn), lambda i, j: (i, j)),
                  pl.BlockSpec((tm, d), lambda i, j: (i, 0)),
                  pl.BlockSpec((tm, kp), lambda i, j: (i, 0)),
                  pl.BlockSpec((kp, tn), lambda i, j: (0, j)),
                  pl.BlockSpec((d, tn), lambda i, j: (0, j))],
        out_specs=pl.BlockSpec((tm, tn), lambda i, j: (i, j)),
        compiler_params=_cparams(("parallel", "arbitrary")),
        name="ple_resid",
    )(x1, x1b, pb, w_ple, w_ple_gate)


def _ffn_out_kernel(act_ref, w_ref, r_ref, g_ref, b_ref, o_ref, ob_ref, acc_ref):
    kk = pl.program_id(1)

    @pl.when(kk == 0)
    def _():
        acc_ref[...] = r_ref[...]

    acc_ref[...] += _dot(act_ref[...], w_ref[...])

    @pl.when(kk == pl.num_programs(1) - 1)
    def _():
        out = _layer_norm(acc_ref[...], g_ref[...], b_ref[...])
        o_ref[...] = out
        ob_ref[...] = out.astype(BF16)


def _ffn_out(act, w_down, resid, g, b, tm, tk):
    m, kf = act.shape
    d = w_down.shape[1]
    return pl.pallas_call(
        _ffn_out_kernel,
        out_shape=(jax.ShapeDtypeStruct((m, d), F32), jax.ShapeDtypeStruct((m, d), BF16)),
        grid=(m // tm, kf // tk),
        in_specs=[pl.BlockSpec((tm, tk), lambda i, k: (i, k)),
                  pl.BlockSpec((tk, d), lambda i, k: (k, 0)),
                  pl.BlockSpec((tm, d), lambda i, k: (i, 0)),
                  pl.BlockSpec((1, d), lambda i, k: (0, 0)),
                  pl.BlockSpec((1, d), lambda i, k: (0, 0))],
        out_specs=(pl.BlockSpec((tm, d), lambda i, k: (i, 0)),
                   pl.BlockSpec((tm, d), lambda i, k: (i, 0))),
        scratch_shapes=[pltpu.VMEM((tm, d), F32)],
        compiler_params=_cparams(("parallel", "arbitrary")),
        name="ffn_out_ln",
    )(act, w_down, resid, g.reshape(1, d), b.reshape(1, d))


DN_TS = 256


def _dn_kernel(alog_ref, dtb_ref,
               q_ref, k_ref, v_ref, z_ref, qp_ref, kp_ref, vp_ref, hs_ref,
               cwq_ref, cwk_ref, cwv_ref, nw_ref, o_ref, state_ref):
    h = pl.program_id(1)
    s = pl.program_id(2)
    c = DN_CHUNK

    @pl.when(s == 0)
    def _():
        state_ref[...] = jnp.zeros_like(state_ref)

    def conv_silu(x_ref, xp_ref, cw_ref):
        prev = jnp.where(s == 0, 0.0, xp_ref[...])
        xx = jnp.concatenate([prev, x_ref[...]], axis=0)
        cw = cw_ref[...]
        y = xx[SUBLANES:] * cw[DN_CONV - 1:DN_CONV]
        for i in range(DN_CONV - 1):
            y = y + pltpu.roll(xx, DN_CONV - 1 - i, axis=0)[SUBLANES:] * cw[i:i + 1]
        return _silu(y)

    def l2norm(x):
        return x * lax.rsqrt(jnp.sum(x * x, axis=-1, keepdims=True) + NORM_EPS)

    q_all = l2norm(conv_silu(q_ref, qp_ref, cwq_ref)) * (HEAD_DIM ** -0.5)
    k_all = l2norm(conv_silu(k_ref, kp_ref, cwk_ref))
    v_all = conv_silu(v_ref, vp_ref, cwv_ref)

    hs = hs_ref[...]
    beta_all = _sigmoid(_lane_col(hs, SC_BETA + h))
    a_all = _lane_col(hs, SC_DECAY + h) + dtb_ref[h]
    softplus = jnp.maximum(a_all, 0.0) + jnp.log(1.0 + jnp.exp(-jnp.abs(a_all)))
    g_all = -jnp.exp(jnp.zeros_like(a_all) + alog_ref[h]) * softplus

    row = lax.broadcasted_iota(jnp.int32, (c, c), 0)
    col = lax.broadcasted_iota(jnp.int32, (c, c), 1)
    causal = row >= col
    strict = row > col
    tri = jnp.where(causal, 1.0, 0.0).astype(F32)
    eye = jnp.where(row == col, 1.0, 0.0).astype(F32)
    ones8 = jnp.ones((SUBLANES, c), F32)

    prepped = []
    for ci in range(DN_TS // c):
        sl = slice(ci * c, (ci + 1) * c)
        q, k, v = q_all[sl], k_all[sl], v_all[sl]
        beta = beta_all[sl]
        gb = jnp.broadcast_to(g_all[sl], (c, LANES))
        gc = jnp.dot(tri, gb, precision=HI, preferred_element_type=F32)
        gc_row = jnp.dot(ones8, jnp.where(row <= col, gb[:, :c], 0.0),
                         precision=HI, preferred_element_type=F32)[0:1]
        decay = jnp.exp(jnp.where(causal, gc[:, :c] - gc_row, NEG_INF))
        kb = k * beta
        kbf = k.astype(BF16)
        lower = jnp.where(strict, _dot_nt(kb.astype(BF16), kbf) * decay, 0.0)
        pw = lower
        tinv = eye - lower
        for _ in range(5):
            pw = jnp.dot(pw, pw, precision=HI, preferred_element_type=F32)
            tinv = jnp.dot(tinv, eye + pw, precision=HI, preferred_element_type=F32)
        eg = jnp.exp(gc)
        rhs = jnp.concatenate([v * beta, kb * eg], axis=1).astype(BF16)
        uw = _dot(tinv.astype(BF16), rhs)
        u, w = uw[:, :LANES], uw[:, LANES:]
        qk = jnp.where(causal, _dot_nt(q.astype(BF16), kbf) * decay, 0.0)
        g_last = gc[c - 1:c]
        q_dec = q * eg
        k_dec = k * jnp.exp(g_last - gc)
        prepped.append((u, w.astype(BF16), qk.astype(BF16), q_dec.astype(BF16),
                        k_dec.astype(BF16), eg[c - 1:c]))

    state = state_ref[...]
    nw = nw_ref[...]
    for ci, (u, w, qk, q_dec, k_dec, eg_last) in enumerate(prepped):
        sl = slice(ci * c, (ci + 1) * c)
        sb = state.astype(BF16)
        v_new = u - _dot(w, sb)
        vb = v_new.astype(BF16)
        o = _dot(q_dec, sb) + _dot(qk, vb)
        state = state * eg_last + _dot_tn(k_dec, vb)
        o = o * lax.rsqrt(jnp.mean(o * o, axis=-1, keepdims=True) + NORM_EPS) * nw
        o_ref[sl, :] = (o * _silu(z_ref[sl, :])).astype(o_ref.dtype)
    state_ref[...] = state


def _deltanet(h_main, h_small, conv_w, a_log, dt_bias, norm_w):
    b_, s_len, _ = h_main.shape
    ts = DN_TS
    blk = lambda cb: pl.BlockSpec((None, ts, LANES), lambda b, h, s: (b, s, cb + h))
    prev = lambda cb: pl.BlockSpec(
        (None, SUBLANES, LANES),
        lambda b, h, s: (b, jnp.maximum(s * (ts // SUBLANES) - 1, 0), cb + h))
    cw = lambda cb: pl.BlockSpec((DN_CONV, LANES), lambda b, h, s: (0, cb + h))
    smem = pl.BlockSpec(memory_space=pltpu.SMEM)
    return pl.pallas_call(
        _dn_kernel,
        out_shape=jax.ShapeDtypeStruct((b_, s_len, HEADS * HEAD_DIM), BF16),
        grid=(b_, HEADS, s_len // ts),
        in_specs=[smem, smem,
                  blk(CB_DN_Q), blk(CB_DN_K), blk(CB_DN_V), blk(CB_DN_Z),
                  prev(CB_DN_Q), prev(CB_DN_K), prev(CB_DN_V),
                  pl.BlockSpec((None, ts, LANES), lambda b, h, s: (b, s, 0)),
                  cw(CB_DN_Q), cw(CB_DN_K), cw(CB_DN_V),
                  pl.BlockSpec((1, LANES), lambda b, h, s: (0, 0))],
        out_specs=pl.BlockSpec((None, ts, LANES), lambda b, h, s: (b, s, h)),
        scratch_shapes=[pltpu.VMEM((HEAD_DIM, HEAD_DIM), F32)],
        compiler_params=_cparams(("parallel", "parallel", "arbitrary")),
        name="deltanet",
    )(a_log, dt_bias, h_main, h_main, h_main, h_main, h_main, h_main, h_main, h_small,
      conv_w, conv_w, conv_w, norm_w.reshape(1, LANES))


def _rope(x, cosf, sinf):
    return x * cosf + pltpu.roll(x, HEAD_DIM // 2, axis=1) * sinf


def _nsa_prep_kernel(kc_ref, vc_ref, ks_ref, vs_ref, kw_ref, vw_ref, cos_ref, sin_ref,
                     pek_ref, w1k_ref, w2k_ref, pev_ref, w1v_ref, w2v_ref,
                     kso_ref, vso_ref, kwo_ref, vwo_ref, kco_ref, vco_ref, buf_ref):
    cosf = cos_ref[...]
    sinf = sin_ref[...]
    kso_ref[...] = _rope(ks_ref[...], cosf, sinf).astype(BF16)
    kwo_ref[...] = _rope(kw_ref[...], cosf, sinf).astype(BF16)
    vso_ref[...] = vs_ref[...].astype(BF16)
    vwo_ref[...] = vw_ref[...].astype(BF16)
    nch = buf_ref.shape[0] // CMP_STRIDE

    def compress(pe_ref, w1_ref, w2_ref, out_ref):
        a0 = jnp.zeros((nch, CMP_HIDDEN), F32)
        a1 = jnp.zeros((nch, CMP_HIDDEN), F32)
        for i in range(CMP_STRIDE):
            xi = buf_ref[pl.ds(i, nch, stride=CMP_STRIDE), :]
            lo = (xi + pe_ref[i:i + 1, :]).astype(BF16)
            hi = (xi + pe_ref[CMP_STRIDE + i:CMP_STRIDE + i + 1, :]).astype(BF16)
            a0 = a0 + _dot(lo, w1_ref[i * HEAD_DIM:(i + 1) * HEAD_DIM, :])
            a1 = a1 + _dot(hi, w1_ref[(CMP_STRIDE + i) * HEAD_DIM:(CMP_STRIDE + i + 1) * HEAD_DIM, :])
        hid = a0 + pltpu.roll(a1, nch - 1, axis=0)
        out_ref[...] = _dot(_silu(hid).astype(BF16), w2_ref[...]).astype(out_ref.dtype)

    buf_ref[...] = _rope(kc_ref[...], cosf, sinf)
    compress(pek_ref, w1k_ref, w2k_ref, kco_ref)
    buf_ref[...] = vc_ref[...]
    compress(pev_ref, w1v_ref, w2v_ref, vco_ref)


def _nsa_prep(h_main, cosf, sinf, pe_k, w1_k, w2_k, pe_v, w1_v, w2_v):
    b_, s_len, _ = h_main.shape
    nch = s_len // CMP_STRIDE
    kv = lambda i: pl.BlockSpec((None, s_len, LANES), lambda b, g: (b, 0, CB_KV + 2 * i + g))
    full = lambda shape: pl.BlockSpec(shape, lambda b, g: tuple(0 for _ in shape))
    seq_out = pl.BlockSpec((None, None, s_len, LANES), lambda b, g: (b, g, 0, 0))
    cmp_out = pl.BlockSpec((None, None, nch, LANES), lambda b, g: (b, g, 0, 0))
    seq_shape = jax.ShapeDtypeStruct((b_, GROUPS, s_len, HEAD_DIM), BF16)
    cmp_shape = jax.ShapeDtypeStruct((b_, GROUPS, nch, HEAD_DIM), BF16)
    return pl.pallas_call(
        _nsa_prep_kernel,
        out_shape=(seq_shape, seq_shape, seq_shape, seq_shape, cmp_shape, cmp_shape),
        grid=(b_, GROUPS),
        in_specs=[kv(0), kv(1), kv(2), kv(3), kv(4), kv(5),
                  full((s_len, LANES)), full((s_len, LANES)),
                  full((CMP_LEN, HEAD_DIM)), full((CMP_LEN * HEAD_DIM, CMP_HIDDEN)),
                  full((CMP_HIDDEN, HEAD_DIM)),
                  full((CMP_LEN, HEAD_DIM)), full((CMP_LEN * HEAD_DIM, CMP_HIDDEN)),
                  full((CMP_HIDDEN, HEAD_DIM))],
        out_specs=(seq_out, seq_out, seq_out, seq_out, cmp_out, cmp_out),
        scratch_shapes=[pltpu.VMEM((s_len, HEAD_DIM), F32)],
        compiler_params=_cparams(("parallel", "parallel")),
        name="nsa_prep",
    )(h_main, h_main, h_main, h_main, h_main, h_main, cosf, sinf,
      pe_k, w1_k, w2_k, pe_v, w1_v, w2_v)


NSA_TQ = 128
NSA_TK = 128


def _nsa_attn_kernel(q_ref, hs_ref, cos_ref, sin_ref, kc_ref, vc_ref, ks_ref, vs_ref,
                     kw_ref, vw_ref, ov_ref, ex_ref, o_ref):
    g = pl.program_id(1)
    qi = pl.program_id(2)
    tq, tk = NSA_TQ, NSA_TK
    rows = HPG * tq
    cosf = cos_ref[...]
    sinf = sin_ref[...]
    scale = HEAD_DIM ** -0.5
    qs = jnp.concatenate(
        [_rope(q_ref[:, hh * HEAD_DIM:(hh + 1) * HEAD_DIM], cosf, sinf) * scale
         for hh in range(HPG)], axis=0).astype(BF16)

    t_abs = qi * tq + lax.broadcasted_iota(jnp.int32, (tq, LANES), 0)
    lane = lax.broadcasted_iota(jnp.int32, (tq, LANES), 1)

    def tile_heads(x):
        return jnp.concatenate([x] * HPG, axis=0)

    mask_c = tile_heads(lane * CMP_STRIDE + (CMP_LEN - 1) <= t_abs)
    s_c = _dot_nt(qs, kc_ref[...])
    m_c = jnp.max(jnp.where(mask_c, s_c, NEG_INF), axis=-1, keepdims=True)
    e_c = jnp.where(mask_c, jnp.exp(s_c - m_c), 0.0)
    l_c = jnp.sum(e_c, axis=-1, keepdims=True)
    p_c = jnp.where(l_c > 0.0, e_c / l_c, 0.0)
    o_c = _dot(p_c.astype(BF16), vc_ref[...])

    psum = p_c[0:tq]
    for hh in range(1, HPG):
        psum = psum + p_c[hh * tq:(hh + 1) * tq]
    p_hi = psum.astype(BF16)
    p_lo = (psum - p_hi.astype(F32)).astype(BF16)
    ov = ov_ref[...]
    imp = _dot(p_hi, ov) + _dot(p_lo, ov)
    ns = ex_ref.shape[0] * (tk // SEL_BLOCK)
    cur = t_abs // SEL_BLOCK
    forced = (lane == 0) | (lane == cur) | (lane == cur - 1)
    imp = jnp.where(forced, jnp.inf, jnp.where(lane <= cur, imp, -jnp.inf))
    imp = jnp.where(lane < ns, imp, -jnp.inf)
    rank = jnp.zeros((tq, LANES), F32)
    for i in range(ns):
        ci = jnp.sum(jnp.where(lane == i, imp, 0.0), axis=1, keepdims=True)
        ci = jnp.where(jnp.isnan(ci), jnp.inf, ci)
        before = (ci > imp) | ((ci == imp) & (lane > i))
        rank = rank + jnp.where(before, 1.0, 0.0)
    sel = jnp.where((rank < float(min(SEL_TOPK, ns))) & (lane < ns), 1.0, 0.0).astype(BF16)

    def attend(carry, k_blk, v_blk, mask):
        m_prev, l_prev, acc = carry
        s_blk = _dot_nt(qs, k_blk)
        m_new = jnp.maximum(m_prev, jnp.max(jnp.where(mask, s_blk, NEG_INF), axis=-1, keepdims=True))
        e_blk = jnp.where(mask, jnp.exp(s_blk - m_new), 0.0)
        alpha = jnp.exp(m_prev - m_new)
        l_new = alpha * l_prev + jnp.sum(e_blk, axis=-1, keepdims=True)
        acc = alpha * acc + _dot(e_blk.astype(BF16), v_blk)
        return m_new, l_new, acc

    init = (jnp.full((rows, 1), NEG_INF, F32), jnp.zeros((rows, 1), F32),
            jnp.zeros((rows, HEAD_DIM), F32))

    def sel_body(kb, carry):
        start = pl.multiple_of(kb * tk, tk)
        picked = _dot(sel, ex_ref[kb]) > 0.5
        kpos = kb * tk + lane
        mask = tile_heads(picked & (kpos <= t_abs))
        return attend(carry, ks_ref[pl.ds(start, tk), :], vs_ref[pl.ds(start, tk), :], mask)

    _, l_s, acc_s = lax.fori_loop(0, qi + 1, sel_body, init)
    o_s = acc_s / l_s

    carry = init
    for d in range(WIN // tk + 1):
        kb = qi - WIN // tk + d
        kbc = jnp.maximum(kb, 0)
        start = pl.multiple_of(kbc * tk, tk)
        diff = t_abs - (kbc * tk + lane)
        mask = tile_heads((diff >= 0) & (diff < WIN) & (kb >= 0))
        carry = attend(carry, kw_ref[pl.ds(start, tk), :], vw_ref[pl.ds(start, tk), :], mask)
    _, l_w, acc_w = carry
    o_w = acc_w / l_w

    hs = hs_ref[...]
    for hh in range(HPG):
        gbase = SC_GATE + (g * HPG + hh) * 3
        r = slice(hh * tq, (hh + 1) * tq)
        out = (_sigmoid(_lane_col(hs, gbase)) * o_c[r]
               + _sigmoid(_lane_col(hs, gbase + 1)) * o_s[r]
               + _sigmoid(_lane_col(hs, gbase + 2)) * o_w[r])
        o_ref[:, hh * HEAD_DIM:(hh + 1) * HEAD_DIM] = out.astype(o_ref.dtype)


def _nsa_attn(h_main, h_small, cosf, sinf, kc, vc, ks, vs, kw, vw, overlap, expand):
    b_, s_len, _ = h_main.shape
    tq = NSA_TQ
    nch = kc.shape[2]
    qw = HPG * HEAD_DIM
    seq = pl.BlockSpec((None, None, s_len, HEAD_DIM), lambda b, g, i: (b, g, 0, 0))
    cmp_ = pl.BlockSpec((None, None, nch, HEAD_DIM), lambda b, g, i: (b, g, 0, 0))
    return pl.pallas_call(
        _nsa_attn_kernel,
        out_shape=jax.ShapeDtypeStruct((b_, s_len, HEADS * HEAD_DIM), BF16),
        grid=(b_, GROUPS, s_len // tq),
        in_specs=[pl.BlockSpec((None, tq, qw), lambda b, g, i: (b, i, CB_NSA_Q * LANES // qw + g)),
                  pl.BlockSpec((None, tq, LANES), lambda b, g, i: (b, i, 0)),
                  pl.BlockSpec((tq, LANES), lambda b, g, i: (i, 0)),
                  pl.BlockSpec((tq, LANES), lambda b, g, i: (i, 0)),
                  cmp_, cmp_, seq, seq, seq, seq,
                  pl.BlockSpec(overlap.shape, lambda b, g, i: (0, 0)),
                  pl.BlockSpec(expand.shape, lambda b, g, i: (0, 0, 0))],
        out_specs=pl.BlockSpec((None, tq, qw), lambda b, g, i: (b, i, g)),
        compiler_params=_cparams(("parallel", "parallel", "arbitrary")),
        name="nsa_attn",
    )(h_main, h_small, cosf, sinf, kc, vc, ks, vs, kw, vw, overlap, expand)


def _nsa_constants(s_len):
    half = HEAD_DIM // 2
    inv_freq = ROPE_THETA ** (-jnp.arange(half, dtype=F32) / half)
    ang = jnp.arange(s_len, dtype=F32)[:, None] * inv_freq[None, :]
    cos, sin = jnp.cos(ang), jnp.sin(ang)
    cosf = jnp.concatenate([cos, cos], axis=-1)
    sinf = jnp.concatenate([-sin, sin], axis=-1)
    nch = s_len // CMP_STRIDE
    ns = s_len // SEL_BLOCK
    n = jnp.arange(nch)[:, None] * CMP_STRIDE
    j = jnp.arange(LANES)[None, :] * SEL_BLOCK
    overlap = ((n <= j + SEL_BLOCK - 1) & (n + CMP_LEN - 1 >= j)
               & (jnp.arange(nch)[:, None] < nch - CMP_LEN // CMP_STRIDE + 1)
               & (jnp.arange(LANES)[None, :] < ns)).astype(BF16)
    per = NSA_TK // SEL_BLOCK
    blk = jnp.arange(LANES)[None, :, None]
    key = jnp.arange(NSA_TK)[None, None, :]
    kb = jnp.arange(s_len // NSA_TK)[:, None, None]
    expand = (blk == kb * per + key // SEL_BLOCK).astype(BF16)
    return cosf, sinf, overlap, expand


def _layer(x, xb, p_i, w_in, conv_w, a_log, dt_bias, norm_w, pe_k, w1_k, w2_k, pe_v, w1_v, w2_v,
           w_a, w_b, w_out, ln1_g, ln1_b, w_gate, w_up, w_down, w_ple, w_ple_gate, ln2_g, ln2_b,
           consts):
    b_, s_len, d = x.shape
    t = b_ * s_len
    cosf, sinf, overlap, expand = consts
    x2 = x.reshape(t, d)
    xb2 = xb.reshape(t, d)

    w_main = jnp.concatenate([w_in[:, :4096], w_in[:, 4112:6672], w_in[:, 6696:]], axis=1).astype(BF16)
    w_small = jnp.concatenate([w_in[:, 4096:4112], w_in[:, 6672:6696],
                               jnp.zeros((d, LANES - 40), w_in.dtype)], axis=1).astype(BF16)
    tm = min(1024, t)
    h_main = _matmul(xb2, w_main, F32, tm, 512, "proj_main")
    h_small = _matmul(xb2, w_small, F32, tm, LANES, "proj_small")
    h_main3 = h_main.reshape(b_, s_len, N_MAIN)
    h_small3 = h_small.reshape(b_, s_len, LANES)

    o_a = _deltanet(h_main3, h_small3, conv_w, a_log, dt_bias, norm_w)
    ks, vs, kw, vw, kc, vc = _nsa_prep(h_main3, cosf, sinf, pe_k, w1_k.astype(BF16),
                                       w2_k.astype(BF16), pe_v, w1_v.astype(BF16),
                                       w2_v.astype(BF16))
    o_b = _nsa_attn(h_main3, h_small3, cosf, sinf, kc, vc, ks, vs, kw, vw, overlap, expand)

    mixed = _merge(o_a.reshape(t, -1), o_b.reshape(t, -1), w_a.astype(BF16), w_b.astype(BF16),
                   h_main, tm, 512)
    x1, x1b = _outproj_ln(mixed, w_out.astype(BF16), x2, ln1_g, ln1_b, min(256, t))
    act = _ffn_act(x1b, w_gate.astype(BF16), w_up.astype(BF16), tm, 512)
    resid = _resid(x1, x1b, p_i.reshape(t, PLE_DIM).astype(BF16), w_ple.astype(BF16),
                   w_ple_gate.astype(BF16), tm, 512)
    y, yb = _ffn_out(act, w_down.astype(BF16), resid, ln2_g, ln2_b, min(512, t), 512)
    return y.reshape(b_, s_len, d), yb.reshape(b_, s_len, d)


def kernel(x, p, w_in, dn_conv_w, dn_a_log, dn_dt_bias, dn_norm_w, cmp_pe_k, cmp_w1_k, cmp_w2_k, cmp_pe_v, cmp_w1_v, cmp_w2_v, w_branch_a, w_branch_b, w_out, ln1_g, ln1_b, w_ffn_gate, w_ffn_up, w_ffn_down, w_ple, w_ple_gate, ln2_g, ln2_b):
    consts = _nsa_constants(x.shape[1])
    xb = x.astype(BF16)
    for i in range(DEPTH):
        x, xb = _layer(x, xb, p[i], w_in[i], dn_conv_w[i], dn_a_log[i], dn_dt_bias[i], dn_norm_w[i],
                       cmp_pe_k[i], cmp_w1_k[i], cmp_w2_k[i], cmp_pe_v[i], cmp_w1_v[i], cmp_w2_v[i],
                       w_branch_a[i], w_branch_b[i], w_out[i], ln1_g[i], ln1_b[i],
                       w_ffn_gate[i], w_ffn_up[i], w_ffn_down[i], w_ple[i], w_ple_gate[i],
                       ln2_g[i], ln2_b[i], consts)
    return x
```

```python
import jax
import jax.numpy as jnp
from jax import lax
from jax.experimental import pallas as pl
from jax.experimental.pallas import tpu as pltpu

D_MODEL = 2048
DEPTH = 2
HEAD_DIM = 128
HEADS = 8
DN_CONV = 4
DN_CHUNK = 64
GROUPS = 2
HPG = HEADS // GROUPS
CMP_LEN = 32
CMP_STRIDE = 16
CMP_HIDDEN = 256
SEL_BLOCK = 64
SEL_TOPK = 16
WIN = 512
ROPE_THETA = 10000.0
D_FF = 5632
PLE_DIM = 256
ALPHA = (2.0 * DEPTH) ** 0.25
LN_EPS = 1e-5
NORM_EPS = 1e-6
NEG_INF = -1e30

LANES = 128
SUBLANES = 8
VMEM_LIMIT = 48 * 1024 * 1024

CB_DN_Q, CB_DN_K, CB_DN_V, CB_DN_Z = 0, 8, 16, 24
CB_NSA_Q = 32
CB_KV = 40
N_MAIN = 10752
COL_MERGE_A = 6656
COL_MERGE_B = 8704
SC_BETA, SC_DECAY, SC_GATE = 0, 8, 16

F32 = jnp.float32
BF16 = jnp.bfloat16


def _cparams(sem):
    return pltpu.CompilerParams(dimension_semantics=sem, vmem_limit_bytes=VMEM_LIMIT)


def _dot(a, b):
    return jnp.dot(a, b, preferred_element_type=F32)


def _dot_nt(a, b):
    return lax.dot_general(a, b, (((1,), (1,)), ((), ())), preferred_element_type=F32)


def _dot_tn(a, b):
    return lax.dot_general(a, b, (((0,), (0,)), ((), ())), preferred_element_type=F32)


def _sigmoid(x):
    return 1.0 / (1.0 + jnp.exp(-x))


def _silu(x):
    return x * _sigmoid(x)


def _layer_norm(y, g, b):
    mu = jnp.mean(y, axis=-1, keepdims=True)
    d = y - mu
    var = jnp.mean(d * d, axis=-1, keepdims=True)
    return d * lax.rsqrt(var + LN_EPS) * g + b


def _lane_col(x, idx):
    lane = lax.broadcasted_iota(jnp.int32, x.shape, 1)
    return jnp.sum(jnp.where(lane == idx, x, 0.0), axis=1, keepdims=True)


def _split2(x):
    hi = x.astype(BF16)
    return hi, (x - hi.astype(F32)).astype(BF16)


def _split3(x):
    x1 = x.astype(BF16)
    r = x - x1.astype(F32)
    x2 = r.astype(BF16)
    return x1, x2, (r - x2.astype(F32)).astype(BF16)


def _dot_01(ones_b, x):
    x1, x2, x3 = _split3(x)
    return _dot(ones_b, x1) + _dot(ones_b, x2) + _dot(ones_b, x3)


def _dot_x3(a, b):
    a_hi, a_lo = _split2(a)
    b_hi, b_lo = _split2(b)
    return _dot(a_hi, b_hi) + _dot(a_lo, b_hi) + _dot(a_hi, b_lo)


def _mm_kernel(a_ref, w_ref, o_ref):
    o_ref[...] = _dot(a_ref[...], w_ref[...]).astype(o_ref.dtype)


def _matmul(a, w, out_dtype, tm, tn, name):
    m, k = a.shape
    n = w.shape[1]
    return pl.pallas_call(
        _mm_kernel,
        out_shape=jax.ShapeDtypeStruct((m, n), out_dtype),
        grid=(m // tm, n // tn),
        in_specs=[pl.BlockSpec((tm, k), lambda i, j: (i, 0)),
                  pl.BlockSpec((k, tn), lambda i, j: (0, j))],
        out_specs=pl.BlockSpec((tm, tn), lambda i, j: (i, j)),
        compiler_params=_cparams(("parallel", "arbitrary")),
        name=name,
    )(a, w)


def _merge_kernel(oa_ref, ob_ref, wa_ref, wb_ref, ma_ref, mb_ref, o_ref):
    ya = _dot(oa_ref[...], wa_ref[...])
    yb = _dot(ob_ref[...], wb_ref[...])
    o_ref[...] = (_sigmoid(ma_ref[...]) * ya + _sigmoid(mb_ref[...]) * yb).astype(o_ref.dtype)


def _merge(o_a, o_b, w_a, w_b, h_main, tm, tn):
    m, k = o_a.shape
    n = w_a.shape[1]
    ca, cb = COL_MERGE_A // tn, COL_MERGE_B // tn
    return pl.pallas_call(
        _merge_kernel,
        out_shape=jax.ShapeDtypeStruct((m, n), BF16),
        grid=(m // tm, n // tn),
        in_specs=[pl.BlockSpec((tm, k), lambda i, j: (i, 0)),
                  pl.BlockSpec((tm, k), lambda i, j: (i, 0)),
                  pl.BlockSpec((k, tn), lambda i, j: (0, j)),
                  pl.BlockSpec((k, tn), lambda i, j: (0, j)),
                  pl.BlockSpec((tm, tn), lambda i, j: (i, ca + j)),
                  pl.BlockSpec((tm, tn), lambda i, j: (i, cb + j))],
        out_specs=pl.BlockSpec((tm, tn), lambda i, j: (i, j)),
        compiler_params=_cparams(("parallel", "arbitrary")),
        name="merge",
    )(o_a, o_b, w_a, w_b, h_main, h_main)


def _outproj_ln_kernel(mx_ref, w_ref, x_ref, g_ref, b_ref, o_ref, ob_ref):
    y = ALPHA * x_ref[...] + _dot(mx_ref[...], w_ref[...])
    out = _layer_norm(y, g_ref[...], b_ref[...])
    o_ref[...] = out
    ob_ref[...] = out.astype(BF16)


def _outproj_ln(mixed, w_out, x, g, b, tm):
    m, d = x.shape
    return pl.pallas_call(
        _outproj_ln_kernel,
        out_shape=(jax.ShapeDtypeStruct((m, d), F32), jax.ShapeDtypeStruct((m, d), BF16)),
        grid=(m // tm,),
        in_specs=[pl.BlockSpec((tm, d), lambda i: (i, 0)),
                  pl.BlockSpec((d, d), lambda i: (0, 0)),
                  pl.BlockSpec((tm, d), lambda i: (i, 0)),
                  pl.BlockSpec((1, d), lambda i: (0, 0)),
                  pl.BlockSpec((1, d), lambda i: (0, 0))],
        out_specs=(pl.BlockSpec((tm, d), lambda i: (i, 0)),
                   pl.BlockSpec((tm, d), lambda i: (i, 0))),
        compiler_params=_cparams(("parallel",)),
        name="outproj_ln",
    )(mixed, w_out, x, g.reshape(1, d), b.reshape(1, d))


def _ffn_act_kernel(x_ref, wg_ref, wu_ref, o_ref):
    xv = x_ref[...]
    o_ref[...] = (_silu(_dot(xv, wg_ref[...])) * _dot(xv, wu_ref[...])).astype(o_ref.dtype)


def _ffn_act(xb, w_gate, w_up, tm, tn):
    m, k = xb.shape
    n = w_gate.shape[1]
    return pl.pallas_call(
        _ffn_act_kernel,
        out_shape=jax.ShapeDtypeStruct((m, n), BF16),
        grid=(m // tm, n // tn),
        in_specs=[pl.BlockSpec((tm, k), lambda i, j: (i, 0)),
                  pl.BlockSpec((k, tn), lambda i, j: (0, j)),
                  pl.BlockSpec((k, tn), lambda i, j: (0, j))],
        out_specs=pl.BlockSpec((tm, tn), lambda i, j: (i, j)),
        compiler_params=_cparams(("parallel", "arbitrary")),
        name="ffn_act",
    )(xb, w_gate, w_up)


def _resid_kernel(x_ref, xb_ref, p_ref, wp_ref, wpg_ref, o_ref):
    ple = _dot(p_ref[...], wp_ref[...]) * _sigmoid(_dot(xb_ref[...], wpg_ref[...]))
    o_ref[...] = ALPHA * x_ref[...] + ple


def _resid(x1, x1b, pb, w_ple, w_ple_gate, tm, tn):
    m, d = x1.shape
    kp = pb.shape[1]
    return pl.pallas_call(
        _resid_kernel,
        out_shape=jax.ShapeDtypeStruct((m, d), F32),
        grid=(m // tm, d // tn),
        in_specs=[pl.BlockSpec((tm, tn), lambda i, j: (i, j)),
                  pl.BlockSpec((tm, d), lambda i, j: (i, 0)),
                  pl.BlockSpec((tm, kp), lambda i, j: (i, 0)),
                  pl.BlockSpec((kp, tn), lambda i, j: (0, j)),
                  pl.BlockSpec((d, tn), lambda i, j: (0, j))],
        out_specs=pl.BlockSpec((tm, tn), lambda i, j: (i, j)),
        compiler_params=_cparams(("parallel", "arbitrary")),
        name="ple_resid",
    )(x1, x1b, pb, w_ple, w_ple_gate)


def _ffn_out_kernel(act_ref, w_ref, r_ref, g_ref, b_ref, o_ref, ob_ref, acc_ref):
    kk = pl.program_id(1)

    @pl.when(kk == 0)
    def _():
        acc_ref[...] = r_ref[...]

    acc_ref[...] += _dot(act_ref[...], w_ref[...])

    @pl.when(kk == pl.num_programs(1) - 1)
    def _():
        out = _layer_norm(acc_ref[...], g_ref[...], b_ref[...])
        o_ref[...] = out
        ob_ref[...] = out.astype(BF16)


def _ffn_out(act, w_down, resid, g, b, tm, tk):
    m, kf = act.shape
    d = w_down.shape[1]
    return pl.pallas_call(
        _ffn_out_kernel,
        out_shape=(jax.ShapeDtypeStruct((m, d), F32), jax.ShapeDtypeStruct((m, d), BF16)),
        grid=(m // tm, kf // tk),
        in_specs=[pl.BlockSpec((tm, tk), lambda i, k: (i, k)),
                  pl.BlockSpec((tk, d), lambda i, k: (k, 0)),
                  pl.BlockSpec((tm, d), lambda i, k: (i, 0)),
                  pl.BlockSpec((1, d), lambda i, k: (0, 0)),
                  pl.BlockSpec((1, d), lambda i, k: (0, 0))],
        out_specs=(pl.BlockSpec((tm, d), lambda i, k: (i, 0)),
                   pl.BlockSpec((tm, d), lambda i, k: (i, 0))),
        scratch_shapes=[pltpu.VMEM((tm, d), F32)],
        compiler_params=_cparams(("parallel", "arbitrary")),
        name="ffn_out_ln",
    )(act, w_down, resid, g.reshape(1, d), b.reshape(1, d))


DN_TS = 256
DN_HB = 4


def _dn_kernel(alog_ref, dtb_ref,
               q_ref, k_ref, v_ref, z_ref, qp_ref, kp_ref, vp_ref, hs_ref,
               cwq_ref, cwk_ref, cwv_ref, nw_ref, o_ref, state_ref):
    hg = pl.program_id(1)
    s = pl.program_id(2)
    c = DN_CHUNK
    d = HEAD_DIM

    @pl.when(s == 0)
    def _():
        state_ref[...] = jnp.zeros_like(state_ref)

    def conv_silu(x_ref, xp_ref, cw_ref, hb):
        cols = slice(hb * d, (hb + 1) * d)
        prev = jnp.where(s == 0, 0.0, xp_ref[:, cols])
        xx = jnp.concatenate([prev, x_ref[:, cols]], axis=0)
        cw = cw_ref[:, cols]
        y = xx[SUBLANES:] * cw[DN_CONV - 1:DN_CONV]
        for i in range(DN_CONV - 1):
            y = y + pltpu.roll(xx, DN_CONV - 1 - i, axis=0)[SUBLANES:] * cw[i:i + 1]
        return _silu(y)

    def l2norm(x):
        return x * lax.rsqrt(jnp.sum(x * x, axis=-1, keepdims=True) + NORM_EPS)

    row = lax.broadcasted_iota(jnp.int32, (c, c), 0)
    col = lax.broadcasted_iota(jnp.int32, (c, c), 1)
    causal = row >= col
    strict = row > col
    tri_b = jnp.where(causal, 1.0, 0.0).astype(BF16)
    eye = jnp.where(row == col, 1.0, 0.0).astype(F32)
    ones8_b = jnp.ones((SUBLANES, c), BF16)
    hs = hs_ref[...]
    nw = nw_ref[...]

    nck = DN_TS // c
    pairs = [(hb, ci) for hb in range(DN_HB) for ci in range(nck)]
    qs, ks, vs, betas, gbs = [], [], [], [], []
    for hb in range(DN_HB):
        h = hg * DN_HB + hb
        q_all = l2norm(conv_silu(q_ref, qp_ref, cwq_ref, hb)) * (d ** -0.5)
        k_all = l2norm(conv_silu(k_ref, kp_ref, cwk_ref, hb))
        v_all = conv_silu(v_ref, vp_ref, cwv_ref, hb)
        beta_all = _sigmoid(_lane_col(hs, SC_BETA + h))
        a_all = _lane_col(hs, SC_DECAY + h) + dtb_ref[h]
        softplus = jnp.maximum(a_all, 0.0) + jnp.log(1.0 + jnp.exp(-jnp.abs(a_all)))
        g_all = -jnp.exp(jnp.zeros_like(a_all) + alog_ref[h]) * softplus
        for ci in range(nck):
            sl = slice(ci * c, (ci + 1) * c)
            qs.append(q_all[sl])
            ks.append(k_all[sl])
            vs.append(v_all[sl])
            betas.append(beta_all[sl])
            gbs.append(jnp.broadcast_to(g_all[sl], (c, LANES)))
    n = len(pairs)
    gcs = [_dot_01(tri_b, gbs[i]) for i in range(n)]
    gc_rows = [_dot_01(ones8_b, jnp.where(row <= col, gbs[i][:, :c], 0.0))[0:1] for i in range(n)]
    decays = [jnp.exp(jnp.where(causal, gcs[i][:, :c] - gc_rows[i], NEG_INF)) for i in range(n)]
    kbs = [ks[i] * betas[i] for i in range(n)]
    kbfs = [ks[i].astype(BF16) for i in range(n)]
    negs = [jnp.where(strict, -(_dot_nt(kbs[i].astype(BF16), kbfs[i]) * decays[i]), 0.0)
            for i in range(n)]
    accs = [eye + negs[i] for i in range(n)]
    pws = [_dot_x3(negs[i], negs[i]) for i in range(n)]
    for _ in range(4):
        ress = [_dot_x3(pws[i], jnp.concatenate([pws[i], accs[i]], axis=1)) for i in range(n)]
        pws = [ress[i][:, :c] for i in range(n)]
        accs = [accs[i] + ress[i][:, c:] for i in range(n)]
    tinvs = [(accs[i] + _dot_x3(pws[i], accs[i])).astype(BF16) for i in range(n)]
    egs = [jnp.exp(gcs[i]) for i in range(n)]
    uws = [_dot(tinvs[i], jnp.concatenate([vs[i] * betas[i], kbs[i] * egs[i]], axis=1).astype(BF16))
           .astype(BF16) for i in range(n)]
    qks = [jnp.where(causal, _dot_nt(qs[i].astype(BF16), kbfs[i]) * decays[i], 0.0).astype(BF16)
           for i in range(n)]
    k_decs = [(ks[i] * jnp.exp(gcs[i][c - 1:c] - gcs[i])).astype(BF16) for i in range(n)]
    nbs = [_dot_tn(k_decs[i], uws[i]) for i in range(n)]
    prs = [_dot(qks[i], uws[i]) for i in range(n)]
    lhss = [jnp.concatenate([(qs[i] * egs[i] - prs[i][:, d:]).astype(BF16),
                             nbs[i][:, d:].astype(BF16)], axis=0) for i in range(n)]

    states = [state_ref[hb] for hb in range(DN_HB)]
    for ci in range(nck):
        sl = slice(ci * c, (ci + 1) * c)
        ress = [_dot(lhss[hb * nck + ci], states[hb].astype(BF16)) for hb in range(DN_HB)]
        for hb in range(DN_HB):
            i = hb * nck + ci
            cols = slice(hb * d, (hb + 1) * d)
            o = ress[hb][:c] + prs[i][:, :d]
            states[hb] = states[hb] * egs[i][c - 1:c] - ress[hb][c:] + nbs[i][:, :d]
            o = o * lax.rsqrt(jnp.mean(o * o, axis=-1, keepdims=True) + NORM_EPS) * nw
            o_ref[sl, cols] = (o * _silu(z_ref[sl, cols])).astype(o_ref.dtype)
    for hb in range(DN_HB):
        state_ref[hb] = states[hb]


def _deltanet(h_main, h_small, conv_w, a_log, dt_bias, norm_w):
    b_, s_len, _ = h_main.shape
    ts = DN_TS
    wd = DN_HB * HEAD_DIM
    ngrp = HEADS // DN_HB
    blk = lambda cb: pl.BlockSpec((None, ts, wd), lambda b, h, s: (b, s, cb // DN_HB + h))
    prev = lambda cb: pl.BlockSpec(
        (None, SUBLANES, wd),
        lambda b, h, s: (b, jnp.maximum(s * (ts // SUBLANES) - 1, 0), cb // DN_HB + h))
    cw = lambda cb: pl.BlockSpec((DN_CONV, wd), lambda b, h, s: (0, cb // DN_HB + h))
    smem = pl.BlockSpec(memory_space=pltpu.SMEM)
    return pl.pallas_call(
        _dn_kernel,
        out_shape=jax.ShapeDtypeStruct((b_, s_len, HEADS * HEAD_DIM), BF16),
        grid=(b_, ngrp, s_len // ts),
        in_specs=[smem, smem,
                  blk(CB_DN_Q), blk(CB_DN_K), blk(CB_DN_V), blk(CB_DN_Z),
                  prev(CB_DN_Q), prev(CB_DN_K), prev(CB_DN_V),
                  pl.BlockSpec((None, ts, LANES), lambda b, h, s: (b, s, 0)),
                  cw(CB_DN_Q), cw(CB_DN_K), cw(CB_DN_V),
                  pl.BlockSpec((1, LANES), lambda b, h, s: (0, 0))],
        out_specs=pl.BlockSpec((None, ts, wd), lambda b, h, s: (b, s, h)),
        scratch_shapes=[pltpu.VMEM((DN_HB, HEAD_DIM, HEAD_DIM), F32)],
        compiler_params=_cparams(("parallel", "parallel", "arbitrary")),
        name="deltanet",
    )(a_log, dt_bias, h_main, h_main, h_main, h_main, h_main, h_main, h_main, h_small,
      conv_w, conv_w, conv_w, norm_w.reshape(1, LANES))


def _rope(x, cosf, sinf):
    return x * cosf + pltpu.roll(x, HEAD_DIM // 2, axis=1) * sinf


def _nsa_prep_kernel(kc_ref, vc_ref, ks_ref, vs_ref, kw_ref, vw_ref, cos_ref, sin_ref,
                     pek_ref, w1k_ref, w2k_ref, pev_ref, w1v_ref, w2v_ref,
                     kso_ref, vso_ref, kwo_ref, vwo_ref, kco_ref, vco_ref, buf_ref):
    cosf = cos_ref[...]
    sinf = sin_ref[...]
    s_len = ks_ref.shape[0]
    pos = lax.broadcasted_iota(jnp.int32, (s_len, LANES), 0)
    lane = lax.broadcasted_iota(jnp.int32, (s_len, LANES), 1)
    kso_ref[:, :HEAD_DIM] = _rope(ks_ref[...], cosf, sinf).astype(BF16)
    kso_ref[:, HEAD_DIM:] = jnp.where(pos // SEL_BLOCK == lane, NEG_INF, 0.0).astype(BF16)
    kwo_ref[...] = _rope(kw_ref[...], cosf, sinf).astype(BF16)
    vso_ref[...] = vs_ref[...].astype(BF16)
    vwo_ref[...] = vw_ref[...].astype(BF16)
    nch = buf_ref.shape[0] // CMP_STRIDE

    def compress(pe_ref, w1_ref, w2_ref, out_ref):
        a0 = jnp.zeros((nch, CMP_HIDDEN), F32)
        a1 = jnp.zeros((nch, CMP_HIDDEN), F32)
        for i in range(CMP_STRIDE):
            xi = buf_ref[pl.ds(i, nch, stride=CMP_STRIDE), :]
            lo = (xi + pe_ref[i:i + 1, :]).astype(BF16)
            hi = (xi + pe_ref[CMP_STRIDE + i:CMP_STRIDE + i + 1, :]).astype(BF16)
            a0 = a0 + _dot(lo, w1_ref[i * HEAD_DIM:(i + 1) * HEAD_DIM, :])
            a1 = a1 + _dot(hi, w1_ref[(CMP_STRIDE + i) * HEAD_DIM:(CMP_STRIDE + i + 1) * HEAD_DIM, :])
        hid = a0 + pltpu.roll(a1, nch - 1, axis=0)
        out_ref[...] = _dot(_silu(hid).astype(BF16), w2_ref[...]).astype(out_ref.dtype)

    buf_ref[...] = _rope(kc_ref[...], cosf, sinf)
    compress(pek_ref, w1k_ref, w2k_ref, kco_ref)
    buf_ref[...] = vc_ref[...]
    compress(pev_ref, w1v_ref, w2v_ref, vco_ref)


def _nsa_prep(h_main, cosf, sinf, pe_k, w1_k, w2_k, pe_v, w1_v, w2_v):
    b_, s_len, _ = h_main.shape
    nch = s_len // CMP_STRIDE
    kv = lambda i: pl.BlockSpec((None, s_len, LANES), lambda b, g: (b, 0, CB_KV + 2 * i + g))
    full = lambda shape: pl.BlockSpec(shape, lambda b, g: tuple(0 for _ in shape))
    assert s_len // SEL_BLOCK <= LANES
    seq_out = pl.BlockSpec((None, None, s_len, LANES), lambda b, g: (b, g, 0, 0))
    aug_out = pl.BlockSpec((None, None, s_len, 2 * LANES), lambda b, g: (b, g, 0, 0))
    cmp_out = pl.BlockSpec((None, None, nch, LANES), lambda b, g: (b, g, 0, 0))
    seq_shape = jax.ShapeDtypeStruct((b_, GROUPS, s_len, HEAD_DIM), BF16)
    aug_shape = jax.ShapeDtypeStruct((b_, GROUPS, s_len, 2 * HEAD_DIM), BF16)
    cmp_shape = jax.ShapeDtypeStruct((b_, GROUPS, nch, HEAD_DIM), BF16)
    return pl.pallas_call(
        _nsa_prep_kernel,
        out_shape=(aug_shape, seq_shape, seq_shape, seq_shape, cmp_shape, cmp_shape),
        grid=(b_, GROUPS),
        in_specs=[kv(0), kv(1), kv(2), kv(3), kv(4), kv(5),
                  full((s_len, LANES)), full((s_len, LANES)),
                  full((CMP_LEN, HEAD_DIM)), full((CMP_LEN * HEAD_DIM, CMP_HIDDEN)),
                  full((CMP_HIDDEN, HEAD_DIM)),
                  full((CMP_LEN, HEAD_DIM)), full((CMP_LEN * HEAD_DIM, CMP_HIDDEN)),
                  full((CMP_HIDDEN, HEAD_DIM))],
        out_specs=(aug_out, seq_out, seq_out, seq_out, cmp_out, cmp_out),
        scratch_shapes=[pltpu.VMEM((s_len, HEAD_DIM), F32)],
        compiler_params=_cparams(("parallel", "parallel")),
        name="nsa_prep",
    )(h_main, h_main, h_main, h_main, h_main, h_main, cosf, sinf,
      pe_k, w1_k, w2_k, pe_v, w1_v, w2_v)


NSA_TQ = 128
NSA_TK = 256


def _nsa_attn_kernel(q_ref, hs_ref, cos_ref, sin_ref, kc_ref, vc_ref, ks_ref, vs_ref,
                     kw_ref, vw_ref, ovt_ref, o_ref, sc_ref):
    g = pl.program_id(1)
    qi = pl.program_id(2)
    tq, tk = NSA_TQ, NSA_TK
    rows = HPG * tq
    ns = ks_ref.shape[0] // SEL_BLOCK
    cosf = cos_ref[...]
    sinf = sin_ref[...]
    scale = HEAD_DIM ** -0.5
    qs = jnp.concatenate(
        [_rope(q_ref[:, hh * HEAD_DIM:(hh + 1) * HEAD_DIM], cosf, sinf) * scale
         for hh in range(HPG)], axis=0).astype(BF16)

    t_abs = qi * tq + lax.broadcasted_iota(jnp.int32, (tq, LANES), 0)
    lane = lax.broadcasted_iota(jnp.int32, (tq, LANES), 1)

    t_abs_k = qi * tq + lax.broadcasted_iota(jnp.int32, (tq, tk), 0)
    lane_k = lax.broadcasted_iota(jnp.int32, (tq, tk), 1)

    def add_bias(s_blk, bias):
        return (s_blk.reshape(HPG, tq, tk) + bias[None]).reshape(rows, tk)

    def fold(x):
        return x[:, :LANES], x[:, LANES:]

    mask_c = jnp.concatenate([lane * CMP_STRIDE + (CMP_LEN - 1) <= t_abs] * HPG, axis=0)
    s_c = _dot_nt(qs, kc_ref[...])
    m_c = jnp.max(jnp.where(mask_c, s_c, NEG_INF), axis=-1, keepdims=True)
    e_c = jnp.where(mask_c, jnp.exp(s_c - m_c), 0.0)
    l_c = jnp.sum(e_c, axis=-1, keepdims=True)
    p_c = jnp.where(l_c > 0.0, e_c / l_c, 0.0)
    o_c = _dot(p_c.astype(BF16), vc_ref[...])

    psum = p_c[0:tq]
    for hh in range(1, HPG):
        psum = psum + p_c[hh * tq:(hh + 1) * tq]
    p_hi, p_lo = _split2(psum)
    ovt = ovt_ref[...]
    imp =(_dot_nt(ovt, p_hi) + _dot_nt(ovt, p_lo))[:ns]
    blk = lax.broadcasted_iota(jnp.int32, (ns, tq), 0)
    cur = (qi * tq + lax.broadcasted_iota(jnp.int32, (ns, tq), 1)) // SEL_BLOCK
    forced = (blk == 0) | (blk == cur) | (blk == cur - 1)
    imp = jnp.where(forced, jnp.inf, jnp.where(blk <= cur, imp, -jnp.inf))
    rank = jnp.zeros((ns, tq), F32)
    for i in range(ns):
        ci = imp[i:i + 1, :]
        before = (ci > imp) | ((ci == imp) & (blk > i))
        rank = rank + jnp.where(before, 1.0, 0.0)
    unsel_t = jnp.where(rank < float(min(SEL_TOPK, ns)), 0.0, 1.0)
    unsel = jnp.concatenate([unsel_t, jnp.zeros((LANES - ns, tq), F32)], axis=0).T
    q_aug = jnp.concatenate([qs, jnp.concatenate([unsel.astype(BF16)] * HPG, axis=0)], axis=1)

    neg_rows = jnp.full((rows, LANES), NEG_INF, F32)
    zero_rows = jnp.zeros((rows, LANES), F32)
    zero_acc = jnp.zeros((rows, HEAD_DIM), F32)

    def row_max(mrun):
        return jnp.broadcast_to(jnp.max(mrun, axis=-1, keepdims=True), (rows, LANES))

    def probs(s_blk, m_b):
        s0, s1 = fold(s_blk)
        p0, p1 = jnp.exp(s0 - m_b), jnp.exp(s1 - m_b)
        return p0 + p1, jnp.concatenate([p0, p1], axis=1).astype(BF16)

    def sel_scores(kb, mrun):
        start = pl.multiple_of(kb * tk, tk)
        s_blk = _dot_nt(q_aug, ks_ref[pl.ds(start, tk), :])
        sc_ref[kb] = s_blk
        s0, s1 = fold(s_blk)
        return jnp.maximum(mrun, jnp.maximum(s0, s1))

    kb_last = (qi * tq) // tk
    mrun = lax.fori_loop(0, kb_last, sel_scores, neg_rows)
    start_last = pl.multiple_of(kb_last * tk, tk)
    bias = jnp.where(kb_last * tk + lane_k <= t_abs_k, 0.0, NEG_INF)
    s_blk = add_bias(_dot_nt(q_aug, ks_ref[pl.ds(start_last, tk), :]), bias)
    sc_ref[kb_last] = s_blk
    s0, s1 = fold(s_blk)
    m_b = row_max(jnp.maximum(mrun, jnp.maximum(s0, s1)))

    def sel_values(kb, carry):
        lrun, acc = carry
        start = pl.multiple_of(kb * tk, tk)
        psum_blk, p_blk = probs(sc_ref[kb], m_b)
        return lrun + psum_blk, acc + _dot(p_blk, vs_ref[pl.ds(start, tk), :])

    lrun, acc = lax.fori_loop(0, kb_last + 1, sel_values, (zero_rows, zero_acc))
    o_s = acc / jnp.sum(lrun, axis=-1, keepdims=True)

    nwin = WIN // tk + 1
    kbs = [kb_last - (nwin - 1) + d for d in range(nwin)]
    starts = [pl.multiple_of(jnp.maximum(kb, 0) * tk, tk) for kb in kbs]
    raw = [_dot_nt(qs, kw_ref[pl.ds(starts[d], tk), :]) for d in range(nwin)]
    mrun = neg_rows
    for d in range(nwin):
        diff = t_abs_k - (jnp.maximum(kbs[d], 0) * tk + lane_k)
        bias = jnp.where((diff >= 0) & (diff < WIN) & (kbs[d] >= 0), 0.0, NEG_INF)
        s_blk = add_bias(raw[d], bias)
        sc_ref[d] = s_blk
        s0, s1 = fold(s_blk)
        mrun = jnp.maximum(mrun, jnp.maximum(s0, s1))
    m_b = row_max(mrun)
    lrun, acc = zero_rows, zero_acc
    for d in range(nwin):
        psum_blk, p_blk = probs(sc_ref[d], m_b)
        lrun = lrun + psum_blk
        acc = acc + _dot(p_blk, vw_ref[pl.ds(starts[d], tk), :])
    o_w = acc / jnp.sum(lrun, axis=-1, keepdims=True)

    hs = hs_ref[...]
    for hh in range(HPG):
        gbase = SC_GATE + (g * HPG + hh) * 3
        r = slice(hh * tq, (hh + 1) * tq)
        out = (_sigmoid(_lane_col(hs, gbase)) * o_c[r]
               + _sigmoid(_lane_col(hs, gbase + 1)) * o_s[r]
               + _sigmoid(_lane_col(hs, gbase + 2)) * o_w[r])
        o_ref[:, hh * HEAD_DIM:(hh + 1) * HEAD_DIM] = out.astype(o_ref.dtype)


def _nsa_attn(h_main, h_small, cosf, sinf, kc, vc, ks, vs, kw, vw, overlap_t):
    b_, s_len, _ = h_main.shape
    tq = NSA_TQ
    nch = kc.shape[2]
    assert nch == LANES, "the compressed-block axis is laid out on one vreg of lanes"
    assert s_len % NSA_TK == 0 and NSA_TK == 2 * LANES
    qw = HPG * HEAD_DIM
    seq = pl.BlockSpec((None, None, s_len, HEAD_DIM), lambda b, g, i: (b, g, 0, 0))
    aug = pl.BlockSpec((None, None, s_len, 2 * HEAD_DIM), lambda b, g, i: (b, g, 0, 0))
    cmp_ = pl.BlockSpec((None, None, nch, HEAD_DIM), lambda b, g, i: (b, g, 0, 0))
    return pl.pallas_call(
        _nsa_attn_kernel,
        out_shape=jax.ShapeDtypeStruct((b_, s_len, HEADS * HEAD_DIM), BF16),
        grid=(b_, GROUPS, s_len // tq),
        in_specs=[pl.BlockSpec((None, tq, qw), lambda b, g, i: (b, i, CB_NSA_Q * LANES // qw + g)),
                  pl.BlockSpec((None, tq, LANES), lambda b, g, i: (b, i, 0)),
                  pl.BlockSpec((tq, LANES), lambda b, g, i: (i, 0)),
                  pl.BlockSpec((tq, LANES), lambda b, g, i: (i, 0)),
                  cmp_, cmp_, aug, seq, seq, seq,
                  pl.BlockSpec(overlap_t.shape, lambda b, g, i: (0, 0))],
        out_specs=pl.BlockSpec((None, tq, qw), lambda b, g, i: (b, i, g)),
        scratch_shapes=[pltpu.VMEM((s_len // NSA_TK, HPG * tq, NSA_TK), F32)],
        compiler_params=_cparams(("parallel", "parallel", "arbitrary")),
        name="nsa_attn",
    )(h_main, h_small, cosf, sinf, kc, vc, ks, vs, kw, vw, overlap_t)


def _nsa_constants(s_len):
    half = HEAD_DIM // 2
    inv_freq = ROPE_THETA ** (-jnp.arange(half, dtype=F32) / half)
    ang = jnp.arange(s_len, dtype=F32)[:, None] * inv_freq[None, :]
    cos, sin = jnp.cos(ang), jnp.sin(ang)
    cosf = jnp.concatenate([cos, cos], axis=-1)
    sinf = jnp.concatenate([-sin, sin], axis=-1)
    nch = s_len // CMP_STRIDE
    ns = s_len // SEL_BLOCK
    n = jnp.arange(nch)[:, None] * CMP_STRIDE
    j = jnp.arange(LANES)[None, :] * SEL_BLOCK
    overlap = ((n <= j + SEL_BLOCK - 1) & (n + CMP_LEN - 1 >= j)
               & (jnp.arange(nch)[:, None] < nch - CMP_LEN // CMP_STRIDE + 1)
               & (jnp.arange(LANES)[None, :] < ns)).astype(BF16)
    return cosf, sinf, overlap.T


def _layer(x, xb, p_i, w_in, conv_w, a_log, dt_bias, norm_w, pe_k, w1_k, w2_k, pe_v, w1_v, w2_v,
           w_a, w_b, w_out, ln1_g, ln1_b, w_gate, w_up, w_down, w_ple, w_ple_gate, ln2_g, ln2_b,
           consts):
    b_, s_len, d = x.shape
    t = b_ * s_len
    cosf, sinf, overlap_t = consts
    x2 = x.reshape(t, d)
    xb2 = xb.reshape(t, d)

    w_main = jnp.concatenate([w_in[:, :4096], w_in[:, 4112:6672], w_in[:, 6696:]], axis=1).astype(BF16)
    w_small = jnp.concatenate([w_in[:, 4096:4112], w_in[:, 6672:6696],
                               jnp.zeros((d, LANES - 40), w_in.dtype)], axis=1).astype(BF16)
    tm = min(1024, t)
    h_main = _matmul(xb2, w_main, F32, tm, 512, "proj_main")
    h_small = _matmul(xb2, w_small, F32, tm, LANES, "proj_small")
    h_main3 = h_main.reshape(b_, s_len, N_MAIN)
    h_small3 = h_small.reshape(b_, s_len, LANES)

    o_a = _deltanet(h_main3, h_small3, conv_w, a_log, dt_bias, norm_w)
    ks, vs, kw, vw, kc, vc = _nsa_prep(h_main3, cosf, sinf, pe_k, w1_k.astype(BF16),
                                       w2_k.astype(BF16), pe_v, w1_v.astype(BF16),
                                       w2_v.astype(BF16))
    o_b = _nsa_attn(h_main3, h_small3, cosf, sinf, kc, vc, ks, vs, kw, vw, overlap_t)

    mixed = _merge(o_a.reshape(t, -1), o_b.reshape(t, -1), w_a.astype(BF16), w_b.astype(BF16),
                   h_main, tm, 512)
    x1, x1b = _outproj_ln(mixed, w_out.astype(BF16), x2, ln1_g, ln1_b, min(256, t))
    act = _ffn_act(x1b, w_gate.astype(BF16), w_up.astype(BF16), tm, 512)
    resid = _resid(x1, x1b, p_i.reshape(t, PLE_DIM).astype(BF16), w_ple.astype(BF16),
                   w_ple_gate.astype(BF16), tm, 512)
    y, yb = _ffn_out(act, w_down.astype(BF16), resid, ln2_g, ln2_b, min(512, t), 512)
    return y.reshape(b_, s_len, d), yb.reshape(b_, s_len, d)


def kernel(x, p, w_in, dn_conv_w, dn_a_log, dn_dt_bias, dn_norm_w, cmp_pe_k, cmp_w1_k, cmp_w2_k, cmp_pe_v, cmp_w1_v, cmp_w2_v, w_branch_a, w_branch_b, w_out, ln1_g, ln1_b, w_ffn_gate, w_ffn_up, w_ffn_down, w_ple, w_ple_gate, ln2_g, ln2_b):
    consts = _nsa_constants(x.shape[1])
    xb = x.astype(BF16)
    for i in range(DEPTH):
        x, xb = _layer(x, xb, p[i], w_in[i], dn_conv_w[i], dn_a_log[i], dn_dt_bias[i], dn_norm_w[i],
                       cmp_pe_k[i], cmp_w1_k[i], cmp_w2_k[i], cmp_pe_v[i], cmp_w1_v[i], cmp_w2_v[i],
                       w_branch_a[i], w_branch_b[i], w_out[i], ln1_g[i], ln1_b[i],
                       w_ffn_gate[i], w_ffn_up[i], w_ffn_down[i], w_ple[i], w_ple_gate[i],
                       ln2_g[i], ln2_b[i], consts)
    return x
```

```python
import jax
import jax.numpy as jnp
from jax import lax
from jax.experimental import pallas as pl
from jax.experimental.pallas import tpu as pltpu

D_MODEL = 2048
DEPTH = 2
HEAD_DIM = 128
HEADS = 8
DN_CONV = 4
DN_CHUNK = 64
GROUPS = 2
HPG = HEADS // GROUPS
CMP_LEN = 32
CMP_STRIDE = 16
CMP_HIDDEN = 256
SEL_BLOCK = 64
SEL_TOPK = 16
WIN = 512
ROPE_THETA = 10000.0
D_FF = 5632
PLE_DIM = 256
ALPHA = (2.0 * DEPTH) ** 0.25
LN_EPS = 1e-5
NORM_EPS = 1e-6
NEG_INF = -1e30

LANES = 128
SUBLANES = 8
VMEM_LIMIT = 56 * 1024 * 1024

CB_DN_Q, CB_DN_K, CB_DN_V, CB_DN_Z = 0, 8, 16, 24
CB_NSA_Q = 32
CB_KV = 40
N_MAIN = 10752
COL_MERGE_A = 6656
COL_MERGE_B = 8704
SC_BETA, SC_DECAY, SC_GATE = 0, 8, 16

F32 = jnp.float32
BF16 = jnp.bfloat16


def _cparams(sem):
    return pltpu.CompilerParams(dimension_semantics=sem, vmem_limit_bytes=VMEM_LIMIT)


def _dot(a, b):
    return jnp.dot(a, b, preferred_element_type=F32)


def _dot_nt(a, b):
    return lax.dot_general(a, b, (((1,), (1,)), ((), ())), preferred_element_type=F32)


def _dot_tn(a, b):
    return lax.dot_general(a, b, (((0,), (0,)), ((), ())), preferred_element_type=F32)


def _sigmoid(x):
    return 1.0 / (1.0 + jnp.exp(-x))


def _silu(x):
    return x * _sigmoid(x)


def _layer_norm(y, g, b):
    mu = jnp.mean(y, axis=-1, keepdims=True)
    d = y - mu
    var = jnp.mean(d * d, axis=-1, keepdims=True)
    return d * lax.rsqrt(var + LN_EPS) * g + b


def _lane_col(x, idx):
    lane = lax.broadcasted_iota(jnp.int32, x.shape, 1)
    return jnp.sum(jnp.where(lane == idx, x, 0.0), axis=1, keepdims=True)


def _split2(x):
    hi = x.astype(BF16)
    return hi, (x - hi.astype(F32)).astype(BF16)


def _split3(x):
    x1 = x.astype(BF16)
    r = x - x1.astype(F32)
    x2 = r.astype(BF16)
    return x1, x2, (r - x2.astype(F32)).astype(BF16)


def _dot_01(ones_b, x):
    x1, x2, x3 = _split3(x)
    return _dot(ones_b, x1) + _dot(ones_b, x2) + _dot(ones_b, x3)


def _mm_kernel(a_ref, w_ref, o_ref):
    o_ref[...] = _dot(a_ref[...], w_ref[...]).astype(o_ref.dtype)


def _matmul(a, w, out_dtype, tm, tn, name):
    m, k = a.shape
    n = w.shape[1]
    return pl.pallas_call(
        _mm_kernel,
        out_shape=jax.ShapeDtypeStruct((m, n), out_dtype),
        grid=(m // tm, n // tn),
        in_specs=[pl.BlockSpec((tm, k), lambda i, j: (i, 0)),
                  pl.BlockSpec((k, tn), lambda i, j: (0, j))],
        out_specs=pl.BlockSpec((tm, tn), lambda i, j: (i, j)),
        compiler_params=_cparams(("parallel", "arbitrary")),
        name=name,
    )(a, w)


def _merge_kernel(oa_ref, ob_ref, wa_ref, wb_ref, ma_ref, mb_ref, o_ref):
    ya = _dot(oa_ref[...], wa_ref[...])
    yb = _dot(ob_ref[...], wb_ref[...])
    o_ref[...] = (_sigmoid(ma_ref[...]) * ya + _sigmoid(mb_ref[...]) * yb).astype(o_ref.dtype)


def _merge(o_a, o_b, w_a, w_b, h_main, tm, tn):
    m, k = o_a.shape
    n = w_a.shape[1]
    ca, cb = COL_MERGE_A // tn, COL_MERGE_B // tn
    return pl.pallas_call(
        _merge_kernel,
        out_shape=jax.ShapeDtypeStruct((m, n), BF16),
        grid=(m // tm, n // tn),
        in_specs=[pl.BlockSpec((tm, k), lambda i, j: (i, 0)),
                  pl.BlockSpec((tm, k), lambda i, j: (i, 0)),
                  pl.BlockSpec((k, tn), lambda i, j: (0, j)),
                  pl.BlockSpec((k, tn), lambda i, j: (0, j)),
                  pl.BlockSpec((tm, tn), lambda i, j: (i, ca + j)),
                  pl.BlockSpec((tm, tn), lambda i, j: (i, cb + j))],
        out_specs=pl.BlockSpec((tm, tn), lambda i, j: (i, j)),
        compiler_params=_cparams(("parallel", "arbitrary")),
        name="merge",
    )(o_a, o_b, w_a, w_b, h_main, h_main)


def _outproj_ln_kernel(mx_ref, w_ref, x_ref, g_ref, b_ref, o_ref, ob_ref):
    y = ALPHA * x_ref[...] + _dot(mx_ref[...], w_ref[...])
    out = _layer_norm(y, g_ref[...], b_ref[...])
    o_ref[...] = out
    ob_ref[...] = out.astype(BF16)


def _outproj_ln(mixed, w_out, x, g, b, tm):
    m, d = x.shape
    return pl.pallas_call(
        _outproj_ln_kernel,
        out_shape=(jax.ShapeDtypeStruct((m, d), F32), jax.ShapeDtypeStruct((m, d), BF16)),
        grid=(m // tm,),
        in_specs=[pl.BlockSpec((tm, d), lambda i: (i, 0)),
                  pl.BlockSpec((d, d), lambda i: (0, 0)),
                  pl.BlockSpec((tm, d), lambda i: (i, 0)),
                  pl.BlockSpec((1, d), lambda i: (0, 0)),
                  pl.BlockSpec((1, d), lambda i: (0, 0))],
        out_specs=(pl.BlockSpec((tm, d), lambda i: (i, 0)),
                   pl.BlockSpec((tm, d), lambda i: (i, 0))),
        compiler_params=_cparams(("parallel",)),
        name="outproj_ln",
    )(mixed, w_out, x, g.reshape(1, d), b.reshape(1, d))


def _ffn_act_kernel(x_ref, wg_ref, wu_ref, o_ref):
    xv = x_ref[...]
    o_ref[...] = (_silu(_dot(xv, wg_ref[...])) * _dot(xv, wu_ref[...])).astype(o_ref.dtype)


def _ffn_act(xb, w_gate, w_up, tm, tn):
    m, k = xb.shape
    n = w_gate.shape[1]
    return pl.pallas_call(
        _ffn_act_kernel,
        out_shape=jax.ShapeDtypeStruct((m, n), BF16),
        grid=(m // tm, n // tn),
        in_specs=[pl.BlockSpec((tm, k), lambda i, j: (i, 0)),
                  pl.BlockSpec((k, tn), lambda i, j: (0, j)),
                  pl.BlockSpec((k, tn), lambda i, j: (0, j))],
        out_specs=pl.BlockSpec((tm, tn), lambda i, j: (i, j)),
        compiler_params=_cparams(("parallel", "arbitrary")),
        name="ffn_act",
    )(xb, w_gate, w_up)


def _resid_kernel(x_ref, xb_ref, p_ref, wp_ref, wpg_ref, o_ref):
    ple = _dot(p_ref[...], wp_ref[...]) * _sigmoid(_dot(xb_ref[...], wpg_ref[...]))
    o_ref[...] = ALPHA * x_ref[...] + ple


def _resid(x1, x1b, pb, w_ple, w_ple_gate, tm, tn):
    m, d = x1.shape
    kp = pb.shape[1]
    return pl.pallas_call(
        _resid_kernel,
        out_shape=jax.ShapeDtypeStruct((m, d), F32),
        grid=(m // tm, d // tn),
        in_specs=[pl.BlockSpec((tm, tn), lambda i, j: (i, j)),
                  pl.BlockSpec((tm, d), lambda i, j: (i, 0)),
                  pl.BlockSpec((tm, kp), lambda i, j: (i, 0)),
                  pl.BlockSpec((kp, tn), lambda i, j: (0, j)),
                  pl.BlockSpec((d, tn), lambda i, j: (0, j))],
        out_specs=pl.BlockSpec((tm, tn), lambda i, j: (i, j)),
        compiler_params=_cparams(("parallel", "arbitrary")),
        name="ple_resid",
    )(x1, x1b, pb, w_ple, w_ple_gate)


def _ffn_out_kernel(act_ref, w_ref, r_ref, g_ref, b_ref, o_ref, ob_ref):
    out = _layer_norm(r_ref[...] + _dot(act_ref[...], w_ref[...]), g_ref[...], b_ref[...])
    o_ref[...] = out
    ob_ref[...] = out.astype(BF16)


def _ffn_out(act, w_down, resid, g, b, tm):
    m, kf = act.shape
    d = w_down.shape[1]
    return pl.pallas_call(
        _ffn_out_kernel,
        out_shape=(jax.ShapeDtypeStruct((m, d), F32), jax.ShapeDtypeStruct((m, d), BF16)),
        grid=(m // tm,),
        in_specs=[pl.BlockSpec((tm, kf), lambda i: (i, 0)),
                  pl.BlockSpec((kf, d), lambda i: (0, 0), pipeline_mode=pl.Buffered(1)),
                  pl.BlockSpec((tm, d), lambda i: (i, 0)),
                  pl.BlockSpec((1, d), lambda i: (0, 0)),
                  pl.BlockSpec((1, d), lambda i: (0, 0))],
        out_specs=(pl.BlockSpec((tm, d), lambda i: (i, 0)),
                   pl.BlockSpec((tm, d), lambda i: (i, 0))),
        compiler_params=_cparams(("parallel",)),
        name="ffn_out_ln",
    )(act, w_down, resid, g.reshape(1, d), b.reshape(1, d))


DN_TS = 256
DN_HB = 4


def _dn_kernel(alog_ref, dtb_ref,
               q_ref, k_ref, v_ref, z_ref, qp_ref, kp_ref, vp_ref, hs_ref,
               cwq_ref, cwk_ref, cwv_ref, nw_ref, o_ref, state_ref):
    hg = pl.program_id(1)
    s = pl.program_id(2)
    c = DN_CHUNK
    d = HEAD_DIM

    @pl.when(s == 0)
    def _():
        state_ref[...] = jnp.zeros_like(state_ref)

    def conv_silu(x_ref, xp_ref, cw_ref, hb):
        cols = slice(hb * d, (hb + 1) * d)
        prev = jnp.where(s == 0, 0.0, xp_ref[:, cols])
        xx = jnp.concatenate([prev, x_ref[:, cols]], axis=0)
        cw = cw_ref[:, cols]
        y = xx[SUBLANES:] * cw[DN_CONV - 1:DN_CONV]
        for i in range(DN_CONV - 1):
            y = y + pltpu.roll(xx, DN_CONV - 1 - i, axis=0)[SUBLANES:] * cw[i:i + 1]
        return _silu(y)

    def l2norm(x):
        return x * lax.rsqrt(jnp.sum(x * x, axis=-1, keepdims=True) + NORM_EPS)

    row = lax.broadcasted_iota(jnp.int32, (c, c), 0)
    col = lax.broadcasted_iota(jnp.int32, (c, c), 1)
    causal = row >= col
    strict = row > col
    tri_b = jnp.where(causal, 1.0, 0.0).astype(BF16)
    eye = jnp.where(row == col, 1.0, 0.0).astype(F32)
    ones8_b = jnp.ones((SUBLANES, c), BF16)
    hs = hs_ref[...]
    nw = nw_ref[...]

    nck = DN_TS // c
    pairs = [(hb, ci) for hb in range(DN_HB) for ci in range(nck)]
    qs, ks, vs, betas, gbs = [], [], [], [], []
    for hb in range(DN_HB):
        h = hg * DN_HB + hb
        q_all = l2norm(conv_silu(q_ref, qp_ref, cwq_ref, hb)) * (d ** -0.5)
        k_all = l2norm(conv_silu(k_ref, kp_ref, cwk_ref, hb))
        v_all = conv_silu(v_ref, vp_ref, cwv_ref, hb)
        beta_all = _sigmoid(_lane_col(hs, SC_BETA + h))
        a_all = _lane_col(hs, SC_DECAY + h) + dtb_ref[h]
        softplus = jnp.maximum(a_all, 0.0) + jnp.log(1.0 + jnp.exp(-jnp.abs(a_all)))
        g_all = -jnp.exp(jnp.zeros_like(a_all) + alog_ref[h]) * softplus
        for ci in range(nck):
            sl = slice(ci * c, (ci + 1) * c)
            qs.append(q_all[sl])
            ks.append(k_all[sl])
            vs.append(v_all[sl])
            betas.append(beta_all[sl])
            gbs.append(jnp.broadcast_to(g_all[sl], (c, LANES)))
    n = len(pairs)
    gcs = [_dot_01(tri_b, gbs[i]) for i in range(n)]
    gc_rows = [_dot_01(ones8_b, jnp.where(row <= col, gbs[i][:, :c], 0.0))[0:1] for i in range(n)]
    decays = [jnp.exp(jnp.where(causal, gcs[i][:, :c] - gc_rows[i], NEG_INF)) for i in range(n)]
    kbs = [ks[i] * betas[i] for i in range(n)]
    kbfs = [ks[i].astype(BF16) for i in range(n)]
    negs = [jnp.where(strict, -(_dot_nt(kbs[i].astype(BF16), kbfs[i]) * decays[i]), 0.0)
            for i in range(n)]
    accs = [eye + negs[i] for i in range(n)]
    nbf = [negs[i].astype(BF16) for i in range(n)]
    pws = [_dot(nbf[i], nbf[i]) for i in range(n)]
    for _ in range(4):
        ress = [_dot(pws[i].astype(BF16), jnp.concatenate([pws[i], accs[i]], axis=1).astype(BF16))
                for i in range(n)]
        pws = [ress[i][:, :c] for i in range(n)]
        accs = [accs[i] + ress[i][:, c:] for i in range(n)]
    tinvs = [(accs[i] + _dot(pws[i].astype(BF16), accs[i].astype(BF16))).astype(BF16)
             for i in range(n)]
    egs = [jnp.exp(gcs[i]) for i in range(n)]
    uws = [_dot(tinvs[i], jnp.concatenate([vs[i] * betas[i], kbs[i] * egs[i]], axis=1).astype(BF16))
           .astype(BF16) for i in range(n)]
    qks = [jnp.where(causal, _dot_nt(qs[i].astype(BF16), kbfs[i]) * decays[i], 0.0).astype(BF16)
           for i in range(n)]
    k_decs = [(ks[i] * jnp.exp(gcs[i][c - 1:c] - gcs[i])).astype(BF16) for i in range(n)]
    nbs = [_dot_tn(k_decs[i], uws[i]) for i in range(n)]
    prs = [_dot(qks[i], uws[i]) for i in range(n)]
    lhss = [jnp.concatenate([(qs[i] * egs[i] - prs[i][:, d:]).astype(BF16),
                             nbs[i][:, d:].astype(BF16)], axis=0) for i in range(n)]

    states = [state_ref[hb] for hb in range(DN_HB)]
    for ci in range(nck):
        sl = slice(ci * c, (ci + 1) * c)
        ress = [_dot(lhss[hb * nck + ci], states[hb].astype(BF16)) for hb in range(DN_HB)]
        for hb in range(DN_HB):
            i = hb * nck + ci
            cols = slice(hb * d, (hb + 1) * d)
            o = ress[hb][:c] + prs[i][:, :d]
            states[hb] = states[hb] * egs[i][c - 1:c] - ress[hb][c:] + nbs[i][:, :d]
            o = o * lax.rsqrt(jnp.mean(o * o, axis=-1, keepdims=True) + NORM_EPS) * nw
            o_ref[sl, cols] = (o * _silu(z_ref[sl, cols])).astype(o_ref.dtype)
    for hb in range(DN_HB):
        state_ref[hb] = states[hb]


def _deltanet(h_main, h_small, conv_w, a_log, dt_bias, norm_w):
    b_, s_len, _ = h_main.shape
    ts = DN_TS
    wd = DN_HB * HEAD_DIM
    ngrp = HEADS // DN_HB
    blk = lambda cb: pl.BlockSpec((None, ts, wd), lambda b, h, s: (b, s, cb // DN_HB + h))
    prev = lambda cb: pl.BlockSpec(
        (None, SUBLANES, wd),
        lambda b, h, s: (b, jnp.maximum(s * (ts // SUBLANES) - 1, 0), cb // DN_HB + h))
    cw = lambda cb: pl.BlockSpec((DN_CONV, wd), lambda b, h, s: (0, cb // DN_HB + h))
    smem = pl.BlockSpec(memory_space=pltpu.SMEM)
    return pl.pallas_call(
        _dn_kernel,
        out_shape=jax.ShapeDtypeStruct((b_, s_len, HEADS * HEAD_DIM), BF16),
        grid=(b_, ngrp, s_len // ts),
        in_specs=[smem, smem,
                  blk(CB_DN_Q), blk(CB_DN_K), blk(CB_DN_V), blk(CB_DN_Z),
                  prev(CB_DN_Q), prev(CB_DN_K), prev(CB_DN_V),
                  pl.BlockSpec((None, ts, LANES), lambda b, h, s: (b, s, 0)),
                  cw(CB_DN_Q), cw(CB_DN_K), cw(CB_DN_V),
                  pl.BlockSpec((1, LANES), lambda b, h, s: (0, 0))],
        out_specs=pl.BlockSpec((None, ts, wd), lambda b, h, s: (b, s, h)),
        scratch_shapes=[pltpu.VMEM((DN_HB, HEAD_DIM, HEAD_DIM), F32)],
        compiler_params=_cparams(("parallel", "parallel", "arbitrary")),
        name="deltanet",
    )(a_log, dt_bias, h_main, h_main, h_main, h_main, h_main, h_main, h_main, h_small,
      conv_w, conv_w, conv_w, norm_w.reshape(1, LANES))


def _rope(x, cosf, sinf):
    return x * cosf + pltpu.roll(x, HEAD_DIM // 2, axis=1) * sinf


def _nsa_prep_kernel(kc_ref, vc_ref, ks_ref, vs_ref, kw_ref, vw_ref, cos_ref, sin_ref,
                     pek_ref, w1k_ref, w2k_ref, pev_ref, w1v_ref, w2v_ref,
                     kso_ref, vso_ref, kwo_ref, vwo_ref, kco_ref, vco_ref, buf_ref):
    cosf = cos_ref[...]
    sinf = sin_ref[...]
    s_len = ks_ref.shape[0]
    pos = lax.broadcasted_iota(jnp.int32, (s_len, LANES), 0)
    lane = lax.broadcasted_iota(jnp.int32, (s_len, LANES), 1)
    kso_ref[:, :HEAD_DIM] = _rope(ks_ref[...], cosf, sinf).astype(BF16)
    kso_ref[:, HEAD_DIM:] = jnp.where(pos // SEL_BLOCK == lane, NEG_INF, 0.0).astype(BF16)
    kwo_ref[...] = _rope(kw_ref[...], cosf, sinf).astype(BF16)
    vso_ref[...] = vs_ref[...].astype(BF16)
    vwo_ref[...] = vw_ref[...].astype(BF16)
    nch = buf_ref.shape[0] // CMP_STRIDE

    def compress(pe_ref, w1_ref, w2_ref, out_ref):
        a0 = jnp.zeros((nch, CMP_HIDDEN), F32)
        a1 = jnp.zeros((nch, CMP_HIDDEN), F32)
        for i in range(CMP_STRIDE):
            xi = buf_ref[pl.ds(i, nch, stride=CMP_STRIDE), :]
            lo = (xi + pe_ref[i:i + 1, :]).astype(BF16)
            hi = (xi + pe_ref[CMP_STRIDE + i:CMP_STRIDE + i + 1, :]).astype(BF16)
            a0 = a0 + _dot(lo, w1_ref[i * HEAD_DIM:(i + 1) * HEAD_DIM, :])
            a1 = a1 + _dot(hi, w1_ref[(CMP_STRIDE + i) * HEAD_DIM:(CMP_STRIDE + i + 1) * HEAD_DIM, :])
        hid = a0 + pltpu.roll(a1, nch - 1, axis=0)
        out_ref[...] = _dot(_silu(hid).astype(BF16), w2_ref[...]).astype(out_ref.dtype)

    buf_ref[...] = _rope(kc_ref[...], cosf, sinf)
    compress(pek_ref, w1k_ref, w2k_ref, kco_ref)
    buf_ref[...] = vc_ref[...]
    compress(pev_ref, w1v_ref, w2v_ref, vco_ref)


def _nsa_prep(h_main, cosf, sinf, pe_k, w1_k, w2_k, pe_v, w1_v, w2_v):
    b_, s_len, _ = h_main.shape
    nch = s_len // CMP_STRIDE
    kv = lambda i: pl.BlockSpec((None, s_len, LANES), lambda b, g: (b, 0, CB_KV + 2 * i + g))
    full = lambda shape: pl.BlockSpec(shape, lambda b, g: tuple(0 for _ in shape))
    assert s_len // SEL_BLOCK <= LANES
    seq_out = pl.BlockSpec((None, None, s_len, LANES), lambda b, g: (b, g, 0, 0))
    aug_out = pl.BlockSpec((None, None, s_len, 2 * LANES), lambda b, g: (b, g, 0, 0))
    cmp_out = pl.BlockSpec((None, None, nch, LANES), lambda b, g: (b, g, 0, 0))
    seq_shape = jax.ShapeDtypeStruct((b_, GROUPS, s_len, HEAD_DIM), BF16)
    aug_shape = jax.ShapeDtypeStruct((b_, GROUPS, s_len, 2 * HEAD_DIM), BF16)
    cmp_shape = jax.ShapeDtypeStruct((b_, GROUPS, nch, HEAD_DIM), BF16)
    return pl.pallas_call(
        _nsa_prep_kernel,
        out_shape=(aug_shape, seq_shape, seq_shape, seq_shape, cmp_shape, cmp_shape),
        grid=(b_, GROUPS),
        in_specs=[kv(0), kv(1), kv(2), kv(3), kv(4), kv(5),
                  full((s_len, LANES)), full((s_len, LANES)),
                  full((CMP_LEN, HEAD_DIM)), full((CMP_LEN * HEAD_DIM, CMP_HIDDEN)),
                  full((CMP_HIDDEN, HEAD_DIM)),
                  full((CMP_LEN, HEAD_DIM)), full((CMP_LEN * HEAD_DIM, CMP_HIDDEN)),
                  full((CMP_HIDDEN, HEAD_DIM))],
        out_specs=(aug_out, seq_out, seq_out, seq_out, cmp_out, cmp_out),
        scratch_shapes=[pltpu.VMEM((s_len, HEAD_DIM), F32)],
        compiler_params=_cparams(("parallel", "parallel")),
        name="nsa_prep",
    )(h_main, h_main, h_main, h_main, h_main, h_main, cosf, sinf,
      pe_k, w1_k, w2_k, pe_v, w1_v, w2_v)


NSA_TQ = 128
NSA_TK = 256


def _nsa_attn_kernel(q_ref, hs_ref, cos_ref, sin_ref, kc_ref, vc_ref, ks_ref, vs_ref,
                     kw_ref, vw_ref, ovt_ref, o_ref, os_ref):
    g = pl.program_id(1)
    qi = pl.program_id(2)
    tq, tk = NSA_TQ, NSA_TK
    rows = HPG * tq
    ns = ks_ref.shape[0] // SEL_BLOCK
    cosf = cos_ref[...]
    sinf = sin_ref[...]
    scale = HEAD_DIM ** -0.5
    qs = jnp.concatenate(
        [_rope(q_ref[:, hh * HEAD_DIM:(hh + 1) * HEAD_DIM], cosf, sinf) * scale
         for hh in range(HPG)], axis=0).astype(BF16)

    t_abs = qi * tq + lax.broadcasted_iota(jnp.int32, (tq, LANES), 0)
    lane = lax.broadcasted_iota(jnp.int32, (tq, LANES), 1)

    t_abs_k = qi * tq + lax.broadcasted_iota(jnp.int32, (tq, tk), 0)
    lane_k = lax.broadcasted_iota(jnp.int32, (tq, tk), 1)

    def add_bias(s_blk, bias):
        return (s_blk.reshape(HPG, tq, tk) + bias[None]).reshape(rows, tk)

    def fold(x):
        return x[:, :LANES], x[:, LANES:]

    neg_rows = jnp.full((rows, LANES), NEG_INF, F32)
    zero_rows = jnp.zeros((rows, LANES), F32)
    zero_acc = jnp.zeros((rows, HEAD_DIM), F32)

    def running_max(score_blocks):
        mrun = neg_rows
        for s_blk in score_blocks:
            s0, s1 = fold(s_blk)
            mrun = jnp.maximum(mrun, jnp.maximum(s0, s1))
        return jnp.broadcast_to(jnp.max(mrun, axis=-1, keepdims=True), (rows, LANES))

    def probs(s_blk, m_b):
        s0, s1 = fold(s_blk)
        p0, p1 = jnp.exp(s0 - m_b), jnp.exp(s1 - m_b)
        return p0 + p1, jnp.concatenate([p0, p1], axis=1).astype(BF16)

    def weighted_values(ps, value_blocks):
        lrun, acc = zero_rows, zero_acc
        for (psum_blk, p_blk), v_blk in zip(ps, value_blocks):
            lrun = lrun + psum_blk
            acc = acc + _dot(p_blk, v_blk)
        return acc / jnp.sum(lrun, axis=-1, keepdims=True)

    kb_last = (qi * tq) // tk
    nwin = WIN // tk + 1
    win_kbs = [kb_last - (nwin - 1) + d for d in range(nwin)]
    win_starts = [pl.multiple_of(jnp.maximum(kb, 0) * tk, tk) for kb in win_kbs]
    win_raw = [_dot_nt(qs, kw_ref[pl.ds(win_starts[d], tk), :]) for d in range(nwin)]

    mask_c = jnp.concatenate([lane * CMP_STRIDE + (CMP_LEN - 1) <= t_abs] * HPG, axis=0)
    s_c = _dot_nt(qs, kc_ref[...])
    m_c = jnp.max(jnp.where(mask_c, s_c, NEG_INF), axis=-1, keepdims=True)
    e_c = jnp.where(mask_c, jnp.exp(s_c - m_c), 0.0)
    l_c = jnp.sum(e_c, axis=-1, keepdims=True)
    p_c = jnp.where(l_c > 0.0, e_c / l_c, 0.0)

    win_s = []
    for d in range(nwin):
        diff = t_abs_k - (jnp.maximum(win_kbs[d], 0) * tk + lane_k)
        bias = jnp.where((diff >= 0) & (diff < WIN) & (win_kbs[d] >= 0), 0.0, NEG_INF)
        win_s.append(add_bias(win_raw[d], bias))
    win_m = running_max(win_s)

    o_c = _dot(p_c.astype(BF16), vc_ref[...])
    psum = p_c[0:tq]
    for hh in range(1, HPG):
        psum = psum + p_c[hh * tq:(hh + 1) * tq]
    p_hi, p_lo = _split2(psum)
    ovt = ovt_ref[...]
    imp = (_dot_nt(ovt, p_hi) + _dot_nt(ovt, p_lo))[:ns]
    blk = lax.broadcasted_iota(jnp.int32, (ns, tq), 0)
    cur = (qi * tq + lax.broadcasted_iota(jnp.int32, (ns, tq), 1)) // SEL_BLOCK
    forced = (blk == 0) | (blk == cur) | (blk == cur - 1)
    imp = jnp.where(forced, jnp.inf, jnp.where(blk <= cur, imp, -jnp.inf))

    win_p = [probs(s_blk, win_m) for s_blk in win_s]

    rank = jnp.zeros((ns, tq), F32)
    for i in range(ns):
        ci = imp[i:i + 1, :]
        before = (ci > imp) | ((ci == imp) & (blk > i))
        rank = rank + jnp.where(before, 1.0, 0.0)
    unsel_t = jnp.where(rank < float(min(SEL_TOPK, ns)), 0.0, 1.0)
    unsel = jnp.concatenate([unsel_t, jnp.zeros((LANES - ns, tq), F32)], axis=0).T
    q_aug = jnp.concatenate([qs, jnp.concatenate([unsel.astype(BF16)] * HPG, axis=0)], axis=1)

    o_w = weighted_values(win_p, [vw_ref[pl.ds(win_starts[d], tk), :] for d in range(nwin)])

    def sel_variant(n_full):
        def run():
            s_blks = [_dot_nt(q_aug, ks_ref[j * tk:(j + 1) * tk, :]) for j in range(n_full + 1)]
            bias = jnp.where(n_full * tk + lane_k <= t_abs_k, 0.0, NEG_INF)
            s_blks[n_full] = add_bias(s_blks[n_full], bias)
            m_b = running_max(s_blks)
            ps = [probs(s_blk, m_b) for s_blk in s_blks]
            os_ref[...] = weighted_values(
                ps, [vs_ref[j * tk:(j + 1) * tk, :] for j in range(n_full + 1)])
        return run

    for n_full in range(ks_ref.shape[0] // tk):
        pl.when(kb_last == n_full)(sel_variant(n_full))
    o_s = os_ref[...]

    hs = hs_ref[...]
    for hh in range(HPG):
        gbase = SC_GATE + (g * HPG + hh) * 3
        r = slice(hh * tq, (hh + 1) * tq)
        out = (_sigmoid(_lane_col(hs, gbase)) * o_c[r]
               + _sigmoid(_lane_col(hs, gbase + 1)) * o_s[r]
               + _sigmoid(_lane_col(hs, gbase + 2)) * o_w[r])
        o_ref[:, hh * HEAD_DIM:(hh + 1) * HEAD_DIM] = out.astype(o_ref.dtype)


def _nsa_attn(h_main, h_small, cosf, sinf, kc, vc, ks, vs, kw, vw, overlap_t):
    b_, s_len, _ = h_main.shape
    tq = NSA_TQ
    nch = kc.shape[2]
    assert nch == LANES, "the compressed-block axis is laid out on one vreg of lanes"
    assert s_len % NSA_TK == 0 and NSA_TK == 2 * LANES
    qw = HPG * HEAD_DIM
    seq = pl.BlockSpec((None, None, s_len, HEAD_DIM), lambda b, g, i: (b, g, 0, 0))
    aug = pl.BlockSpec((None, None, s_len, 2 * HEAD_DIM), lambda b, g, i: (b, g, 0, 0))
    cmp_ = pl.BlockSpec((None, None, nch, HEAD_DIM), lambda b, g, i: (b, g, 0, 0))
    return pl.pallas_call(
        _nsa_attn_kernel,
        out_shape=jax.ShapeDtypeStruct((b_, s_len, HEADS * HEAD_DIM), BF16),
        grid=(b_, GROUPS, s_len // tq),
        in_specs=[pl.BlockSpec((None, tq, qw), lambda b, g, i: (b, i, CB_NSA_Q * LANES // qw + g)),
                  pl.BlockSpec((None, tq, LANES), lambda b, g, i: (b, i, 0)),
                  pl.BlockSpec((tq, LANES), lambda b, g, i: (i, 0)),
                  pl.BlockSpec((tq, LANES), lambda b, g, i: (i, 0)),
                  cmp_, cmp_, aug, seq, seq, seq,
                  pl.BlockSpec(overlap_t.shape, lambda b, g, i: (0, 0))],
        out_specs=pl.BlockSpec((None, tq, qw), lambda b, g, i: (b, i, g)),
        scratch_shapes=[pltpu.VMEM((HPG * tq, HEAD_DIM), F32)],
        compiler_params=_cparams(("parallel", "parallel", "arbitrary")),
        name="nsa_attn",
    )(h_main, h_small, cosf, sinf, kc, vc, ks, vs, kw, vw, overlap_t)


def _nsa_constants(s_len):
    half = HEAD_DIM // 2
    inv_freq = ROPE_THETA ** (-jnp.arange(half, dtype=F32) / half)
    ang = jnp.arange(s_len, dtype=F32)[:, None] * inv_freq[None, :]
    cos, sin = jnp.cos(ang), jnp.sin(ang)
    cosf = jnp.concatenate([cos, cos], axis=-1)
    sinf = jnp.concatenate([-sin, sin], axis=-1)
    nch = s_len // CMP_STRIDE
    ns = s_len // SEL_BLOCK
    n = jnp.arange(nch)[:, None] * CMP_STRIDE
    j = jnp.arange(LANES)[None, :] * SEL_BLOCK
    overlap = ((n <= j + SEL_BLOCK - 1) & (n + CMP_LEN - 1 >= j)
               & (jnp.arange(nch)[:, None] < nch - CMP_LEN // CMP_STRIDE + 1)
               & (jnp.arange(LANES)[None, :] < ns)).astype(BF16)
    return cosf, sinf, overlap.T


def _layer(x, xb, p_i, w_in, conv_w, a_log, dt_bias, norm_w, pe_k, w1_k, w2_k, pe_v, w1_v, w2_v,
           w_a, w_b, w_out, ln1_g, ln1_b, w_gate, w_up, w_down, w_ple, w_ple_gate, ln2_g, ln2_b,
           consts):
    b_, s_len, d = x.shape
    t = b_ * s_len
    cosf, sinf, overlap_t = consts
    x2 = x.reshape(t, d)
    xb2 = xb.reshape(t, d)

    w_main = jnp.concatenate([w_in[:, :4096], w_in[:, 4112:6672], w_in[:, 6696:]], axis=1).astype(BF16)
    w_small = jnp.concatenate([w_in[:, 4096:4112], w_in[:, 6672:6696],
                               jnp.zeros((d, LANES - 40), w_in.dtype)], axis=1).astype(BF16)
    tm = min(1024, t)
    h_main = _matmul(xb2, w_main, F32, tm, 1536, "proj_main")
    h_small = _matmul(xb2, w_small, F32, tm, LANES, "proj_small")
    h_main3 = h_main.reshape(b_, s_len, N_MAIN)
    h_small3 = h_small.reshape(b_, s_len, LANES)

    o_a = _deltanet(h_main3, h_small3, conv_w, a_log, dt_bias, norm_w)
    ks, vs, kw, vw, kc, vc = _nsa_prep(h_main3, cosf, sinf, pe_k, w1_k.astype(BF16),
                                       w2_k.astype(BF16), pe_v, w1_v.astype(BF16),
                                       w2_v.astype(BF16))
    o_b = _nsa_attn(h_main3, h_small3, cosf, sinf, kc, vc, ks, vs, kw, vw, overlap_t)

    mixed = _merge(o_a.reshape(t, -1), o_b.reshape(t, -1), w_a.astype(BF16), w_b.astype(BF16),
                   h_main, tm, 512)
    x1, x1b = _outproj_ln(mixed, w_out.astype(BF16), x2, ln1_g, ln1_b, min(256, t))
    act = _ffn_act(x1b, w_gate.astype(BF16), w_up.astype(BF16), tm, 512)
    resid = _resid(x1, x1b, p_i.reshape(t, PLE_DIM).astype(BF16), w_ple.astype(BF16),
                   w_ple_gate.astype(BF16), tm, 512)
    y, yb = _ffn_out(act, w_down.astype(BF16), resid, ln2_g, ln2_b, min(256, t))
    return y.reshape(b_, s_len, d), yb.reshape(b_, s_len, d)


def kernel(x, p, w_in, dn_conv_w, dn_a_log, dn_dt_bias, dn_norm_w, cmp_pe_k, cmp_w1_k, cmp_w2_k, cmp_pe_v, cmp_w1_v, cmp_w2_v, w_branch_a, w_branch_b, w_out, ln1_g, ln1_b, w_ffn_gate, w_ffn_up, w_ffn_down, w_ple, w_ple_gate, ln2_g, ln2_b):
    consts = _nsa_constants(x.shape[1])
    xb = x.astype(BF16)
    for i in range(DEPTH):
        x, xb = _layer(x, xb, p[i], w_in[i], dn_conv_w[i], dn_a_log[i], dn_dt_bias[i], dn_norm_w[i],
                       cmp_pe_k[i], cmp_w1_k[i], cmp_w2_k[i], cmp_pe_v[i], cmp_w1_v[i], cmp_w2_v[i],
                       w_branch_a[i], w_branch_b[i], w_out[i], ln1_g[i], ln1_b[i],
                       w_ffn_gate[i], w_ffn_up[i], w_ffn_down[i], w_ple[i], w_ple_gate[i],
                       ln2_g[i], ln2_b[i], consts)
    return x
```

```python
import jax
import jax.numpy as jnp
from jax import lax
from jax.experimental import pallas as pl
from jax.experimental.pallas import tpu as pltpu

D_MODEL = 2048
DEPTH = 2
HEAD_DIM = 128
HEADS = 8
DN_CONV = 4
DN_CHUNK = 64
GROUPS = 2
HPG = HEADS // GROUPS
CMP_LEN = 32
CMP_STRIDE = 16
CMP_HIDDEN = 256
SEL_BLOCK = 64
SEL_TOPK = 16
WIN = 512
ROPE_THETA = 10000.0
D_FF = 5632
PLE_DIM = 256
ALPHA = (2.0 * DEPTH) ** 0.25
LN_EPS = 1e-5
NORM_EPS = 1e-6
NEG_INF = -1e30

LANES = 128
SUBLANES = 8
VMEM_LIMIT = 56 * 1024 * 1024

CB_DN_Q, CB_DN_K, CB_DN_V, CB_DN_Z = 0, 8, 16, 24
CB_NSA_Q = 32
CB_KV = 40
N_MAIN = 10752
COL_MERGE_A = 6656
COL_MERGE_B = 8704
SC_BETA, SC_DECAY, SC_GATE = 0, 8, 16

F32 = jnp.float32
BF16 = jnp.bfloat16


def _cparams(sem):
    return pltpu.CompilerParams(dimension_semantics=sem, vmem_limit_bytes=VMEM_LIMIT)


def _dot(a, b):
    return jnp.dot(a, b, preferred_element_type=F32)


def _dot_nt(a, b):
    return lax.dot_general(a, b, (((1,), (1,)), ((), ())), preferred_element_type=F32)


def _dot_tn(a, b):
    return lax.dot_general(a, b, (((0,), (0,)), ((), ())), preferred_element_type=F32)


def _sigmoid(x):
    return 1.0 / (1.0 + jnp.exp(-x))


def _silu(x):
    return x * _sigmoid(x)


def _layer_norm(y, g, b):
    mu = jnp.mean(y, axis=-1, keepdims=True)
    d = y - mu
    var = jnp.mean(d * d, axis=-1, keepdims=True)
    return d * lax.rsqrt(var + LN_EPS) * g + b


def _lane_col(x, idx):
    lane = lax.broadcasted_iota(jnp.int32, x.shape, 1)
    return jnp.sum(jnp.where(lane == idx, x, 0.0), axis=1, keepdims=True)


def _split2(x):
    hi = x.astype(BF16)
    return hi, (x - hi.astype(F32)).astype(BF16)


def _split3(x):
    x1 = x.astype(BF16)
    r = x - x1.astype(F32)
    x2 = r.astype(BF16)
    return x1, x2, (r - x2.astype(F32)).astype(BF16)


def _dot_01(ones_b, x):
    x1, x2, x3 = _split3(x)
    return _dot(ones_b, x1) + _dot(ones_b, x2) + _dot(ones_b, x3)


def _mm_kernel(a_ref, w_ref, o_ref):
    o_ref[...] = _dot(a_ref[...], w_ref[...]).astype(o_ref.dtype)


def _matmul(a, w, out_dtype, tm, tn, name):
    m, k = a.shape
    n = w.shape[1]
    return pl.pallas_call(
        _mm_kernel,
        out_shape=jax.ShapeDtypeStruct((m, n), out_dtype),
        grid=(m // tm, n // tn),
        in_specs=[pl.BlockSpec((tm, k), lambda i, j: (i, 0)),
                  pl.BlockSpec((k, tn), lambda i, j: (0, j))],
        out_specs=pl.BlockSpec((tm, tn), lambda i, j: (i, j)),
        compiler_params=_cparams(("parallel", "arbitrary")),
        name=name,
    )(a, w)


def _merge_kernel(oa_ref, ob_ref, wa_ref, wb_ref, ma_ref, mb_ref, o_ref):
    ya = _dot(oa_ref[...], wa_ref[...])
    yb = _dot(ob_ref[...], wb_ref[...])
    o_ref[...] = (_sigmoid(ma_ref[...]) * ya + _sigmoid(mb_ref[...]) * yb).astype(o_ref.dtype)


def _merge(o_a, o_b, w_a, w_b, h_main, tm, tn):
    m, k = o_a.shape
    n = w_a.shape[1]
    ca, cb = COL_MERGE_A // tn, COL_MERGE_B // tn
    return pl.pallas_call(
        _merge_kernel,
        out_shape=jax.ShapeDtypeStruct((m, n), BF16),
        grid=(m // tm, n // tn),
        in_specs=[pl.BlockSpec((tm, k), lambda i, j: (i, 0)),
                  pl.BlockSpec((tm, k), lambda i, j: (i, 0)),
                  pl.BlockSpec((k, tn), lambda i, j: (0, j)),
                  pl.BlockSpec((k, tn), lambda i, j: (0, j)),
                  pl.BlockSpec((tm, tn), lambda i, j: (i, ca + j)),
                  pl.BlockSpec((tm, tn), lambda i, j: (i, cb + j))],
        out_specs=pl.BlockSpec((tm, tn), lambda i, j: (i, j)),
        compiler_params=_cparams(("parallel", "arbitrary")),
        name="merge",
    )(o_a, o_b, w_a, w_b, h_main, h_main)


def _outproj_ln_kernel(mx_ref, w_ref, x_ref, g_ref, b_ref, o_ref, ob_ref):
    y = ALPHA * x_ref[...] + _dot(mx_ref[...], w_ref[...])
    out = _layer_norm(y, g_ref[...], b_ref[...])
    o_ref[...] = out
    ob_ref[...] = out.astype(BF16)


def _outproj_ln(mixed, w_out, x, g, b, tm):
    m, d = x.shape
    return pl.pallas_call(
        _outproj_ln_kernel,
        out_shape=(jax.ShapeDtypeStruct((m, d), F32), jax.ShapeDtypeStruct((m, d), BF16)),
        grid=(m // tm,),
        in_specs=[pl.BlockSpec((tm, d), lambda i: (i, 0)),
                  pl.BlockSpec((d, d), lambda i: (0, 0), pipeline_mode=pl.Buffered(1)),
                  pl.BlockSpec((tm, d), lambda i: (i, 0)),
                  pl.BlockSpec((1, d), lambda i: (0, 0)),
                  pl.BlockSpec((1, d), lambda i: (0, 0))],
        out_specs=(pl.BlockSpec((tm, d), lambda i: (i, 0)),
                   pl.BlockSpec((tm, d), lambda i: (i, 0))),
        compiler_params=_cparams(("parallel",)),
        name="outproj_ln",
    )(mixed, w_out, x, g.reshape(1, d), b.reshape(1, d))


def _ffn_act_kernel(x_ref, wg_ref, wu_ref, o_ref):
    xv = x_ref[...]
    o_ref[...] = (_silu(_dot(xv, wg_ref[...])) * _dot(xv, wu_ref[...])).astype(o_ref.dtype)


def _ffn_act(xb, w_gate, w_up, tm, tn):
    m, k = xb.shape
    n = w_gate.shape[1]
    return pl.pallas_call(
        _ffn_act_kernel,
        out_shape=jax.ShapeDtypeStruct((m, n), BF16),
        grid=(m // tm, n // tn),
        in_specs=[pl.BlockSpec((tm, k), lambda i, j: (i, 0)),
                  pl.BlockSpec((k, tn), lambda i, j: (0, j)),
                  pl.BlockSpec((k, tn), lambda i, j: (0, j))],
        out_specs=pl.BlockSpec((tm, tn), lambda i, j: (i, j)),
        compiler_params=_cparams(("parallel", "arbitrary")),
        name="ffn_act",
    )(xb, w_gate, w_up)


def _resid_kernel(x_ref, xb_ref, p_ref, wp_ref, wpg_ref, o_ref):
    ple = _dot(p_ref[...], wp_ref[...]) * _sigmoid(_dot(xb_ref[...], wpg_ref[...]))
    o_ref[...] = ALPHA * x_ref[...] + ple


def _resid(x1, x1b, pb, w_ple, w_ple_gate, tm, tn):
    m, d = x1.shape
    kp = pb.shape[1]
    return pl.pallas_call(
        _resid_kernel,
        out_shape=jax.ShapeDtypeStruct((m, d), F32),
        grid=(m // tm, d // tn),
        in_specs=[pl.BlockSpec((tm, tn), lambda i, j: (i, j)),
                  pl.BlockSpec((tm, d), lambda i, j: (i, 0)),
                  pl.BlockSpec((tm, kp), lambda i, j: (i, 0)),
                  pl.BlockSpec((kp, tn), lambda i, j: (0, j)),
                  pl.BlockSpec((d, tn), lambda i, j: (0, j))],
        out_specs=pl.BlockSpec((tm, tn), lambda i, j: (i, j)),
        compiler_params=_cparams(("parallel", "arbitrary")),
        name="ple_resid",
    )(x1, x1b, pb, w_ple, w_ple_gate)


def _ffn_out_kernel(act_ref, w_ref, r_ref, g_ref, b_ref, o_ref, ob_ref):
    out = _layer_norm(r_ref[...] + _dot(act_ref[...], w_ref[...]), g_ref[...], b_ref[...])
    o_ref[...] = out
    ob_ref[...] = out.astype(BF16)


def _ffn_out(act, w_down, resid, g, b, tm):
    m, kf = act.shape
    d = w_down.shape[1]
    return pl.pallas_call(
        _ffn_out_kernel,
        out_shape=(jax.ShapeDtypeStruct((m, d), F32), jax.ShapeDtypeStruct((m, d), BF16)),
        grid=(m // tm,),
        in_specs=[pl.BlockSpec((tm, kf), lambda i: (i, 0)),
                  pl.BlockSpec((kf, d), lambda i: (0, 0), pipeline_mode=pl.Buffered(1)),
                  pl.BlockSpec((tm, d), lambda i: (i, 0)),
                  pl.BlockSpec((1, d), lambda i: (0, 0)),
                  pl.BlockSpec((1, d), lambda i: (0, 0))],
        out_specs=(pl.BlockSpec((tm, d), lambda i: (i, 0)),
                   pl.BlockSpec((tm, d), lambda i: (i, 0))),
        compiler_params=_cparams(("parallel",)),
        name="ffn_out_ln",
    )(act, w_down, resid, g.reshape(1, d), b.reshape(1, d))


DN_TS = 256
DN_HB = 8


def _dn_kernel(alog_ref, dtb_ref,
               q_ref, k_ref, v_ref, z_ref, qp_ref, kp_ref, vp_ref, hs_ref,
               cwq_ref, cwk_ref, cwv_ref, nw_ref, o_ref, state_ref, cbuf_ref):
    hg = pl.program_id(1)
    s = pl.program_id(2)
    c = DN_CHUNK
    d = HEAD_DIM

    @pl.when(s == 0)
    def _():
        state_ref[...] = jnp.zeros_like(state_ref)

    conv_slots = []

    def conv_silu(x_ref, xp_ref, cw_ref, hb):
        cols = slice(hb * d, (hb + 1) * d)
        buf = cbuf_ref.at[len(conv_slots)]
        conv_slots.append(None)
        buf[0:SUBLANES, :] = jnp.where(s == 0, 0.0, xp_ref[:, cols])
        buf[SUBLANES:, :] = x_ref[:, cols]
        cw = cw_ref[:, cols]
        y = None
        for i in range(DN_CONV):
            off = SUBLANES - (DN_CONV - 1) + i
            term = buf[off:off + DN_TS, :] * cw[i:i + 1]
            y = term if y is None else y + term
        return _silu(y)

    def l2norm(x):
        return x * lax.rsqrt(jnp.sum(x * x, axis=-1, keepdims=True) + NORM_EPS)

    row = lax.broadcasted_iota(jnp.int32, (c, c), 0)
    col = lax.broadcasted_iota(jnp.int32, (c, c), 1)
    causal = row >= col
    strict = row > col
    tri_b = jnp.where(causal, 1.0, 0.0).astype(BF16)
    eye = jnp.where(row == col, 1.0, 0.0).astype(F32)
    ones8_b = jnp.ones((SUBLANES, c), BF16)
    hs = hs_ref[...]
    nw = nw_ref[...]

    nck = DN_TS // c
    pairs = [(hb, ci) for hb in range(DN_HB) for ci in range(nck)]
    qs, ks, vs, betas, gbs = [], [], [], [], []
    for hb in range(DN_HB):
        h = hg * DN_HB + hb
        q_all = l2norm(conv_silu(q_ref, qp_ref, cwq_ref, hb)) * (d ** -0.5)
        k_all = l2norm(conv_silu(k_ref, kp_ref, cwk_ref, hb))
        v_all = conv_silu(v_ref, vp_ref, cwv_ref, hb)
        beta_all = _sigmoid(_lane_col(hs, SC_BETA + h))
        a_all = _lane_col(hs, SC_DECAY + h) + dtb_ref[h]
        softplus = jnp.maximum(a_all, 0.0) + jnp.log(1.0 + jnp.exp(-jnp.abs(a_all)))
        g_all = -jnp.exp(jnp.zeros_like(a_all) + alog_ref[h]) * softplus
        for ci in range(nck):
            sl = slice(ci * c, (ci + 1) * c)
            qs.append(q_all[sl])
            ks.append(k_all[sl])
            vs.append(v_all[sl])
            betas.append(beta_all[sl])
            gbs.append(jnp.broadcast_to(g_all[sl], (c, LANES)))
    n = len(pairs)
    gcs = [_dot_01(tri_b, gbs[i]) for i in range(n)]
    gc_rows = [_dot_01(ones8_b, jnp.where(row <= col, gbs[i][:, :c], 0.0))[0:1] for i in range(n)]
    decays = [jnp.exp(jnp.where(causal, gcs[i][:, :c] - gc_rows[i], NEG_INF)) for i in range(n)]
    kbs = [ks[i] * betas[i] for i in range(n)]
    kbfs = [ks[i].astype(BF16) for i in range(n)]
    negs = [jnp.where(strict, -(_dot_nt(kbs[i].astype(BF16), kbfs[i]) * decays[i]), 0.0)
            for i in range(n)]
    accs = [eye + negs[i] for i in range(n)]
    nbf = [negs[i].astype(BF16) for i in range(n)]
    pws = [_dot(nbf[i], nbf[i]) for i in range(n)]
    for _ in range(4):
        ress = [_dot(pws[i].astype(BF16), jnp.concatenate([pws[i], accs[i]], axis=1).astype(BF16))
                for i in range(n)]
        pws = [ress[i][:, :c] for i in range(n)]
        accs = [accs[i] + ress[i][:, c:] for i in range(n)]
    tinvs = [(accs[i] + _dot(pws[i].astype(BF16), accs[i].astype(BF16))).astype(BF16)
             for i in range(n)]
    egs = [jnp.exp(gcs[i]) for i in range(n)]
    uws = [_dot(tinvs[i], jnp.concatenate([vs[i] * betas[i], kbs[i] * egs[i]], axis=1).astype(BF16))
           .astype(BF16) for i in range(n)]
    qks = [jnp.where(causal, _dot_nt(qs[i].astype(BF16), kbfs[i]) * decays[i], 0.0).astype(BF16)
           for i in range(n)]
    k_decs = [(ks[i] * jnp.exp(gcs[i][c - 1:c] - gcs[i])).astype(BF16) for i in range(n)]
    nbs = [_dot_tn(k_decs[i], uws[i]) for i in range(n)]
    prs = [_dot(qks[i], uws[i]) for i in range(n)]
    lhss = [jnp.concatenate([(qs[i] * egs[i] - prs[i][:, d:]).astype(BF16),
                             nbs[i][:, d:].astype(BF16)], axis=0) for i in range(n)]

    states = [state_ref[hb] for hb in range(DN_HB)]
    for ci in range(nck):
        sl = slice(ci * c, (ci + 1) * c)
        ress = [_dot(lhss[hb * nck + ci], states[hb].astype(BF16)) for hb in range(DN_HB)]
        for hb in range(DN_HB):
            i = hb * nck + ci
            cols = slice(hb * d, (hb + 1) * d)
            o = ress[hb][:c] + prs[i][:, :d]
            states[hb] = states[hb] * egs[i][c - 1:c] - ress[hb][c:] + nbs[i][:, :d]
            o = o * lax.rsqrt(jnp.mean(o * o, axis=-1, keepdims=True) + NORM_EPS) * nw
            o_ref[sl, cols] = (o * _silu(z_ref[sl, cols])).astype(o_ref.dtype)
    for hb in range(DN_HB):
        state_ref[hb] = states[hb]


def _deltanet(h_main, h_small, conv_w, a_log, dt_bias, norm_w):
    b_, s_len, _ = h_main.shape
    ts = DN_TS
    wd = DN_HB * HEAD_DIM
    ngrp = HEADS // DN_HB
    blk = lambda cb: pl.BlockSpec((None, ts, wd), lambda b, h, s: (b, s, cb // DN_HB + h))
    prev = lambda cb: pl.BlockSpec(
        (None, SUBLANES, wd),
        lambda b, h, s: (b, jnp.maximum(s * (ts // SUBLANES) - 1, 0), cb // DN_HB + h))
    cw = lambda cb: pl.BlockSpec((DN_CONV, wd), lambda b, h, s: (0, cb // DN_HB + h))
    smem = pl.BlockSpec(memory_space=pltpu.SMEM)
    return pl.pallas_call(
        _dn_kernel,
        out_shape=jax.ShapeDtypeStruct((b_, s_len, HEADS * HEAD_DIM), BF16),
        grid=(b_, ngrp, s_len // ts),
        in_specs=[smem, smem,
                  blk(CB_DN_Q), blk(CB_DN_K), blk(CB_DN_V), blk(CB_DN_Z),
                  prev(CB_DN_Q), prev(CB_DN_K), prev(CB_DN_V),
                  pl.BlockSpec((None, ts, LANES), lambda b, h, s: (b, s, 0)),
                  cw(CB_DN_Q), cw(CB_DN_K), cw(CB_DN_V),
                  pl.BlockSpec((1, LANES), lambda b, h, s: (0, 0))],
        out_specs=pl.BlockSpec((None, ts, wd), lambda b, h, s: (b, s, h)),
        scratch_shapes=[pltpu.VMEM((DN_HB, HEAD_DIM, HEAD_DIM), F32),
                        pltpu.VMEM((3 * DN_HB, SUBLANES + DN_TS, HEAD_DIM), F32)],
        compiler_params=_cparams(("parallel", "parallel", "arbitrary")),
        name="deltanet",
    )(a_log, dt_bias, h_main, h_main, h_main, h_main, h_main, h_main, h_main, h_small,
      conv_w, conv_w, conv_w, norm_w.reshape(1, LANES))


def _rope(x, cosf, sinf):
    return x * cosf + pltpu.roll(x, HEAD_DIM // 2, axis=1) * sinf


def _nsa_prep_kernel(kc_ref, vc_ref, ks_ref, vs_ref, kw_ref, vw_ref, cos_ref, sin_ref,
                     pek_ref, w1k_ref, w2k_ref, pev_ref, w1v_ref, w2v_ref,
                     kso_ref, vso_ref, kwo_ref, vwo_ref, kco_ref, vco_ref, buf_ref):
    cosf = cos_ref[...]
    sinf = sin_ref[...]
    s_len = ks_ref.shape[0]
    pos = lax.broadcasted_iota(jnp.int32, (s_len, LANES), 0)
    lane = lax.broadcasted_iota(jnp.int32, (s_len, LANES), 1)
    kso_ref[:, :HEAD_DIM] = _rope(ks_ref[...], cosf, sinf).astype(BF16)
    kso_ref[:, HEAD_DIM:] = jnp.where(pos // SEL_BLOCK == lane, NEG_INF, 0.0).astype(BF16)
    kwo_ref[...] = _rope(kw_ref[...], cosf, sinf).astype(BF16)
    ones = jnp.ones((s_len, HEAD_DIM), BF16)
    vso_ref[:, :HEAD_DIM] = vs_ref[...].astype(BF16)
    vso_ref[:, HEAD_DIM:] = ones
    vwo_ref[:, :HEAD_DIM] = vw_ref[...].astype(BF16)
    vwo_ref[:, HEAD_DIM:] = ones
    nch = buf_ref.shape[0] // CMP_STRIDE

    def compress(pe_ref, w1_ref, w2_ref, out_ref):
        a0 = jnp.zeros((nch, CMP_HIDDEN), F32)
        a1 = jnp.zeros((nch, CMP_HIDDEN), F32)
        for i in range(CMP_STRIDE):
            xi = buf_ref[pl.ds(i, nch, stride=CMP_STRIDE), :]
            lo = (xi + pe_ref[i:i + 1, :]).astype(BF16)
            hi = (xi + pe_ref[CMP_STRIDE + i:CMP_STRIDE + i + 1, :]).astype(BF16)
            a0 = a0 + _dot(lo, w1_ref[i * HEAD_DIM:(i + 1) * HEAD_DIM, :])
            a1 = a1 + _dot(hi, w1_ref[(CMP_STRIDE + i) * HEAD_DIM:(CMP_STRIDE + i + 1) * HEAD_DIM, :])
        hid = a0 + pltpu.roll(a1, nch - 1, axis=0)
        out_ref[...] = _dot(_silu(hid).astype(BF16), w2_ref[...]).astype(out_ref.dtype)

    buf_ref[...] = _rope(kc_ref[...], cosf, sinf)
    compress(pek_ref, w1k_ref, w2k_ref, kco_ref)
    buf_ref[...] = vc_ref[...]
    compress(pev_ref, w1v_ref, w2v_ref, vco_ref)


def _nsa_prep(h_main, cosf, sinf, pe_k, w1_k, w2_k, pe_v, w1_v, w2_v):
    b_, s_len, _ = h_main.shape
    nch = s_len // CMP_STRIDE
    kv = lambda i: pl.BlockSpec((None, s_len, LANES), lambda b, g: (b, 0, CB_KV + 2 * i + g))
    full = lambda shape: pl.BlockSpec(shape, lambda b, g: tuple(0 for _ in shape))
    assert s_len // SEL_BLOCK <= LANES
    seq_out = pl.BlockSpec((None, None, s_len, LANES), lambda b, g: (b, g, 0, 0))
    aug_out = pl.BlockSpec((None, None, s_len, 2 * LANES), lambda b, g: (b, g, 0, 0))
    cmp_out = pl.BlockSpec((None, None, nch, LANES), lambda b, g: (b, g, 0, 0))
    seq_shape = jax.ShapeDtypeStruct((b_, GROUPS, s_len, HEAD_DIM), BF16)
    aug_shape = jax.ShapeDtypeStruct((b_, GROUPS, s_len, 2 * HEAD_DIM), BF16)
    cmp_shape = jax.ShapeDtypeStruct((b_, GROUPS, nch, HEAD_DIM), BF16)
    return pl.pallas_call(
        _nsa_prep_kernel,
        out_shape=(aug_shape, aug_shape, seq_shape, aug_shape, cmp_shape, cmp_shape),
        grid=(b_, GROUPS),
        in_specs=[kv(0), kv(1), kv(2), kv(3), kv(4), kv(5),
                  full((s_len, LANES)), full((s_len, LANES)),
                  full((CMP_LEN, HEAD_DIM)), full((CMP_LEN * HEAD_DIM, CMP_HIDDEN)),
                  full((CMP_HIDDEN, HEAD_DIM)),
                  full((CMP_LEN, HEAD_DIM)), full((CMP_LEN * HEAD_DIM, CMP_HIDDEN)),
                  full((CMP_HIDDEN, HEAD_DIM))],
        out_specs=(aug_out, aug_out, seq_out, aug_out, cmp_out, cmp_out),
        scratch_shapes=[pltpu.VMEM((s_len, HEAD_DIM), F32)],
        compiler_params=_cparams(("parallel", "parallel")),
        name="nsa_prep",
    )(h_main, h_main, h_main, h_main, h_main, h_main, cosf, sinf,
      pe_k, w1_k, w2_k, pe_v, w1_v, w2_v)


NSA_TQ = 128
NSA_TK = 256


def _nsa_attn_kernel(q_ref, hs_ref, cos_ref, sin_ref, kc_ref, vc_ref, ks_ref, vs_ref,
                     kw_ref, vw_ref, ovt_ref, o_ref, os_ref):
    g = pl.program_id(1)
    qi = pl.program_id(2)
    tq, tk = NSA_TQ, NSA_TK
    rows = HPG * tq
    ns = ks_ref.shape[0] // SEL_BLOCK
    cosf = cos_ref[...]
    sinf = sin_ref[...]
    scale = HEAD_DIM ** -0.5
    qs = jnp.concatenate(
        [_rope(q_ref[:, hh * HEAD_DIM:(hh + 1) * HEAD_DIM], cosf, sinf) * scale
         for hh in range(HPG)], axis=0).astype(BF16)

    t_abs = qi * tq + lax.broadcasted_iota(jnp.int32, (tq, LANES), 0)
    lane = lax.broadcasted_iota(jnp.int32, (tq, LANES), 1)

    t_abs_k = qi * tq + lax.broadcasted_iota(jnp.int32, (tq, tk), 0)
    lane_k = lax.broadcasted_iota(jnp.int32, (tq, tk), 1)

    def add_bias(s_blk, bias):
        return (s_blk.reshape(HPG, tq, tk) + bias[None]).reshape(rows, tk)

    def fold(x):
        return x[:, :LANES], x[:, LANES:]

    neg_rows = jnp.full((rows, LANES), NEG_INF, F32)

    def running_max(score_blocks):
        mrun = neg_rows
        for s_blk in score_blocks:
            s0, s1 = fold(s_blk)
            mrun = jnp.maximum(mrun, jnp.maximum(s0, s1))
        return jnp.broadcast_to(jnp.max(mrun, axis=-1, keepdims=True), (rows, LANES))

    def probs(s_blk, m_b):
        s0, s1 = fold(s_blk)
        return jnp.exp(jnp.concatenate([(s0 - m_b).astype(BF16), (s1 - m_b).astype(BF16)], axis=1))

    def weighted_values(ps, value_blocks):
        acc = jnp.zeros((rows, 2 * HEAD_DIM), F32)
        for p_blk, v_blk in zip(ps, value_blocks):
            acc = acc + _dot(p_blk, v_blk)
        return acc[:, :HEAD_DIM] / acc[:, HEAD_DIM:]

    kb_last = (qi * tq) // tk
    nwin = WIN // tk + 1
    win_kbs = [kb_last - (nwin - 1) + d for d in range(nwin)]
    win_starts = [pl.multiple_of(jnp.maximum(kb, 0) * tk, tk) for kb in win_kbs]
    win_raw = [_dot_nt(qs, kw_ref[pl.ds(win_starts[d], tk), :]) for d in range(nwin)]

    mask_c = jnp.concatenate([lane * CMP_STRIDE + (CMP_LEN - 1) <= t_abs] * HPG, axis=0)
    s_c = _dot_nt(qs, kc_ref[...])
    m_c = jnp.max(jnp.where(mask_c, s_c, NEG_INF), axis=-1, keepdims=True)
    e_c = jnp.where(mask_c, jnp.exp(s_c - m_c), 0.0)
    l_c = jnp.sum(e_c, axis=-1, keepdims=True)
    p_c = jnp.where(l_c > 0.0, e_c / l_c, 0.0)

    win_s = []
    for d in range(nwin):
        diff = t_abs_k - (jnp.maximum(win_kbs[d], 0) * tk + lane_k)
        bias = jnp.where((diff >= 0) & (diff < WIN) & (win_kbs[d] >= 0), 0.0, NEG_INF)
        win_s.append(add_bias(win_raw[d], bias))
    win_m = running_max(win_s)

    o_c = _dot(p_c.astype(BF16), vc_ref[...])
    psum = p_c[0:tq]
    for hh in range(1, HPG):
        psum = psum + p_c[hh * tq:(hh + 1) * tq]
    p_hi, p_lo = _split2(psum)
    ovt = ovt_ref[...]
    imp = (_dot_nt(ovt, p_hi) + _dot_nt(ovt, p_lo))[:ns]
    blk = lax.broadcasted_iota(jnp.int32, (ns, tq), 0)
    cur = (qi * tq + lax.broadcasted_iota(jnp.int32, (ns, tq), 1)) // SEL_BLOCK
    forced = (blk == 0) | (blk == cur) | (blk == cur - 1)
    imp = jnp.where(forced, jnp.inf, jnp.where(blk <= cur, imp, -jnp.inf))

    win_p = [probs(s_blk, win_m) for s_blk in win_s]

    rank = jnp.zeros((ns, tq), F32)
    for i in range(ns):
        ci = imp[i:i + 1, :]
        before = (ci > imp) | ((ci == imp) & (blk > i))
        rank = rank + jnp.where(before, 1.0, 0.0)
    unsel_t = jnp.where(rank < float(min(SEL_TOPK, ns)), 0.0, 1.0)
    unsel = jnp.concatenate([unsel_t, jnp.zeros((LANES - ns, tq), F32)], axis=0).T
    q_aug = jnp.concatenate([qs, jnp.concatenate([unsel.astype(BF16)] * HPG, axis=0)], axis=1)

    o_w = weighted_values(win_p, [vw_ref[pl.ds(win_starts[d], tk), :] for d in range(nwin)])

    def sel_variant(n_full):
        def run():
            s_blks = [_dot_nt(q_aug, ks_ref[j * tk:(j + 1) * tk, :]) for j in range(n_full + 1)]
            bias = jnp.where(n_full * tk + lane_k <= t_abs_k, 0.0, NEG_INF)
            s_blks[n_full] = add_bias(s_blks[n_full], bias)
            m_b = running_max(s_blks)
            ps = [probs(s_blk, m_b) for s_blk in s_blks]
            os_ref[...] = weighted_values(
                ps, [vs_ref[j * tk:(j + 1) * tk, :] for j in range(n_full + 1)])
        return run

    for n_full in range(ks_ref.shape[0] // tk):
        pl.when(kb_last == n_full)(sel_variant(n_full))
    o_s = os_ref[...]

    hs = hs_ref[...]
    for hh in range(HPG):
        gbase = SC_GATE + (g * HPG + hh) * 3
        r = slice(hh * tq, (hh + 1) * tq)
        out = (_sigmoid(_lane_col(hs, gbase)) * o_c[r]
               + _sigmoid(_lane_col(hs, gbase + 1)) * o_s[r]
               + _sigmoid(_lane_col(hs, gbase + 2)) * o_w[r])
        o_ref[:, hh * HEAD_DIM:(hh + 1) * HEAD_DIM] = out.astype(o_ref.dtype)


def _nsa_attn(h_main, h_small, cosf, sinf, kc, vc, ks, vs, kw, vw, overlap_t):
    b_, s_len, _ = h_main.shape
    tq = NSA_TQ
    nch = kc.shape[2]
    assert nch == LANES, "the compressed-block axis is laid out on one vreg of lanes"
    assert s_len % NSA_TK == 0 and NSA_TK == 2 * LANES
    qw = HPG * HEAD_DIM
    seq = pl.BlockSpec((None, None, s_len, HEAD_DIM), lambda b, g, i: (b, g, 0, 0))
    aug = pl.BlockSpec((None, None, s_len, 2 * HEAD_DIM), lambda b, g, i: (b, g, 0, 0))
    cmp_ = pl.BlockSpec((None, None, nch, HEAD_DIM), lambda b, g, i: (b, g, 0, 0))
    return pl.pallas_call(
        _nsa_attn_kernel,
        out_shape=jax.ShapeDtypeStruct((b_, s_len, HEADS * HEAD_DIM), BF16),
        grid=(b_, GROUPS, s_len // tq),
        in_specs=[pl.BlockSpec((None, tq, qw), lambda b, g, i: (b, i, CB_NSA_Q * LANES // qw + g)),
                  pl.BlockSpec((None, tq, LANES), lambda b, g, i: (b, i, 0)),
                  pl.BlockSpec((tq, LANES), lambda b, g, i: (i, 0)),
                  pl.BlockSpec((tq, LANES), lambda b, g, i: (i, 0)),
                  cmp_, cmp_, aug, aug, seq, aug,
                  pl.BlockSpec(overlap_t.shape, lambda b, g, i: (0, 0))],
        out_specs=pl.BlockSpec((None, tq, qw), lambda b, g, i: (b, i, g)),
        scratch_shapes=[pltpu.VMEM((HPG * tq, HEAD_DIM), F32)],
        compiler_params=_cparams(("parallel", "parallel", "arbitrary")),
        name="nsa_attn",
    )(h_main, h_small, cosf, sinf, kc, vc, ks, vs, kw, vw, overlap_t)


def _nsa_constants(s_len):
    half = HEAD_DIM // 2
    inv_freq = ROPE_THETA ** (-jnp.arange(half, dtype=F32) / half)
    ang = jnp.arange(s_len, dtype=F32)[:, None] * inv_freq[None, :]
    cos, sin = jnp.cos(ang), jnp.sin(ang)
    cosf = jnp.concatenate([cos, cos], axis=-1)
    sinf = jnp.concatenate([-sin, sin], axis=-1)
    nch = s_len // CMP_STRIDE
    ns = s_len // SEL_BLOCK
    n = jnp.arange(nch)[:, None] * CMP_STRIDE
    j = jnp.arange(LANES)[None, :] * SEL_BLOCK
    overlap = ((n <= j + SEL_BLOCK - 1) & (n + CMP_LEN - 1 >= j)
               & (jnp.arange(nch)[:, None] < nch - CMP_LEN // CMP_STRIDE + 1)
               & (jnp.arange(LANES)[None, :] < ns)).astype(BF16)
    return cosf, sinf, overlap.T


def _layer(x, xb, p_i, w_in, conv_w, a_log, dt_bias, norm_w, pe_k, w1_k, w2_k, pe_v, w1_v, w2_v,
           w_a, w_b, w_out, ln1_g, ln1_b, w_gate, w_up, w_down, w_ple, w_ple_gate, ln2_g, ln2_b,
           consts):
    b_, s_len, d = x.shape
    t = b_ * s_len
    cosf, sinf, overlap_t = consts
    x2 = x.reshape(t, d)
    xb2 = xb.reshape(t, d)

    w_main = jnp.concatenate([w_in[:, :4096], w_in[:, 4112:6672], w_in[:, 6696:]], axis=1).astype(BF16)
    w_small = jnp.concatenate([w_in[:, 4096:4112], w_in[:, 6672:6696],
                               jnp.zeros((d, LANES - 40), w_in.dtype)], axis=1).astype(BF16)
    tm = min(1024, t)
    h_main = _matmul(xb2, w_main, F32, tm, 1536, "proj_main")
    h_small = _matmul(xb2, w_small, F32, tm, LANES, "proj_small")
    h_main3 = h_main.reshape(b_, s_len, N_MAIN)
    h_small3 = h_small.reshape(b_, s_len, LANES)

    o_a = _deltanet(h_main3, h_small3, conv_w, a_log, dt_bias, norm_w)
    ks, vs, kw, vw, kc, vc = _nsa_prep(h_main3, cosf, sinf, pe_k, w1_k.astype(BF16),
                                       w2_k.astype(BF16), pe_v, w1_v.astype(BF16),
                                       w2_v.astype(BF16))
    o_b = _nsa_attn(h_main3, h_small3, cosf, sinf, kc, vc, ks, vs, kw, vw, overlap_t)

    mixed = _merge(o_a.reshape(t, -1), o_b.reshape(t, -1), w_a.astype(BF16), w_b.astype(BF16),
                   h_main, tm, 512)
    x1, x1b = _outproj_ln(mixed, w_out.astype(BF16), x2, ln1_g, ln1_b, min(512, t))
    act = _ffn_act(x1b, w_gate.astype(BF16), w_up.astype(BF16), tm, 512)
    resid = _resid(x1, x1b, p_i.reshape(t, PLE_DIM).astype(BF16), w_ple.astype(BF16),
                   w_ple_gate.astype(BF16), tm, 1024)
    y, yb = _ffn_out(act, w_down.astype(BF16), resid, ln2_g, ln2_b, min(256, t))
    return y.reshape(b_, s_len, d), yb.reshape(b_, s_len, d)


def kernel(x, p, w_in, dn_conv_w, dn_a_log, dn_dt_bias, dn_norm_w, cmp_pe_k, cmp_w1_k, cmp_w2_k, cmp_pe_v, cmp_w1_v, cmp_w2_v, w_branch_a, w_branch_b, w_out, ln1_g, ln1_b, w_ffn_gate, w_ffn_up, w_ffn_down, w_ple, w_ple_gate, ln2_g, ln2_b):
    consts = _nsa_constants(x.shape[1])
    xb = x.astype(BF16)
    for i in range(DEPTH):
        x, xb = _layer(x, xb, p[i], w_in[i], dn_conv_w[i], dn_a_log[i], dn_dt_bias[i], dn_norm_w[i],
                       cmp_pe_k[i], cmp_w1_k[i], cmp_w2_k[i], cmp_pe_v[i], cmp_w1_v[i], cmp_w2_v[i],
                       w_branch_a[i], w_branch_b[i], w_out[i], ln1_g[i], ln1_b[i],
                       w_ffn_gate[i], w_ffn_up[i], w_ffn_down[i], w_ple[i], w_ple_gate[i],
                       ln2_g[i], ln2_b[i], consts)
    return x
```

```python
import jax
import jax.numpy as jnp
from jax import lax
from jax.experimental import pallas as pl
from jax.experimental.pallas import tpu as pltpu

D_MODEL = 2048
DEPTH = 2
HEAD_DIM = 128
HEADS = 8
DN_CONV = 4
DN_CHUNK = 64
GROUPS = 2
HPG = HEADS // GROUPS
CMP_LEN = 32
CMP_STRIDE = 16
CMP_HIDDEN = 256
SEL_BLOCK = 64
SEL_TOPK = 16
WIN = 512
ROPE_THETA = 10000.0
D_FF = 5632
PLE_DIM = 256
ALPHA = (2.0 * DEPTH) ** 0.25
LN_EPS = 1e-5
NORM_EPS = 1e-6
NEG_INF = -1e30

LANES = 128
SUBLANES = 8
VMEM_LIMIT = 56 * 1024 * 1024

CB_DN_Q, CB_DN_K, CB_DN_V, CB_DN_Z = 0, 8, 16, 24
CB_NSA_Q = 32
CB_KV = 40
N_MAIN = 10752
COL_MERGE_A = 6656
COL_MERGE_B = 8704
SC_BETA, SC_DECAY, SC_GATE = 0, 8, 16

F32 = jnp.float32
BF16 = jnp.bfloat16


def _cparams(sem):
    return pltpu.CompilerParams(dimension_semantics=sem, vmem_limit_bytes=VMEM_LIMIT)


def _dot(a, b):
    return jnp.dot(a, b, preferred_element_type=F32)


def _dot_nt(a, b):
    return lax.dot_general(a, b, (((1,), (1,)), ((), ())), preferred_element_type=F32)


def _dot_tn(a, b):
    return lax.dot_general(a, b, (((0,), (0,)), ((), ())), preferred_element_type=F32)


def _sigmoid(x):
    return 1.0 / (1.0 + jnp.exp(-x))


def _silu(x):
    return x * _sigmoid(x)


def _layer_norm(y, g, b):
    mu = jnp.mean(y, axis=-1, keepdims=True)
    d = y - mu
    var = jnp.mean(d * d, axis=-1, keepdims=True)
    return d * lax.rsqrt(var + LN_EPS) * g + b


def _lane_col(x, idx):
    lane = lax.broadcasted_iota(jnp.int32, x.shape, 1)
    return jnp.sum(jnp.where(lane == idx, x, 0.0), axis=1, keepdims=True)


def _split2(x):
    hi = x.astype(BF16)
    return hi, (x - hi.astype(F32)).astype(BF16)


def _split3(x):
    x1 = x.astype(BF16)
    r = x - x1.astype(F32)
    x2 = r.astype(BF16)
    return x1, x2, (r - x2.astype(F32)).astype(BF16)


def _dot_01(ones_b, x):
    x1, x2, x3 = _split3(x)
    return _dot(ones_b, x1) + _dot(ones_b, x2) + _dot(ones_b, x3)


def _mm_kernel(a_ref, w_ref, o_ref):
    o_ref[...] = _dot(a_ref[...], w_ref[...]).astype(o_ref.dtype)


def _matmul(a, w, out_dtype, tm, tn, name):
    m, k = a.shape
    n = w.shape[1]
    return pl.pallas_call(
        _mm_kernel,
        out_shape=jax.ShapeDtypeStruct((m, n), out_dtype),
        grid=(m // tm, n // tn),
        in_specs=[pl.BlockSpec((tm, k), lambda i, j: (i, 0)),
                  pl.BlockSpec((k, tn), lambda i, j: (0, j))],
        out_specs=pl.BlockSpec((tm, tn), lambda i, j: (i, j)),
        compiler_params=_cparams(("parallel", "arbitrary")),
        name=name,
    )(a, w)


def _merge_kernel(oa_ref, ob_ref, wa_ref, wb_ref, ma_ref, mb_ref, o_ref):
    ya = _dot(oa_ref[...], wa_ref[...])
    yb = _dot(ob_ref[...], wb_ref[...])
    o_ref[...] = (_sigmoid(ma_ref[...]) * ya + _sigmoid(mb_ref[...]) * yb).astype(o_ref.dtype)


def _merge(o_a, o_b, w_a, w_b, h_main, tm, tn):
    m, k = o_a.shape
    n = w_a.shape[1]
    ca, cb = COL_MERGE_A // tn, COL_MERGE_B // tn
    return pl.pallas_call(
        _merge_kernel,
        out_shape=jax.ShapeDtypeStruct((m, n), BF16),
        grid=(m // tm, n // tn),
        in_specs=[pl.BlockSpec((tm, k), lambda i, j: (i, 0)),
                  pl.BlockSpec((tm, k), lambda i, j: (i, 0)),
                  pl.BlockSpec((k, tn), lambda i, j: (0, j)),
                  pl.BlockSpec((k, tn), lambda i, j: (0, j)),
                  pl.BlockSpec((tm, tn), lambda i, j: (i, ca + j)),
                  pl.BlockSpec((tm, tn), lambda i, j: (i, cb + j))],
        out_specs=pl.BlockSpec((tm, tn), lambda i, j: (i, j)),
        compiler_params=_cparams(("parallel", "arbitrary")),
        name="merge",
    )(o_a, o_b, w_a, w_b, h_main, h_main)


def _outproj_ln_kernel(mx_ref, w_ref, x_ref, g_ref, b_ref, o_ref, ob_ref):
    y = ALPHA * x_ref[...] + _dot(mx_ref[...], w_ref[...])
    out = _layer_norm(y, g_ref[...], b_ref[...])
    o_ref[...] = out
    ob_ref[...] = out.astype(BF16)


def _outproj_ln(mixed, w_out, x, g, b, tm):
    m, d = x.shape
    return pl.pallas_call(
        _outproj_ln_kernel,
        out_shape=(jax.ShapeDtypeStruct((m, d), F32), jax.ShapeDtypeStruct((m, d), BF16)),
        grid=(m // tm,),
        in_specs=[pl.BlockSpec((tm, d), lambda i: (i, 0)),
                  pl.BlockSpec((d, d), lambda i: (0, 0), pipeline_mode=pl.Buffered(1)),
                  pl.BlockSpec((tm, d), lambda i: (i, 0)),
                  pl.BlockSpec((1, d), lambda i: (0, 0)),
                  pl.BlockSpec((1, d), lambda i: (0, 0))],
        out_specs=(pl.BlockSpec((tm, d), lambda i: (i, 0)),
                   pl.BlockSpec((tm, d), lambda i: (i, 0))),
        compiler_params=_cparams(("parallel",)),
        name="outproj_ln",
    )(mixed, w_out, x, g.reshape(1, d), b.reshape(1, d))


def _ffn_act_kernel(x_ref, wg_ref, wu_ref, o_ref):
    xv = x_ref[...]
    o_ref[...] = (_silu(_dot(xv, wg_ref[...])) * _dot(xv, wu_ref[...])).astype(o_ref.dtype)


def _ffn_act(xb, w_gate, w_up, tm, tn):
    m, k = xb.shape
    n = w_gate.shape[1]
    return pl.pallas_call(
        _ffn_act_kernel,
        out_shape=jax.ShapeDtypeStruct((m, n), BF16),
        grid=(m // tm, n // tn),
        in_specs=[pl.BlockSpec((tm, k), lambda i, j: (i, 0)),
                  pl.BlockSpec((k, tn), lambda i, j: (0, j)),
                  pl.BlockSpec((k, tn), lambda i, j: (0, j))],
        out_specs=pl.BlockSpec((tm, tn), lambda i, j: (i, j)),
        compiler_params=_cparams(("parallel", "arbitrary")),
        name="ffn_act",
    )(xb, w_gate, w_up)


def _resid_kernel(x_ref, xb_ref, p_ref, wp_ref, wpg_ref, o_ref):
    ple = _dot(p_ref[...], wp_ref[...]) * _sigmoid(_dot(xb_ref[...], wpg_ref[...]))
    o_ref[...] = ALPHA * x_ref[...] + ple


def _resid(x1, x1b, pb, w_ple, w_ple_gate, tm, tn):
    m, d = x1.shape
    kp = pb.shape[1]
    return pl.pallas_call(
        _resid_kernel,
        out_shape=jax.ShapeDtypeStruct((m, d), F32),
        grid=(m // tm, d // tn),
        in_specs=[pl.BlockSpec((tm, tn), lambda i, j: (i, j)),
                  pl.BlockSpec((tm, d), lambda i, j: (i, 0)),
                  pl.BlockSpec((tm, kp), lambda i, j: (i, 0)),
                  pl.BlockSpec((kp, tn), lambda i, j: (0, j)),
                  pl.BlockSpec((d, tn), lambda i, j: (0, j))],
        out_specs=pl.BlockSpec((tm, tn), lambda i, j: (i, j)),
        compiler_params=_cparams(("parallel", "arbitrary")),
        name="ple_resid",
    )(x1, x1b, pb, w_ple, w_ple_gate)


def _ffn_out_kernel(act_ref, w_ref, r_ref, g_ref, b_ref, o_ref, ob_ref):
    out = _layer_norm(r_ref[...] + _dot(act_ref[...], w_ref[...]), g_ref[...], b_ref[...])
    o_ref[...] = out
    ob_ref[...] = out.astype(BF16)


def _ffn_out(act, w_down, resid, g, b, tm):
    m, kf = act.shape
    d = w_down.shape[1]
    return pl.pallas_call(
        _ffn_out_kernel,
        out_shape=(jax.ShapeDtypeStruct((m, d), F32), jax.ShapeDtypeStruct((m, d), BF16)),
        grid=(m // tm,),
        in_specs=[pl.BlockSpec((tm, kf), lambda i: (i, 0)),
                  pl.BlockSpec((kf, d), lambda i: (0, 0), pipeline_mode=pl.Buffered(1)),
                  pl.BlockSpec((tm, d), lambda i: (i, 0)),
                  pl.BlockSpec((1, d), lambda i: (0, 0)),
                  pl.BlockSpec((1, d), lambda i: (0, 0))],
        out_specs=(pl.BlockSpec((tm, d), lambda i: (i, 0)),
                   pl.BlockSpec((tm, d), lambda i: (i, 0))),
        compiler_params=_cparams(("parallel",)),
        name="ffn_out_ln",
    )(act, w_down, resid, g.reshape(1, d), b.reshape(1, d))


DN_TS = 256
DN_HB = 8


def _dn_kernel(alog_ref, dtb_ref,
               q_ref, k_ref, v_ref, z_ref, qp_ref, kp_ref, vp_ref, hs_ref,
               cwq_ref, cwk_ref, cwv_ref, nw_ref, o_ref, state_ref, cbuf_ref):
    hg = pl.program_id(1)
    s = pl.program_id(2)
    c = DN_CHUNK
    d = HEAD_DIM

    @pl.when(s == 0)
    def _():
        state_ref[...] = jnp.zeros_like(state_ref)

    conv_slots = []

    def conv_silu(x_ref, xp_ref, cw_ref, hb):
        cols = slice(hb * d, (hb + 1) * d)
        buf = cbuf_ref.at[len(conv_slots)]
        conv_slots.append(None)
        buf[0:SUBLANES, :] = jnp.where(s == 0, 0.0, xp_ref[:, cols])
        buf[SUBLANES:, :] = x_ref[:, cols]
        cw = cw_ref[:, cols]
        y = None
        for i in range(DN_CONV):
            off = SUBLANES - (DN_CONV - 1) + i
            term = buf[off:off + DN_TS, :] * cw[i:i + 1]
            y = term if y is None else y + term
        return _silu(y)

    def l2norm(x):
        return x * lax.rsqrt(jnp.sum(x * x, axis=-1, keepdims=True) + NORM_EPS)

    row = lax.broadcasted_iota(jnp.int32, (c, c), 0)
    col = lax.broadcasted_iota(jnp.int32, (c, c), 1)
    causal = row >= col
    strict = row > col
    tri_b = jnp.where(causal, 1.0, 0.0).astype(BF16)
    eye = jnp.where(row == col, 1.0, 0.0).astype(F32)
    ones8_b = jnp.ones((SUBLANES, c), BF16)
    hs = hs_ref[...]
    nw = nw_ref[...]

    nck = DN_TS // c
    pairs = [(hb, ci) for hb in range(DN_HB) for ci in range(nck)]
    qs, ks, vs, betas, gbs = [], [], [], [], []
    for hb in range(DN_HB):
        h = hg * DN_HB + hb
        q_all = l2norm(conv_silu(q_ref, qp_ref, cwq_ref, hb)) * (d ** -0.5)
        k_all = l2norm(conv_silu(k_ref, kp_ref, cwk_ref, hb))
        v_all = conv_silu(v_ref, vp_ref, cwv_ref, hb)
        beta_all = _sigmoid(_lane_col(hs, SC_BETA + h))
        a_all = _lane_col(hs, SC_DECAY + h) + dtb_ref[h]
        softplus = jnp.maximum(a_all, 0.0) + jnp.log(1.0 + jnp.exp(-jnp.abs(a_all)))
        g_all = -jnp.exp(jnp.zeros_like(a_all) + alog_ref[h]) * softplus
        for ci in range(nck):
            sl = slice(ci * c, (ci + 1) * c)
            qs.append(q_all[sl])
            ks.append(k_all[sl])
            vs.append(v_all[sl])
            betas.append(beta_all[sl])
            gbs.append(jnp.broadcast_to(g_all[sl], (c, LANES)))
    n = len(pairs)
    gcs = [_dot_01(tri_b, gbs[i]) for i in range(n)]
    gc_rows = [_dot_01(ones8_b, jnp.where(row <= col, gbs[i][:, :c], 0.0))[0:1] for i in range(n)]
    decays = [jnp.exp(jnp.where(causal, gcs[i][:, :c] - gc_rows[i], NEG_INF)) for i in range(n)]
    kbs = [ks[i] * betas[i] for i in range(n)]
    kbfs = [ks[i].astype(BF16) for i in range(n)]
    negs = [jnp.where(strict, -(_dot_nt(kbs[i].astype(BF16), kbfs[i]) * decays[i]), 0.0)
            for i in range(n)]
    accs = [eye + negs[i] for i in range(n)]
    nbf = [negs[i].astype(BF16) for i in range(n)]
    pws = [_dot(nbf[i], nbf[i]) for i in range(n)]
    for _ in range(4):
        ress = [_dot(pws[i].astype(BF16), jnp.concatenate([pws[i], accs[i]], axis=1).astype(BF16))
                for i in range(n)]
        pws = [ress[i][:, :c] for i in range(n)]
        accs = [accs[i] + ress[i][:, c:] for i in range(n)]
    tinvs = [(accs[i] + _dot(pws[i].astype(BF16), accs[i].astype(BF16))).astype(BF16)
             for i in range(n)]
    egs = [jnp.exp(gcs[i]) for i in range(n)]
    uws = [_dot(tinvs[i], jnp.concatenate([vs[i] * betas[i], kbs[i] * egs[i]], axis=1).astype(BF16))
           .astype(BF16) for i in range(n)]
    qks = [jnp.where(causal, _dot_nt(qs[i].astype(BF16), kbfs[i]) * decays[i], 0.0).astype(BF16)
           for i in range(n)]
    k_decs = [(ks[i] * jnp.exp(gcs[i][c - 1:c] - gcs[i])).astype(BF16) for i in range(n)]
    nbs = [_dot_tn(k_decs[i], uws[i]) for i in range(n)]
    prs = [_dot(qks[i], uws[i]) for i in range(n)]
    lhss = [jnp.concatenate([(qs[i] * egs[i] - prs[i][:, d:]).astype(BF16),
                             nbs[i][:, d:].astype(BF16)], axis=0) for i in range(n)]

    states = [state_ref[hb] for hb in range(DN_HB)]
    for ci in range(nck):
        sl = slice(ci * c, (ci + 1) * c)
        ress = [_dot(lhss[hb * nck + ci], states[hb].astype(BF16)) for hb in range(DN_HB)]
        for hb in range(DN_HB):
            i = hb * nck + ci
            cols = slice(hb * d, (hb + 1) * d)
            o = ress[hb][:c] + prs[i][:, :d]
            states[hb] = states[hb] * egs[i][c - 1:c] - ress[hb][c:] + nbs[i][:, :d]
            o = o * lax.rsqrt(jnp.mean(o * o, axis=-1, keepdims=True) + NORM_EPS) * nw
            o_ref[sl, cols] = (o * _silu(z_ref[sl, cols])).astype(o_ref.dtype)
    for hb in range(DN_HB):
        state_ref[hb] = states[hb]


def _deltanet(h_main, h_small, conv_w, a_log, dt_bias, norm_w):
    b_, s_len, _ = h_main.shape
    ts = DN_TS
    wd = DN_HB * HEAD_DIM
    ngrp = HEADS // DN_HB
    blk = lambda cb: pl.BlockSpec((None, ts, wd), lambda b, h, s: (b, s, cb // DN_HB + h))
    prev = lambda cb: pl.BlockSpec(
        (None, SUBLANES, wd),
        lambda b, h, s: (b, jnp.maximum(s * (ts // SUBLANES) - 1, 0), cb // DN_HB + h))
    cw = lambda cb: pl.BlockSpec((DN_CONV, wd), lambda b, h, s: (0, cb // DN_HB + h))
    smem = pl.BlockSpec(memory_space=pltpu.SMEM)
    return pl.pallas_call(
        _dn_kernel,
        out_shape=jax.ShapeDtypeStruct((b_, s_len, HEADS * HEAD_DIM), BF16),
        grid=(b_, ngrp, s_len // ts),
        in_specs=[smem, smem,
                  blk(CB_DN_Q), blk(CB_DN_K), blk(CB_DN_V), blk(CB_DN_Z),
                  prev(CB_DN_Q), prev(CB_DN_K), prev(CB_DN_V),
                  pl.BlockSpec((None, ts, LANES), lambda b, h, s: (b, s, 0)),
                  cw(CB_DN_Q), cw(CB_DN_K), cw(CB_DN_V),
                  pl.BlockSpec((1, LANES), lambda b, h, s: (0, 0))],
        out_specs=pl.BlockSpec((None, ts, wd), lambda b, h, s: (b, s, h)),
        scratch_shapes=[pltpu.VMEM((DN_HB, HEAD_DIM, HEAD_DIM), F32),
                        pltpu.VMEM((3 * DN_HB, SUBLANES + DN_TS, HEAD_DIM), F32)],
        compiler_params=_cparams(("parallel", "parallel", "arbitrary")),
        name="deltanet",
    )(a_log, dt_bias, h_main, h_main, h_main, h_main, h_main, h_main, h_main, h_small,
      conv_w, conv_w, conv_w, norm_w.reshape(1, LANES))


def _rope(x, cosf, sinf):
    return x * cosf + pltpu.roll(x, HEAD_DIM // 2, axis=1) * sinf


def _nsa_prep_kernel(kc_ref, vc_ref, ks_ref, vs_ref, kw_ref, vw_ref, cos_ref, sin_ref,
                     pek_ref, w1k_ref, w2k_ref, pev_ref, w1v_ref, w2v_ref,
                     kso_ref, vso_ref, kwo_ref, vwo_ref, kco_ref, vco_ref, buf_ref):
    cosf = cos_ref[...]
    sinf = sin_ref[...]
    s_len = ks_ref.shape[0]
    pos = lax.broadcasted_iota(jnp.int32, (s_len, LANES), 0)
    lane = lax.broadcasted_iota(jnp.int32, (s_len, LANES), 1)
    kso_ref[:, :HEAD_DIM] = _rope(ks_ref[...], cosf, sinf).astype(BF16)
    kso_ref[:, HEAD_DIM:] = jnp.where(pos // SEL_BLOCK == lane, NEG_INF, 0.0).astype(BF16)
    kwo_ref[...] = _rope(kw_ref[...], cosf, sinf).astype(BF16)
    ones = jnp.ones((s_len, HEAD_DIM), BF16)
    vso_ref[:, :HEAD_DIM] = vs_ref[...].astype(BF16)
    vso_ref[:, HEAD_DIM:] = ones
    vwo_ref[:, :HEAD_DIM] = vw_ref[...].astype(BF16)
    vwo_ref[:, HEAD_DIM:] = ones
    nch = buf_ref.shape[0] // CMP_STRIDE

    def compress(pe_ref, w1_ref, w2_ref, out_ref):
        a0 = jnp.zeros((nch, CMP_HIDDEN), F32)
        a1 = jnp.zeros((nch, CMP_HIDDEN), F32)
        for i in range(CMP_STRIDE):
            xi = buf_ref[pl.ds(i, nch, stride=CMP_STRIDE), :]
            lo = (xi + pe_ref[i:i + 1, :]).astype(BF16)
            hi = (xi + pe_ref[CMP_STRIDE + i:CMP_STRIDE + i + 1, :]).astype(BF16)
            a0 = a0 + _dot(lo, w1_ref[i * HEAD_DIM:(i + 1) * HEAD_DIM, :])
            a1 = a1 + _dot(hi, w1_ref[(CMP_STRIDE + i) * HEAD_DIM:(CMP_STRIDE + i + 1) * HEAD_DIM, :])
        hid = a0 + pltpu.roll(a1, nch - 1, axis=0)
        out_ref[...] = _dot(_silu(hid).astype(BF16), w2_ref[...]).astype(out_ref.dtype)

    buf_ref[...] = _rope(kc_ref[...], cosf, sinf)
    compress(pek_ref, w1k_ref, w2k_ref, kco_ref)
    buf_ref[...] = vc_ref[...]
    compress(pev_ref, w1v_ref, w2v_ref, vco_ref)


def _nsa_prep(h_main, cosf, sinf, pe_k, w1_k, w2_k, pe_v, w1_v, w2_v):
    b_, s_len, _ = h_main.shape
    nch = s_len // CMP_STRIDE
    kv = lambda i: pl.BlockSpec((None, s_len, LANES), lambda b, g: (b, 0, CB_KV + 2 * i + g))
    full = lambda shape: pl.BlockSpec(shape, lambda b, g: tuple(0 for _ in shape))
    assert s_len // SEL_BLOCK <= LANES
    seq_out = pl.BlockSpec((None, None, s_len, LANES), lambda b, g: (b, g, 0, 0))
    aug_out = pl.BlockSpec((None, None, s_len, 2 * LANES), lambda b, g: (b, g, 0, 0))
    cmp_out = pl.BlockSpec((None, None, nch, LANES), lambda b, g: (b, g, 0, 0))
    seq_shape = jax.ShapeDtypeStruct((b_, GROUPS, s_len, HEAD_DIM), BF16)
    aug_shape = jax.ShapeDtypeStruct((b_, GROUPS, s_len, 2 * HEAD_DIM), BF16)
    cmp_shape = jax.ShapeDtypeStruct((b_, GROUPS, nch, HEAD_DIM), BF16)
    return pl.pallas_call(
        _nsa_prep_kernel,
        out_shape=(aug_shape, aug_shape, seq_shape, aug_shape, cmp_shape, cmp_shape),
        grid=(b_, GROUPS),
        in_specs=[kv(0), kv(1), kv(2), kv(3), kv(4), kv(5),
                  full((s_len, LANES)), full((s_len, LANES)),
                  full((CMP_LEN, HEAD_DIM)), full((CMP_LEN * HEAD_DIM, CMP_HIDDEN)),
                  full((CMP_HIDDEN, HEAD_DIM)),
                  full((CMP_LEN, HEAD_DIM)), full((CMP_LEN * HEAD_DIM, CMP_HIDDEN)),
                  full((CMP_HIDDEN, HEAD_DIM))],
        out_specs=(aug_out, aug_out, seq_out, aug_out, cmp_out, cmp_out),
        scratch_shapes=[pltpu.VMEM((s_len, HEAD_DIM), F32)],
        compiler_params=_cparams(("parallel", "parallel")),
        name="nsa_prep",
    )(h_main, h_main, h_main, h_main, h_main, h_main, cosf, sinf,
      pe_k, w1_k, w2_k, pe_v, w1_v, w2_v)


NSA_TQ = 128
NSA_TK = 256


def _nsa_attn_kernel(q_ref, hs_ref, cos_ref, sin_ref, kc_ref, vc_ref, ks_ref, vs_ref,
                     kw_ref, vw_ref, ovt_ref, o_ref, os_ref):
    g = pl.program_id(1)
    qi = pl.program_id(2)
    tq, tk = NSA_TQ, NSA_TK
    rows = HPG * tq
    ns = ks_ref.shape[0] // SEL_BLOCK
    cosf = cos_ref[...]
    sinf = sin_ref[...]
    scale = HEAD_DIM ** -0.5
    qs = jnp.concatenate(
        [_rope(q_ref[:, hh * HEAD_DIM:(hh + 1) * HEAD_DIM], cosf, sinf) * scale
         for hh in range(HPG)], axis=0).astype(BF16)

    t_abs = qi * tq + lax.broadcasted_iota(jnp.int32, (tq, LANES), 0)
    lane = lax.broadcasted_iota(jnp.int32, (tq, LANES), 1)

    t_abs_k = qi * tq + lax.broadcasted_iota(jnp.int32, (tq, tk), 0)
    lane_k = lax.broadcasted_iota(jnp.int32, (tq, tk), 1)

    def add_bias(s_blk, bias):
        return (s_blk.reshape(HPG, tq, tk) + bias[None]).reshape(rows, tk)

    def split_rows(dot_fn, lhs, rhs):
        half = lhs.shape[0] // 2
        return jnp.concatenate([dot_fn(lhs[:half], rhs), dot_fn(lhs[half:], rhs)], axis=0)

    def fold(x):
        return x[:, :LANES], x[:, LANES:]

    neg_rows = jnp.full((rows, LANES), NEG_INF, F32)

    def running_max(score_blocks):
        mrun = neg_rows
        for s_blk in score_blocks:
            s0, s1 = fold(s_blk)
            mrun = jnp.maximum(mrun, jnp.maximum(s0, s1))
        return jnp.broadcast_to(jnp.max(mrun, axis=-1, keepdims=True), (rows, LANES))

    def probs(s_blk, m_b):
        s0, s1 = fold(s_blk)
        return jnp.exp(jnp.concatenate([(s0 - m_b).astype(BF16), (s1 - m_b).astype(BF16)], axis=1))

    def weighted_values(ps, value_rows):
        acc = _dot(jnp.concatenate(ps, axis=1), value_rows)
        return acc[:, :HEAD_DIM] / acc[:, HEAD_DIM:]

    kb_last = (qi * tq) // tk
    nwin = WIN // tk + 1
    win_start = pl.multiple_of(jnp.maximum(kb_last - (nwin - 1), 0) * tk, tk)
    win_raw = split_rows(_dot_nt, qs, kw_ref[pl.ds(win_start, nwin * tk), :])

    mask_c = jnp.concatenate([lane * CMP_STRIDE + (CMP_LEN - 1) <= t_abs] * HPG, axis=0)
    s_c = split_rows(_dot_nt, qs, kc_ref[...])
    m_c = jnp.max(jnp.where(mask_c, s_c, NEG_INF), axis=-1, keepdims=True)
    e_c = jnp.where(mask_c, jnp.exp(s_c - m_c), 0.0)
    l_c = jnp.sum(e_c, axis=-1, keepdims=True)
    p_c = jnp.where(l_c > 0.0, e_c / l_c, 0.0)

    win_s = []
    for d in range(nwin):
        diff = t_abs_k - (win_start + d * tk + lane_k)
        bias = jnp.where((diff >= 0) & (diff < WIN), 0.0, NEG_INF)
        win_s.append(add_bias(win_raw[:, d * tk:(d + 1) * tk], bias))
    win_m = running_max(win_s)

    o_c = split_rows(_dot, p_c.astype(BF16), vc_ref[...])
    psum = p_c[0:tq]
    for hh in range(1, HPG):
        psum = psum + p_c[hh * tq:(hh + 1) * tq]
    p_hi, p_lo = _split2(psum)
    ovt = ovt_ref[...]
    imp = (_dot_nt(ovt, p_hi) + _dot_nt(ovt, p_lo))[:ns]
    blk = lax.broadcasted_iota(jnp.int32, (ns, tq), 0)
    cur = (qi * tq + lax.broadcasted_iota(jnp.int32, (ns, tq), 1)) // SEL_BLOCK
    forced = (blk == 0) | (blk == cur) | (blk == cur - 1)
    imp = jnp.where(forced, jnp.inf, jnp.where(blk <= cur, imp, -jnp.inf))

    win_p = [probs(s_blk, win_m) for s_blk in win_s]

    rank = jnp.zeros((ns, tq), F32)
    for i in range(ns):
        ci = imp[i:i + 1, :]
        before = (ci > imp) | ((ci == imp) & (blk > i))
        rank = rank + jnp.where(before, 1.0, 0.0)
    unsel_t = jnp.where(rank < float(min(SEL_TOPK, ns)), 0.0, 1.0)
    unsel = jnp.concatenate([unsel_t, jnp.zeros((LANES - ns, tq), F32)], axis=0).T
    q_aug = jnp.concatenate([qs, jnp.concatenate([unsel.astype(BF16)] * HPG, axis=0)], axis=1)

    o_w = weighted_values(win_p, vw_ref[pl.ds(win_start, nwin * tk), :])

    def sel_variant(n_full):
        def run():
            n_keys = (n_full + 1) * tk
            s_all = split_rows(_dot_nt, q_aug, ks_ref[0:n_keys, :])
            s_blks = [s_all[:, j * tk:(j + 1) * tk] for j in range(n_full + 1)]
            bias = jnp.where(n_full * tk + lane_k <= t_abs_k, 0.0, NEG_INF)
            s_blks[n_full] = add_bias(s_blks[n_full], bias)
            m_b = running_max(s_blks)
            ps = [probs(s_blk, m_b) for s_blk in s_blks]
            os_ref[...] = weighted_values(ps, vs_ref[0:n_keys, :])
        return run

    for n_full in range(ks_ref.shape[0] // tk):
        pl.when(kb_last == n_full)(sel_variant(n_full))
    o_s = os_ref[...]

    hs = hs_ref[...]
    for hh in range(HPG):
        gbase = SC_GATE + (g * HPG + hh) * 3
        r = slice(hh * tq, (hh + 1) * tq)
        out = (_sigmoid(_lane_col(hs, gbase)) * o_c[r]
               + _sigmoid(_lane_col(hs, gbase + 1)) * o_s[r]
               + _sigmoid(_lane_col(hs, gbase + 2)) * o_w[r])
        o_ref[:, hh * HEAD_DIM:(hh + 1) * HEAD_DIM] = out.astype(o_ref.dtype)


def _nsa_attn(h_main, h_small, cosf, sinf, kc, vc, ks, vs, kw, vw, overlap_t):
    b_, s_len, _ = h_main.shape
    tq = NSA_TQ
    nch = kc.shape[2]
    assert nch == LANES, "the compressed-block axis is laid out on one vreg of lanes"
    assert s_len % NSA_TK == 0 and NSA_TK == 2 * LANES
    qw = HPG * HEAD_DIM
    seq = pl.BlockSpec((None, None, s_len, HEAD_DIM), lambda b, g, i: (b, g, 0, 0))
    aug = pl.BlockSpec((None, None, s_len, 2 * HEAD_DIM), lambda b, g, i: (b, g, 0, 0))
    cmp_ = pl.BlockSpec((None, None, nch, HEAD_DIM), lambda b, g, i: (b, g, 0, 0))
    return pl.pallas_call(
        _nsa_attn_kernel,
        out_shape=jax.ShapeDtypeStruct((b_, s_len, HEADS * HEAD_DIM), BF16),
        grid=(b_, GROUPS, s_len // tq),
        in_specs=[pl.BlockSpec((None, tq, qw), lambda b, g, i: (b, i, CB_NSA_Q * LANES // qw + g)),
                  pl.BlockSpec((None, tq, LANES), lambda b, g, i: (b, i, 0)),
                  pl.BlockSpec((tq, LANES), lambda b, g, i: (i, 0)),
                  pl.BlockSpec((tq, LANES), lambda b, g, i: (i, 0)),
                  cmp_, cmp_, aug, aug, seq, aug,
                  pl.BlockSpec(overlap_t.shape, lambda b, g, i: (0, 0))],
        out_specs=pl.BlockSpec((None, tq, qw), lambda b, g, i: (b, i, g)),
        scratch_shapes=[pltpu.VMEM((HPG * tq, HEAD_DIM), F32)],
        compiler_params=_cparams(("parallel", "parallel", "arbitrary")),
        name="nsa_attn",
    )(h_main, h_small, cosf, sinf, kc, vc, ks, vs, kw, vw, overlap_t)


def _nsa_constants(s_len):
    half = HEAD_DIM // 2
    inv_freq = ROPE_THETA ** (-jnp.arange(half, dtype=F32) / half)
    ang = jnp.arange(s_len, dtype=F32)[:, None] * inv_freq[None, :]
    cos, sin = jnp.cos(ang), jnp.sin(ang)
    cosf = jnp.concatenate([cos, cos], axis=-1)
    sinf = jnp.concatenate([-sin, sin], axis=-1)
    nch = s_len // CMP_STRIDE
    ns = s_len // SEL_BLOCK
    n = jnp.arange(nch)[:, None] * CMP_STRIDE
    j = jnp.arange(LANES)[None, :] * SEL_BLOCK
    overlap = ((n <= j + SEL_BLOCK - 1) & (n + CMP_LEN - 1 >= j)
               & (jnp.arange(nch)[:, None] < nch - CMP_LEN // CMP_STRIDE + 1)
               & (jnp.arange(LANES)[None, :] < ns)).astype(BF16)
    return cosf, sinf, overlap.T


def _layer(x, xb, p_i, w_in, conv_w, a_log, dt_bias, norm_w, pe_k, w1_k, w2_k, pe_v, w1_v, w2_v,
           w_a, w_b, w_out, ln1_g, ln1_b, w_gate, w_up, w_down, w_ple, w_ple_gate, ln2_g, ln2_b,
           consts):
    b_, s_len, d = x.shape
    t = b_ * s_len
    cosf, sinf, overlap_t = consts
    x2 = x.reshape(t, d)
    xb2 = xb.reshape(t, d)

    w_main = jnp.concatenate([w_in[:, :4096], w_in[:, 4112:6672], w_in[:, 6696:]], axis=1).astype(BF16)
    w_small = jnp.concatenate([w_in[:, 4096:4112], w_in[:, 6672:6696],
                               jnp.zeros((d, LANES - 40), w_in.dtype)], axis=1).astype(BF16)
    tm = min(1024, t)
    h_main = _matmul(xb2, w_main, F32, tm, 1536, "proj_main")
    h_small = _matmul(xb2, w_small, F32, tm, LANES, "proj_small")
    h_main3 = h_main.reshape(b_, s_len, N_MAIN)
    h_small3 = h_small.reshape(b_, s_len, LANES)

    o_a = _deltanet(h_main3, h_small3, conv_w, a_log, dt_bias, norm_w)
    ks, vs, kw, vw, kc, vc = _nsa_prep(h_main3, cosf, sinf, pe_k, w1_k.astype(BF16),
                                       w2_k.astype(BF16), pe_v, w1_v.astype(BF16),
                                       w2_v.astype(BF16))
    o_b = _nsa_attn(h_main3, h_small3, cosf, sinf, kc, vc, ks, vs, kw, vw, overlap_t)

    mixed = _merge(o_a.reshape(t, -1), o_b.reshape(t, -1), w_a.astype(BF16), w_b.astype(BF16),
                   h_main, tm, 512)
    x1, x1b = _outproj_ln(mixed, w_out.astype(BF16), x2, ln1_g, ln1_b, min(512, t))
    act = _ffn_act(x1b, w_gate.astype(BF16), w_up.astype(BF16), tm, 512)
    resid = _resid(x1, x1b, p_i.reshape(t, PLE_DIM).astype(BF16), w_ple.astype(BF16),
                   w_ple_gate.astype(BF16), tm, 1024)
    y, yb = _ffn_out(act, w_down.astype(BF16), resid, ln2_g, ln2_b, min(256, t))
    return y.reshape(b_, s_len, d), yb.reshape(b_, s_len, d)


def kernel(x, p, w_in, dn_conv_w, dn_a_log, dn_dt_bias, dn_norm_w, cmp_pe_k, cmp_w1_k, cmp_w2_k, cmp_pe_v, cmp_w1_v, cmp_w2_v, w_branch_a, w_branch_b, w_out, ln1_g, ln1_b, w_ffn_gate, w_ffn_up, w_ffn_down, w_ple, w_ple_gate, ln2_g, ln2_b):
    consts = _nsa_constants(x.shape[1])
    xb = x.astype(BF16)
    for i in range(DEPTH):
        x, xb = _layer(x, xb, p[i], w_in[i], dn_conv_w[i], dn_a_log[i], dn_dt_bias[i], dn_norm_w[i],
                       cmp_pe_k[i], cmp_w1_k[i], cmp_w2_k[i], cmp_pe_v[i], cmp_w1_v[i], cmp_w2_v[i],
                       w_branch_a[i], w_branch_b[i], w_out[i], ln1_g[i], ln1_b[i],
                       w_ffn_gate[i], w_ffn_up[i], w_ffn_down[i], w_ple[i], w_ple_gate[i],
                       ln2_g[i], ln2_b[i], consts)
    return x
```

```python
import jax
import jax.numpy as jnp
from jax import lax
from jax.experimental import pallas as pl
from jax.experimental.pallas import tpu as pltpu

D_MODEL = 2048
DEPTH = 2
HEAD_DIM = 128
HEADS = 8
DN_CONV = 4
DN_CHUNK = 64
GROUPS = 2
HPG = HEADS // GROUPS
CMP_LEN = 32
CMP_STRIDE = 16
CMP_HIDDEN = 256
SEL_BLOCK = 64
SEL_TOPK = 16
WIN = 512
ROPE_THETA = 10000.0
D_FF = 5632
PLE_DIM = 256
ALPHA = (2.0 * DEPTH) ** 0.25
LN_EPS = 1e-5
NORM_EPS = 1e-6
NEG_INF = -1e30

LANES = 128
SUBLANES = 8
VMEM_LIMIT = 56 * 1024 * 1024

CB_DN_Q, CB_DN_K, CB_DN_V, CB_DN_Z = 0, 8, 16, 24
CB_NSA_Q = 32
CB_KV = 40
N_MAIN = 10752
COL_MERGE_A = 6656
COL_MERGE_B = 8704
SC_BETA, SC_DECAY, SC_GATE = 0, 8, 16

F32 = jnp.float32
BF16 = jnp.bfloat16


def _cparams(sem):
    return pltpu.CompilerParams(dimension_semantics=sem, vmem_limit_bytes=VMEM_LIMIT)


def _dot(a, b):
    return jnp.dot(a, b, preferred_element_type=F32)


def _dot_nt(a, b):
    return lax.dot_general(a, b, (((1,), (1,)), ((), ())), preferred_element_type=F32)


def _dot_tn(a, b):
    return lax.dot_general(a, b, (((0,), (0,)), ((), ())), preferred_element_type=F32)


def _sigmoid(x):
    return 0.5 * jnp.tanh(0.5 * x) + 0.5


def _silu(x):
    return x * _sigmoid(x)


def _layer_norm(y, g, b):
    mu = jnp.mean(y, axis=-1, keepdims=True)
    d = y - mu
    var = jnp.mean(d * d, axis=-1, keepdims=True)
    return d * lax.rsqrt(var + LN_EPS) * g + b


def _lane_col(x, idx):
    lane = lax.broadcasted_iota(jnp.int32, x.shape, 1)
    return jnp.sum(jnp.where(lane == idx, x, 0.0), axis=1, keepdims=True)


def _split2(x):
    hi = x.astype(BF16)
    return hi, (x - hi.astype(F32)).astype(BF16)


def _split3(x):
    x1 = x.astype(BF16)
    r = x - x1.astype(F32)
    x2 = r.astype(BF16)
    return x1, x2, (r - x2.astype(F32)).astype(BF16)


def _dot_01(ones_b, x):
    x1, x2, x3 = _split3(x)
    return _dot(ones_b, x1) + _dot(ones_b, x2) + _dot(ones_b, x3)


def _mm_kernel(a_ref, w_ref, o_ref):
    o_ref[...] = _dot(a_ref[...], w_ref[...]).astype(o_ref.dtype)


def _matmul(a, w, out_dtype, tm, tn, name):
    m, k = a.shape
    n = w.shape[1]
    return pl.pallas_call(
        _mm_kernel,
        out_shape=jax.ShapeDtypeStruct((m, n), out_dtype),
        grid=(m // tm, n // tn),
        in_specs=[pl.BlockSpec((tm, k), lambda i, j: (i, 0)),
                  pl.BlockSpec((k, tn), lambda i, j: (0, j))],
        out_specs=pl.BlockSpec((tm, tn), lambda i, j: (i, j)),
        compiler_params=_cparams(("parallel", "arbitrary")),
        name=name,
    )(a, w)


def _merge_kernel(oa_ref, ob_ref, wa_ref, wb_ref, ma_ref, mb_ref, o_ref):
    ya = _dot(oa_ref[...], wa_ref[...])
    yb = _dot(ob_ref[...], wb_ref[...])
    o_ref[...] = (_sigmoid(ma_ref[...]) * ya + _sigmoid(mb_ref[...]) * yb).astype(o_ref.dtype)


def _merge(o_a, o_b, w_a, w_b, h_main, tm, tn):
    m, k = o_a.shape
    n = w_a.shape[1]
    ca, cb = COL_MERGE_A // tn, COL_MERGE_B // tn
    return pl.pallas_call(
        _merge_kernel,
        out_shape=jax.ShapeDtypeStruct((m, n), BF16),
        grid=(m // tm, n // tn),
        in_specs=[pl.BlockSpec((tm, k), lambda i, j: (i, 0)),
                  pl.BlockSpec((tm, k), lambda i, j: (i, 0)),
                  pl.BlockSpec((k, tn), lambda i, j: (0, j)),
                  pl.BlockSpec((k, tn), lambda i, j: (0, j)),
                  pl.BlockSpec((tm, tn), lambda i, j: (i, ca + j)),
                  pl.BlockSpec((tm, tn), lambda i, j: (i, cb + j))],
        out_specs=pl.BlockSpec((tm, tn), lambda i, j: (i, j)),
        compiler_params=_cparams(("parallel", "arbitrary")),
        name="merge",
    )(o_a, o_b, w_a, w_b, h_main, h_main)


def _outproj_ln_kernel(mx_ref, w_ref, x_ref, g_ref, b_ref, o_ref, ob_ref):
    half = mx_ref.shape[0] // 2
    for r in (slice(0, half), slice(half, 2 * half)):
        y = ALPHA * x_ref[r, :] + _dot(mx_ref[r, :], w_ref[...])
        out = _layer_norm(y, g_ref[...], b_ref[...])
        o_ref[r, :] = out
        ob_ref[r, :] = out.astype(BF16)


def _outproj_ln(mixed, w_out, x, g, b, tm):
    m, d = x.shape
    return pl.pallas_call(
        _outproj_ln_kernel,
        out_shape=(jax.ShapeDtypeStruct((m, d), F32), jax.ShapeDtypeStruct((m, d), BF16)),
        grid=(m // tm,),
        in_specs=[pl.BlockSpec((tm, d), lambda i: (i, 0)),
                  pl.BlockSpec((d, d), lambda i: (0, 0), pipeline_mode=pl.Buffered(1)),
                  pl.BlockSpec((tm, d), lambda i: (i, 0)),
                  pl.BlockSpec((1, d), lambda i: (0, 0)),
                  pl.BlockSpec((1, d), lambda i: (0, 0))],
        out_specs=(pl.BlockSpec((tm, d), lambda i: (i, 0)),
                   pl.BlockSpec((tm, d), lambda i: (i, 0))),
        compiler_params=_cparams(("parallel",)),
        name="outproj_ln",
    )(mixed, w_out, x, g.reshape(1, d), b.reshape(1, d))


def _ffn_act_kernel(x_ref, wg_ref, wu_ref, o_ref):
    xv = x_ref[...]
    o_ref[...] = (_silu(_dot(xv, wg_ref[...])) * _dot(xv, wu_ref[...])).astype(o_ref.dtype)


def _ffn_act(xb, w_gate, w_up, tm, tn):
    m, k = xb.shape
    n = w_gate.shape[1]
    return pl.pallas_call(
        _ffn_act_kernel,
        out_shape=jax.ShapeDtypeStruct((m, n), BF16),
        grid=(m // tm, n // tn),
        in_specs=[pl.BlockSpec((tm, k), lambda i, j: (i, 0)),
                  pl.BlockSpec((k, tn), lambda i, j: (0, j)),
                  pl.BlockSpec((k, tn), lambda i, j: (0, j))],
        out_specs=pl.BlockSpec((tm, tn), lambda i, j: (i, j)),
        compiler_params=_cparams(("parallel", "arbitrary")),
        name="ffn_act",
    )(xb, w_gate, w_up)


def _resid_kernel(x_ref, xb_ref, p_ref, wp_ref, wpg_ref, o_ref):
    ple = _dot(p_ref[...], wp_ref[...]) * _sigmoid(_dot(xb_ref[...], wpg_ref[...]))
    o_ref[...] = ALPHA * x_ref[...] + ple


def _resid(x1, x1b, pb, w_ple, w_ple_gate, tm, tn):
    m, d = x1.shape
    kp = pb.shape[1]
    return pl.pallas_call(
        _resid_kernel,
        out_shape=jax.ShapeDtypeStruct((m, d), F32),
        grid=(m // tm, d // tn),
        in_specs=[pl.BlockSpec((tm, tn), lambda i, j: (i, j)),
                  pl.BlockSpec((tm, d), lambda i, j: (i, 0)),
                  pl.BlockSpec((tm, kp), lambda i, j: (i, 0)),
                  pl.BlockSpec((kp, tn), lambda i, j: (0, j)),
                  pl.BlockSpec((d, tn), lambda i, j: (0, j))],
        out_specs=pl.BlockSpec((tm, tn), lambda i, j: (i, j)),
        compiler_params=_cparams(("parallel", "arbitrary")),
        name="ple_resid",
    )(x1, x1b, pb, w_ple, w_ple_gate)


def _ffn_out_kernel(act_ref, w_ref, r_ref, g_ref, b_ref, o_ref, ob_ref):
    out = _layer_norm(r_ref[...] + _dot(act_ref[...], w_ref[...]), g_ref[...], b_ref[...])
    o_ref[...] = out
    ob_ref[...] = out.astype(BF16)


def _ffn_out(act, w_down, resid, g, b, tm):
    m, kf = act.shape
    d = w_down.shape[1]
    return pl.pallas_call(
        _ffn_out_kernel,
        out_shape=(jax.ShapeDtypeStruct((m, d), F32), jax.ShapeDtypeStruct((m, d), BF16)),
        grid=(m // tm,),
        in_specs=[pl.BlockSpec((tm, kf), lambda i: (i, 0)),
                  pl.BlockSpec((kf, d), lambda i: (0, 0), pipeline_mode=pl.Buffered(1)),
                  pl.BlockSpec((tm, d), lambda i: (i, 0)),
                  pl.BlockSpec((1, d), lambda i: (0, 0)),
                  pl.BlockSpec((1, d), lambda i: (0, 0))],
        out_specs=(pl.BlockSpec((tm, d), lambda i: (i, 0)),
                   pl.BlockSpec((tm, d), lambda i: (i, 0))),
        compiler_params=_cparams(("parallel",)),
        name="ffn_out_ln",
    )(act, w_down, resid, g.reshape(1, d), b.reshape(1, d))


DN_TS = 256
DN_HB = 8


def _dn_kernel(alog_ref, dtb_ref,
               q_ref, k_ref, v_ref, z_ref, qp_ref, kp_ref, vp_ref, hs_ref,
               cwq_ref, cwk_ref, cwv_ref, nw_ref, o_ref, state_ref, cbuf_ref):
    hg = pl.program_id(1)
    s = pl.program_id(2)
    c = DN_CHUNK
    d = HEAD_DIM

    @pl.when(s == 0)
    def _():
        state_ref[...] = jnp.zeros_like(state_ref)

    conv_slots = []

    def conv_silu(x_ref, xp_ref, cw_ref, hb):
        cols = slice(hb * d, (hb + 1) * d)
        buf = cbuf_ref.at[len(conv_slots)]
        conv_slots.append(None)
        buf[0:SUBLANES, :] = jnp.where(s == 0, 0.0, xp_ref[:, cols])
        buf[SUBLANES:, :] = x_ref[:, cols]
        cw = cw_ref[:, cols]
        y = None
        for i in range(DN_CONV):
            off = SUBLANES - (DN_CONV - 1) + i
            term = buf[off:off + DN_TS, :] * cw[i:i + 1]
            y = term if y is None else y + term
        return _silu(y)

    def l2norm(x):
        return x * lax.rsqrt(jnp.sum(x * x, axis=-1, keepdims=True) + NORM_EPS)

    row = lax.broadcasted_iota(jnp.int32, (c, c), 0)
    col = lax.broadcasted_iota(jnp.int32, (c, c), 1)
    causal = row >= col
    strict = row > col
    tri_b = jnp.where(causal, 1.0, 0.0).astype(BF16)
    eye = jnp.where(row == col, 1.0, 0.0).astype(F32)
    ones8_b = jnp.ones((SUBLANES, c), BF16)
    hs = hs_ref[...]
    nw = nw_ref[...]

    nck = DN_TS // c
    pairs = [(hb, ci) for hb in range(DN_HB) for ci in range(nck)]
    qs, ks, vs, betas, gbs = [], [], [], [], []
    for hb in range(DN_HB):
        h = hg * DN_HB + hb
        q_all = l2norm(conv_silu(q_ref, qp_ref, cwq_ref, hb)) * (d ** -0.5)
        k_all = l2norm(conv_silu(k_ref, kp_ref, cwk_ref, hb))
        v_all = conv_silu(v_ref, vp_ref, cwv_ref, hb)
        beta_all = _sigmoid(_lane_col(hs, SC_BETA + h))
        a_all = _lane_col(hs, SC_DECAY + h) + dtb_ref[h]
        softplus = jnp.maximum(a_all, 0.0) + jnp.log(1.0 + jnp.exp(-jnp.abs(a_all)))
        g_all = -jnp.exp(jnp.zeros_like(a_all) + alog_ref[h]) * softplus
        for ci in range(nck):
            sl = slice(ci * c, (ci + 1) * c)
            qs.append(q_all[sl])
            ks.append(k_all[sl])
            vs.append(v_all[sl])
            betas.append(beta_all[sl])
            gbs.append(jnp.broadcast_to(g_all[sl], (c, LANES)))
    n = len(pairs)
    gcs = [_dot_01(tri_b, gbs[i]) for i in range(n)]
    gc_rows = [_dot_01(ones8_b, jnp.where(row <= col, gbs[i][:, :c], 0.0))[0:1] for i in range(n)]
    decays = [jnp.exp(jnp.where(causal, gcs[i][:, :c] - gc_rows[i], NEG_INF)) for i in range(n)]
    kbs = [ks[i] * betas[i] for i in range(n)]
    kbfs = [ks[i].astype(BF16) for i in range(n)]
    negs = [jnp.where(strict, -(_dot_nt(kbs[i].astype(BF16), kbfs[i]) * decays[i]), 0.0)
            for i in range(n)]
    accs = [eye + negs[i] for i in range(n)]
    nbf = [negs[i].astype(BF16) for i in range(n)]
    pws = [_dot(nbf[i], nbf[i]) for i in range(n)]
    for _ in range(4):
        pbs = [pws[i].astype(BF16) for i in range(n)]
        accs = [accs[i] + _dot(pbs[i], accs[i].astype(BF16)) for i in range(n)]
        pws = [_dot(pbs[i], pbs[i]) for i in range(n)]
    tinvs = [(accs[i] + _dot(pws[i].astype(BF16), accs[i].astype(BF16))).astype(BF16)
             for i in range(n)]
    egs = [jnp.exp(gcs[i]) for i in range(n)]
    uws = [_dot(tinvs[i], jnp.concatenate([vs[i] * betas[i], kbs[i] * egs[i]], axis=1).astype(BF16))
           .astype(BF16) for i in range(n)]
    qks = [jnp.where(causal, _dot_nt(qs[i].astype(BF16), kbfs[i]) * decays[i], 0.0).astype(BF16)
           for i in range(n)]
    k_decs = [(ks[i] * jnp.exp(gcs[i][c - 1:c] - gcs[i])).astype(BF16) for i in range(n)]
    nbs = [_dot_tn(k_decs[i], uws[i]) for i in range(n)]
    prs = [_dot(qks[i], uws[i]) for i in range(n)]
    lhss = [jnp.concatenate([(qs[i] * egs[i] - prs[i][:, d:]).astype(BF16),
                             nbs[i][:, d:].astype(BF16)], axis=0) for i in range(n)]

    states = [state_ref[hb] for hb in range(DN_HB)]
    for ci in range(nck):
        sl = slice(ci * c, (ci + 1) * c)
        ress = [_dot(lhss[hb * nck + ci], states[hb].astype(BF16)) for hb in range(DN_HB)]
        for hb in range(DN_HB):
            i = hb * nck + ci
            cols = slice(hb * d, (hb + 1) * d)
            o = ress[hb][:c] + prs[i][:, :d]
            states[hb] = states[hb] * egs[i][c - 1:c] - ress[hb][c:] + nbs[i][:, :d]
            o = o * lax.rsqrt(jnp.mean(o * o, axis=-1, keepdims=True) + NORM_EPS) * nw
            o_ref[sl, cols] = (o * _silu(z_ref[sl, cols])).astype(o_ref.dtype)
    for hb in range(DN_HB):
        state_ref[hb] = states[hb]


def _deltanet(h_main, h_small, conv_w, a_log, dt_bias, norm_w):
    b_, s_len, _ = h_main.shape
    ts = DN_TS
    wd = DN_HB * HEAD_DIM
    ngrp = HEADS // DN_HB
    blk = lambda cb: pl.BlockSpec((None, ts, wd), lambda b, h, s: (b, s, cb // DN_HB + h))
    prev = lambda cb: pl.BlockSpec(
        (None, SUBLANES, wd),
        lambda b, h, s: (b, jnp.maximum(s * (ts // SUBLANES) - 1, 0), cb // DN_HB + h))
    cw = lambda cb: pl.BlockSpec((DN_CONV, wd), lambda b, h, s: (0, cb // DN_HB + h))
    smem = pl.BlockSpec(memory_space=pltpu.SMEM)
    return pl.pallas_call(
        _dn_kernel,
        out_shape=jax.ShapeDtypeStruct((b_, s_len, HEADS * HEAD_DIM), BF16),
        grid=(b_, ngrp, s_len // ts),
        in_specs=[smem, smem,
                  blk(CB_DN_Q), blk(CB_DN_K), blk(CB_DN_V), blk(CB_DN_Z),
                  prev(CB_DN_Q), prev(CB_DN_K), prev(CB_DN_V),
                  pl.BlockSpec((None, ts, LANES), lambda b, h, s: (b, s, 0)),
                  cw(CB_DN_Q), cw(CB_DN_K), cw(CB_DN_V),
                  pl.BlockSpec((1, LANES), lambda b, h, s: (0, 0))],
        out_specs=pl.BlockSpec((None, ts, wd), lambda b, h, s: (b, s, h)),
        scratch_shapes=[pltpu.VMEM((DN_HB, HEAD_DIM, HEAD_DIM), F32),
                        pltpu.VMEM((3 * DN_HB, SUBLANES + DN_TS, HEAD_DIM), F32)],
        compiler_params=_cparams(("parallel", "parallel", "arbitrary")),
        name="deltanet",
    )(a_log, dt_bias, h_main, h_main, h_main, h_main, h_main, h_main, h_main, h_small,
      conv_w, conv_w, conv_w, norm_w.reshape(1, LANES))


def _rope(x, cosf, sinf):
    return x * cosf + pltpu.roll(x, HEAD_DIM // 2, axis=1) * sinf


def _nsa_prep_kernel(kc_ref, vc_ref, ks_ref, vs_ref, kw_ref, vw_ref, cos_ref, sin_ref,
                     pek_ref, w1k_ref, w2k_ref, pev_ref, w1v_ref, w2v_ref,
                     kso_ref, vso_ref, kwo_ref, vwo_ref, kco_ref, vco_ref, buf_ref):
    cosf = cos_ref[...]
    sinf = sin_ref[...]
    s_len = ks_ref.shape[0]
    pos = lax.broadcasted_iota(jnp.int32, (s_len, LANES), 0)
    lane = lax.broadcasted_iota(jnp.int32, (s_len, LANES), 1)
    kso_ref[:, :HEAD_DIM] = _rope(ks_ref[...], cosf, sinf).astype(BF16)
    kso_ref[:, HEAD_DIM:] = jnp.where(pos // SEL_BLOCK == lane, NEG_INF, 0.0).astype(BF16)
    kwo_ref[...] = _rope(kw_ref[...], cosf, sinf).astype(BF16)
    ones = jnp.ones((s_len, HEAD_DIM), BF16)
    vso_ref[:, :HEAD_DIM] = vs_ref[...].astype(BF16)
    vso_ref[:, HEAD_DIM:] = ones
    vwo_ref[:, :HEAD_DIM] = vw_ref[...].astype(BF16)
    vwo_ref[:, HEAD_DIM:] = ones
    nch = buf_ref.shape[0] // CMP_STRIDE

    def compress(pe_ref, w1_ref, w2_ref, out_ref):
        a0 = jnp.zeros((nch, CMP_HIDDEN), F32)
        a1 = jnp.zeros((nch, CMP_HIDDEN), F32)
        for i in range(CMP_STRIDE):
            xi = buf_ref[pl.ds(i, nch, stride=CMP_STRIDE), :]
            lo = (xi + pe_ref[i:i + 1, :]).astype(BF16)
            hi = (xi + pe_ref[CMP_STRIDE + i:CMP_STRIDE + i + 1, :]).astype(BF16)
            a0 = a0 + _dot(lo, w1_ref[i * HEAD_DIM:(i + 1) * HEAD_DIM, :])
            a1 = a1 + _dot(hi, w1_ref[(CMP_STRIDE + i) * HEAD_DIM:(CMP_STRIDE + i + 1) * HEAD_DIM, :])
        hid = a0 + pltpu.roll(a1, nch - 1, axis=0)
        out_ref[...] = _dot(_silu(hid).astype(BF16), w2_ref[...]).astype(out_ref.dtype)

    buf_ref[...] = _rope(kc_ref[...], cosf, sinf)
    compress(pek_ref, w1k_ref, w2k_ref, kco_ref)
    buf_ref[...] = vc_ref[...]
    compress(pev_ref, w1v_ref, w2v_ref, vco_ref)


def _nsa_prep(h_main, cosf, sinf, pe_k, w1_k, w2_k, pe_v, w1_v, w2_v):
    b_, s_len, _ = h_main.shape
    nch = s_len // CMP_STRIDE
    kv = lambda i: pl.BlockSpec((None, s_len, LANES), lambda b, g: (b, 0, CB_KV + 2 * i + g))
    full = lambda shape: pl.BlockSpec(shape, lambda b, g: tuple(0 for _ in shape))
    assert s_len // SEL_BLOCK <= LANES
    seq_out = pl.BlockSpec((None, None, s_len, LANES), lambda b, g: (b, g, 0, 0))
    aug_out = pl.BlockSpec((None, None, s_len, 2 * LANES), lambda b, g: (b, g, 0, 0))
    cmp_out = pl.BlockSpec((None, None, nch, LANES), lambda b, g: (b, g, 0, 0))
    seq_shape = jax.ShapeDtypeStruct((b_, GROUPS, s_len, HEAD_DIM), BF16)
    aug_shape = jax.ShapeDtypeStruct((b_, GROUPS, s_len, 2 * HEAD_DIM), BF16)
    cmp_shape = jax.ShapeDtypeStruct((b_, GROUPS, nch, HEAD_DIM), BF16)
    return pl.pallas_call(
        _nsa_prep_kernel,
        out_shape=(aug_shape, aug_shape, seq_shape, aug_shape, cmp_shape, cmp_shape),
        grid=(b_, GROUPS),
        in_specs=[kv(0), kv(1), kv(2), kv(3), kv(4), kv(5),
                  full((s_len, LANES)), full((s_len, LANES)),
                  full((CMP_LEN, HEAD_DIM)), full((CMP_LEN * HEAD_DIM, CMP_HIDDEN)),
                  full((CMP_HIDDEN, HEAD_DIM)),
                  full((CMP_LEN, HEAD_DIM)), full((CMP_LEN * HEAD_DIM, CMP_HIDDEN)),
                  full((CMP_HIDDEN, HEAD_DIM))],
        out_specs=(aug_out, aug_out, seq_out, aug_out, cmp_out, cmp_out),
        scratch_shapes=[pltpu.VMEM((s_len, HEAD_DIM), F32)],
        compiler_params=_cparams(("parallel", "parallel")),
        name="nsa_prep",
    )(h_main, h_main, h_main, h_main, h_main, h_main, cosf, sinf,
      pe_k, w1_k, w2_k, pe_v, w1_v, w2_v)


NSA_TQ = 256
NSA_TK = 256


def _nsa_attn_kernel(q_ref, hs_ref, cos_ref, sin_ref, kc_ref, vc_ref, ks_ref, vs_ref,
                     kw_ref, vw_ref, ovt_ref, o_ref, os_ref):
    g = pl.program_id(1)
    qi = pl.program_id(2)
    tq, tk = NSA_TQ, NSA_TK
    rows = HPG * tq
    ns = ks_ref.shape[0] // SEL_BLOCK
    cosf = cos_ref[...]
    sinf = sin_ref[...]
    scale = HEAD_DIM ** -0.5
    qs = jnp.concatenate(
        [_rope(q_ref[:, hh * HEAD_DIM:(hh + 1) * HEAD_DIM], cosf, sinf) * scale
         for hh in range(HPG)], axis=0).astype(BF16)

    t_abs = qi * tq + lax.broadcasted_iota(jnp.int32, (tq, LANES), 0)
    lane = lax.broadcasted_iota(jnp.int32, (tq, LANES), 1)

    t_abs_k = qi * tq + lax.broadcasted_iota(jnp.int32, (tq, tk), 0)
    lane_k = lax.broadcasted_iota(jnp.int32, (tq, tk), 1)

    def add_bias(s_blk, bias):
        return (s_blk.reshape(HPG, tq, tk) + bias[None]).reshape(rows, tk)

    def split_rows(dot_fn, lhs, rhs):
        half = lhs.shape[0] // 2
        return jnp.concatenate([dot_fn(lhs[:half], rhs), dot_fn(lhs[half:], rhs)], axis=0)

    def fold(x):
        return x[:, :LANES], x[:, LANES:]

    neg_rows = jnp.full((rows, LANES), NEG_INF, F32)

    def running_max(score_blocks):
        mrun = neg_rows
        for s_blk in score_blocks:
            s0, s1 = fold(s_blk)
            mrun = jnp.maximum(mrun, jnp.maximum(s0, s1))
        return jnp.broadcast_to(jnp.max(mrun, axis=-1, keepdims=True), (rows, LANES))

    def probs(s_blk, m_b):
        s0, s1 = fold(s_blk)
        return jnp.exp(jnp.concatenate([(s0 - m_b).astype(BF16), (s1 - m_b).astype(BF16)], axis=1))

    def weighted_values(ps, value_rows):
        acc = _dot(jnp.concatenate(ps, axis=1), value_rows)
        return acc[:, :HEAD_DIM] / acc[:, HEAD_DIM:]

    kb_last = (qi * tq) // tk
    nwin = WIN // tk + 1
    win_start = pl.multiple_of(jnp.maximum(kb_last - (nwin - 1), 0) * tk, tk)
    win_raw = split_rows(_dot_nt, qs, kw_ref[pl.ds(win_start, nwin * tk), :])

    mask_c = jnp.concatenate([lane * CMP_STRIDE + (CMP_LEN - 1) <= t_abs] * HPG, axis=0)
    s_c = split_rows(_dot_nt, qs, kc_ref[...])
    m_c = jnp.max(jnp.where(mask_c, s_c, NEG_INF), axis=-1, keepdims=True)
    e_c = jnp.where(mask_c, jnp.exp(s_c - m_c), 0.0)
    l_c = jnp.sum(e_c, axis=-1, keepdims=True)
    p_c = jnp.where(l_c > 0.0, e_c / l_c, 0.0)

    win_s = []
    for d in range(nwin):
        diff = t_abs_k - (win_start + d * tk + lane_k)
        bias = jnp.where((diff >= 0) & (diff < WIN), 0.0, NEG_INF)
        win_s.append(add_bias(win_raw[:, d * tk:(d + 1) * tk], bias))
    win_m = running_max(win_s)

    o_c = split_rows(_dot, p_c.astype(BF16), vc_ref[...])
    psum = p_c[0:tq]
    for hh in range(1, HPG):
        psum = psum + p_c[hh * tq:(hh + 1) * tq]
    p_hi, p_lo = _split2(psum)
    ovt = ovt_ref[...]
    imp = (_dot_nt(ovt, p_hi) + _dot_nt(ovt, p_lo))[:ns]
    blk = lax.broadcasted_iota(jnp.int32, (ns, tq), 0)
    cur = (qi * tq + lax.broadcasted_iota(jnp.int32, (ns, tq), 1)) // SEL_BLOCK
    forced = (blk == 0) | (blk == cur) | (blk == cur - 1)
    imp = jnp.where(forced, jnp.inf, jnp.where(blk <= cur, imp, -jnp.inf))

    win_p = [probs(s_blk, win_m) for s_blk in win_s]

    rank = jnp.zeros((ns, tq), F32)
    for i in range(ns):
        ci = imp[i:i + 1, :]
        before = (ci > imp) | ((ci == imp) & (blk > i))
        rank = rank + jnp.where(before, 1.0, 0.0)
    unsel_t = jnp.where(rank < float(min(SEL_TOPK, ns)), 0.0, 1.0)
    unsel = jnp.concatenate([unsel_t, jnp.zeros((LANES - ns, tq), F32)], axis=0).T
    q_aug = jnp.concatenate([qs, jnp.concatenate([unsel.astype(BF16)] * HPG, axis=0)], axis=1)

    o_w = weighted_values(win_p, vw_ref[pl.ds(win_start, nwin * tk), :])

    def sel_variant(n_full):
        def run():
            n_keys = (n_full + 1) * tk
            s_all = split_rows(_dot_nt, q_aug, ks_ref[0:n_keys, :])
            s_blks = [s_all[:, j * tk:(j + 1) * tk] for j in range(n_full + 1)]
            bias = jnp.where(n_full * tk + lane_k <= t_abs_k, 0.0, NEG_INF)
            s_blks[n_full] = add_bias(s_blks[n_full], bias)
            m_b = running_max(s_blks)
            ps = [probs(s_blk, m_b) for s_blk in s_blks]
            os_ref[...] = weighted_values(ps, vs_ref[0:n_keys, :])
        return run

    for n_full in range(ks_ref.shape[0] // tk):
        pl.when(kb_last == n_full)(sel_variant(n_full))
    o_s = os_ref[...]

    hs = hs_ref[...]
    for hh in range(HPG):
        gbase = SC_GATE + (g * HPG + hh) * 3
        r = slice(hh * tq, (hh + 1) * tq)
        out = (_sigmoid(_lane_col(hs, gbase)) * o_c[r]
               + _sigmoid(_lane_col(hs, gbase + 1)) * o_s[r]
               + _sigmoid(_lane_col(hs, gbase + 2)) * o_w[r])
        o_ref[:, hh * HEAD_DIM:(hh + 1) * HEAD_DIM] = out.astype(o_ref.dtype)


def _nsa_attn(h_main, h_small, cosf, sinf, kc, vc, ks, vs, kw, vw, overlap_t):
    b_, s_len, _ = h_main.shape
    tq = NSA_TQ
    nch = kc.shape[2]
    assert nch == LANES, "the compressed-block axis is laid out on one vreg of lanes"
    assert s_len % NSA_TK == 0 and NSA_TK == 2 * LANES
    qw = HPG * HEAD_DIM
    seq = pl.BlockSpec((None, None, s_len, HEAD_DIM), lambda b, g, i: (b, g, 0, 0))
    aug = pl.BlockSpec((None, None, s_len, 2 * HEAD_DIM), lambda b, g, i: (b, g, 0, 0))
    cmp_ = pl.BlockSpec((None, None, nch, HEAD_DIM), lambda b, g, i: (b, g, 0, 0))
    return pl.pallas_call(
        _nsa_attn_kernel,
        out_shape=jax.ShapeDtypeStruct((b_, s_len, HEADS * HEAD_DIM), BF16),
        grid=(b_, GROUPS, s_len // tq),
        in_specs=[pl.BlockSpec((None, tq, qw), lambda b, g, i: (b, i, CB_NSA_Q * LANES // qw + g)),
                  pl.BlockSpec((None, tq, LANES), lambda b, g, i: (b, i, 0)),
                  pl.BlockSpec((tq, LANES), lambda b, g, i: (i, 0)),
                  pl.BlockSpec((tq, LANES), lambda b, g, i: (i, 0)),
                  cmp_, cmp_, aug, aug, seq, aug,
                  pl.BlockSpec(overlap_t.shape, lambda b, g, i: (0, 0))],
        out_specs=pl.BlockSpec((None, tq, qw), lambda b, g, i: (b, i, g)),
        scratch_shapes=[pltpu.VMEM((HPG * tq, HEAD_DIM), F32)],
        compiler_params=_cparams(("parallel", "parallel", "arbitrary")),
        name="nsa_attn",
    )(h_main, h_small, cosf, sinf, kc, vc, ks, vs, kw, vw, overlap_t)


def _nsa_constants(s_len):
    half = HEAD_DIM // 2
    inv_freq = ROPE_THETA ** (-jnp.arange(half, dtype=F32) / half)
    ang = jnp.arange(s_len, dtype=F32)[:, None] * inv_freq[None, :]
    cos, sin = jnp.cos(ang), jnp.sin(ang)
    cosf = jnp.concatenate([cos, cos], axis=-1)
    sinf = jnp.concatenate([-sin, sin], axis=-1)
    nch = s_len // CMP_STRIDE
    ns = s_len // SEL_BLOCK
    n = jnp.arange(nch)[:, None] * CMP_STRIDE
    j = jnp.arange(LANES)[None, :] * SEL_BLOCK
    overlap = ((n <= j + SEL_BLOCK - 1) & (n + CMP_LEN - 1 >= j)
               & (jnp.arange(nch)[:, None] < nch - CMP_LEN // CMP_STRIDE + 1)
               & (jnp.arange(LANES)[None, :] < ns)).astype(BF16)
    return cosf, sinf, overlap.T


def _layer(x, xb, p_i, w_in, conv_w, a_log, dt_bias, norm_w, pe_k, w1_k, w2_k, pe_v, w1_v, w2_v,
           w_a, w_b, w_out, ln1_g, ln1_b, w_gate, w_up, w_down, w_ple, w_ple_gate, ln2_g, ln2_b,
           consts):
    b_, s_len, d = x.shape
    t = b_ * s_len
    cosf, sinf, overlap_t = consts
    x2 = x.reshape(t, d)
    xb2 = xb.reshape(t, d)

    w_main = jnp.concatenate([w_in[:, :4096], w_in[:, 4112:6672], w_in[:, 6696:]], axis=1).astype(BF16)
    w_small = jnp.concatenate([w_in[:, 4096:4112], w_in[:, 6672:6696],
                               jnp.zeros((d, LANES - 40), w_in.dtype)], axis=1).astype(BF16)
    tm = min(1024, t)
    h_main = _matmul(xb2, w_main, F32, tm, 1536, "proj_main")
    h_small = _matmul(xb2, w_small, F32, tm, LANES, "proj_small")
    h_main3 = h_main.reshape(b_, s_len, N_MAIN)
    h_small3 = h_small.reshape(b_, s_len, LANES)

    o_a = _deltanet(h_main3, h_small3, conv_w, a_log, dt_bias, norm_w)
    ks, vs, kw, vw, kc, vc = _nsa_prep(h_main3, cosf, sinf, pe_k, w1_k.astype(BF16),
                                       w2_k.astype(BF16), pe_v, w1_v.astype(BF16),
                                       w2_v.astype(BF16))
    o_b = _nsa_attn(h_main3, h_small3, cosf, sinf, kc, vc, ks, vs, kw, vw, overlap_t)

    mixed = _merge(o_a.reshape(t, -1), o_b.reshape(t, -1), w_a.astype(BF16), w_b.astype(BF16),
                   h_main, tm, 512)
    x1, x1b = _outproj_ln(mixed, w_out.astype(BF16), x2, ln1_g, ln1_b, min(512, t))
    act = _ffn_act(x1b, w_gate.astype(BF16), w_up.astype(BF16), tm, 512)
    resid = _resid(x1, x1b, p_i.reshape(t, PLE_DIM).astype(BF16), w_ple.astype(BF16),
                   w_ple_gate.astype(BF16), tm, 1024)
    y, yb = _ffn_out(act, w_down.astype(BF16), resid, ln2_g, ln2_b, min(256, t))
    return y.reshape(b_, s_len, d), yb.reshape(b_, s_len, d)


def kernel(x, p, w_in, dn_conv_w, dn_a_log, dn_dt_bias, dn_norm_w, cmp_pe_k, cmp_w1_k, cmp_w2_k, cmp_pe_v, cmp_w1_v, cmp_w2_v, w_branch_a, w_branch_b, w_out, ln1_g, ln1_b, w_ffn_gate, w_ffn_up, w_ffn_down, w_ple, w_ple_gate, ln2_g, ln2_b):
    consts = _nsa_constants(x.shape[1])
    xb = x.astype(BF16)
    for i in range(DEPTH):
        x, xb = _layer(x, xb, p[i], w_in[i], dn_conv_w[i], dn_a_log[i], dn_dt_bias[i], dn_norm_w[i],
                       cmp_pe_k[i], cmp_w1_k[i], cmp_w2_k[i], cmp_pe_v[i], cmp_w1_v[i], cmp_w2_v[i],
                       w_branch_a[i], w_branch_b[i], w_out[i], ln1_g[i], ln1_b[i],
                       w_ffn_gate[i], w_ffn_up[i], w_ffn_down[i], w_ple[i], w_ple_gate[i],
                       ln2_g[i], ln2_b[i], consts)
    return x
```

```python
import jax
import jax.numpy as jnp
from jax import lax
from jax.experimental import pallas as pl
from jax.experimental.pallas import tpu as pltpu

D_MODEL = 2048
DEPTH = 2
HEAD_DIM = 128
HEADS = 8
DN_CONV = 4
DN_CHUNK = 64
GROUPS = 2
HPG = HEADS // GROUPS
CMP_LEN = 32
CMP_STRIDE = 16
CMP_HIDDEN = 256
SEL_BLOCK = 64
SEL_TOPK = 16
WIN = 512
ROPE_THETA = 10000.0
D_FF = 5632
PLE_DIM = 256
ALPHA = (2.0 * DEPTH) ** 0.25
LN_EPS = 1e-5
NORM_EPS = 1e-6
NEG_INF = -1e30

LANES = 128
SUBLANES = 8
VMEM_LIMIT = 56 * 1024 * 1024

CB_DN_Q, CB_DN_K, CB_DN_V, CB_DN_Z = 0, 8, 16, 24
CB_NSA_Q = 64
CB_KV = 72
N_MAIN = 10752
COL_MERGE_A = 4096
COL_MERGE_B = 6144
SC_BETA, SC_DECAY, SC_GATE = 0, 8, 16

F32 = jnp.float32
BF16 = jnp.bfloat16


def _cparams(sem):
    return pltpu.CompilerParams(dimension_semantics=sem, vmem_limit_bytes=VMEM_LIMIT)


def _dot(a, b):
    return jnp.dot(a, b, preferred_element_type=F32)


def _dot_nt(a, b):
    return lax.dot_general(a, b, (((1,), (1,)), ((), ())), preferred_element_type=F32)


def _dot_tn(a, b):
    return lax.dot_general(a, b, (((0,), (0,)), ((), ())), preferred_element_type=F32)


def _sigmoid(x):
    return 0.5 * jnp.tanh(0.5 * x) + 0.5


def _silu(x):
    return x * _sigmoid(x)


def _layer_norm(y, g, b):
    mu = jnp.mean(y, axis=-1, keepdims=True)
    d = y - mu
    var = jnp.mean(d * d, axis=-1, keepdims=True)
    return d * lax.rsqrt(var + LN_EPS) * g + b


def _lane_col(x, idx):
    lane = lax.broadcasted_iota(jnp.int32, x.shape, 1)
    return jnp.sum(jnp.where(lane == idx, x, 0.0), axis=1, keepdims=True)


def _split2(x):
    hi = x.astype(BF16)
    return hi, (x - hi.astype(F32)).astype(BF16)


def _split3(x):
    x1 = x.astype(BF16)
    r = x - x1.astype(F32)
    x2 = r.astype(BF16)
    return x1, x2, (r - x2.astype(F32)).astype(BF16)


def _dot_01(ones_b, x):
    x1, x2, x3 = _split3(x)
    return _dot(ones_b, x1) + _dot(ones_b, x2) + _dot(ones_b, x3)


def _mm_kernel(a_ref, w_ref, o_ref):
    o_ref[...] = _dot(a_ref[...], w_ref[...]).astype(o_ref.dtype)


def _matmul(a, w, out_dtype, tm, tn, name):
    m, k = a.shape
    n = w.shape[1]
    return pl.pallas_call(
        _mm_kernel,
        out_shape=jax.ShapeDtypeStruct((m, n), out_dtype),
        grid=(m // tm, n // tn),
        in_specs=[pl.BlockSpec((tm, k), lambda i, j: (i, 0)),
                  pl.BlockSpec((k, tn), lambda i, j: (0, j))],
        out_specs=pl.BlockSpec((tm, tn), lambda i, j: (i, j)),
        compiler_params=_cparams(("parallel", "arbitrary")),
        name=name,
    )(a, w)


MERGE_CHUNK = 512


def _merge_kernel(oa_ref, ob_ref, wa_ref, wb_ref, ma_ref, mb_ref, o_ref):
    oa = oa_ref[...]
    ob = ob_ref[...]
    for c in range(o_ref.shape[1] // MERGE_CHUNK):
        cols = slice(c * MERGE_CHUNK, (c + 1) * MERGE_CHUNK)
        ya = _dot(oa, wa_ref[:, cols])
        yb = _dot(ob, wb_ref[:, cols])
        o_ref[:, cols] = (_sigmoid(ma_ref[:, cols]) * ya
                          + _sigmoid(mb_ref[:, cols]) * yb).astype(o_ref.dtype)


def _merge(o_a, o_b, w_a, w_b, h_main, tm):
    m, k = o_a.shape
    n = w_a.shape[1]
    ca, cb = COL_MERGE_A // n, COL_MERGE_B // n
    return pl.pallas_call(
        _merge_kernel,
        out_shape=jax.ShapeDtypeStruct((m, n), BF16),
        grid=(m // tm,),
        in_specs=[pl.BlockSpec((tm, k), lambda i: (i, 0)),
                  pl.BlockSpec((tm, k), lambda i: (i, 0)),
                  pl.BlockSpec((k, n), lambda i: (0, 0), pipeline_mode=pl.Buffered(1)),
                  pl.BlockSpec((k, n), lambda i: (0, 0), pipeline_mode=pl.Buffered(1)),
                  pl.BlockSpec((tm, n), lambda i: (i, ca)),
                  pl.BlockSpec((tm, n), lambda i: (i, cb))],
        out_specs=pl.BlockSpec((tm, n), lambda i: (i, 0)),
        compiler_params=_cparams(("parallel",)),
        name="merge",
    )(o_a, o_b, w_a, w_b, h_main, h_main)


def _outproj_ln_kernel(mx_ref, w_ref, x_ref, g_ref, b_ref, o_ref, ob_ref):
    half = mx_ref.shape[0] // 2
    for r in (slice(0, half), slice(half, 2 * half)):
        y = ALPHA * x_ref[r, :] + _dot(mx_ref[r, :], w_ref[...])
        out = _layer_norm(y, g_ref[...], b_ref[...])
        o_ref[r, :] = out
        ob_ref[r, :] = out.astype(BF16)


def _outproj_ln(mixed, w_out, x, g, b, tm):
    m, d = x.shape
    return pl.pallas_call(
        _outproj_ln_kernel,
        out_shape=(jax.ShapeDtypeStruct((m, d), F32), jax.ShapeDtypeStruct((m, d), BF16)),
        grid=(m // tm,),
        in_specs=[pl.BlockSpec((tm, d), lambda i: (i, 0)),
                  pl.BlockSpec((d, d), lambda i: (0, 0), pipeline_mode=pl.Buffered(1)),
                  pl.BlockSpec((tm, d), lambda i: (i, 0)),
                  pl.BlockSpec((1, d), lambda i: (0, 0)),
                  pl.BlockSpec((1, d), lambda i: (0, 0))],
        out_specs=(pl.BlockSpec((tm, d), lambda i: (i, 0)),
                   pl.BlockSpec((tm, d), lambda i: (i, 0))),
        compiler_params=_cparams(("parallel",)),
        name="outproj_ln",
    )(mixed, w_out, x, g.reshape(1, d), b.reshape(1, d))


def _ffn_act_kernel(x_ref, wg_ref, wu_ref, o_ref):
    xv = x_ref[...]
    o_ref[...] = (_silu(_dot(xv, wg_ref[...])) * _dot(xv, wu_ref[...])).astype(o_ref.dtype)


def _ffn_act(xb, w_gate, w_up, tm, tn):
    m, k = xb.shape
    n = w_gate.shape[1]
    return pl.pallas_call(
        _ffn_act_kernel,
        out_shape=jax.ShapeDtypeStruct((m, n), BF16),
        grid=(m // tm, n // tn),
        in_specs=[pl.BlockSpec((tm, k), lambda i, j: (i, 0)),
                  pl.BlockSpec((k, tn), lambda i, j: (0, j)),
                  pl.BlockSpec((k, tn), lambda i, j: (0, j))],
        out_specs=pl.BlockSpec((tm, tn), lambda i, j: (i, j)),
        compiler_params=_cparams(("parallel", "arbitrary")),
        name="ffn_act",
    )(xb, w_gate, w_up)


def _resid_kernel(x_ref, xb_ref, p_ref, wp_ref, wpg_ref, o_ref):
    ple = _dot(p_ref[...], wp_ref[...]) * _sigmoid(_dot(xb_ref[...], wpg_ref[...]))
    o_ref[...] = ALPHA * x_ref[...] + ple


def _resid(x1, x1b, pb, w_ple, w_ple_gate, tm, tn):
    m, d = x1.shape
    kp = pb.shape[1]
    return pl.pallas_call(
        _resid_kernel,
        out_shape=jax.ShapeDtypeStruct((m, d), F32),
        grid=(m // tm, d // tn),
        in_specs=[pl.BlockSpec((tm, tn), lambda i, j: (i, j)),
                  pl.BlockSpec((tm, d), lambda i, j: (i, 0)),
                  pl.BlockSpec((tm, kp), lambda i, j: (i, 0)),
                  pl.BlockSpec((kp, tn), lambda i, j: (0, j)),
                  pl.BlockSpec((d, tn), lambda i, j: (0, j))],
        out_specs=pl.BlockSpec((tm, tn), lambda i, j: (i, j)),
        compiler_params=_cparams(("parallel", "arbitrary")),
        name="ple_resid",
    )(x1, x1b, pb, w_ple, w_ple_gate)


def _ffn_out_kernel(act_ref, w_ref, r_ref, g_ref, b_ref, o_ref, ob_ref):
    half = act_ref.shape[0] // 2
    for r in (slice(0, half), slice(half, 2 * half)):
        y = r_ref[r, :] + _dot(act_ref[r, :], w_ref[...])
        out = _layer_norm(y, g_ref[...], b_ref[...])
        o_ref[r, :] = out
        ob_ref[r, :] = out.astype(BF16)


def _ffn_out(act, w_down, resid, g, b, tm):
    m, kf = act.shape
    d = w_down.shape[1]
    return pl.pallas_call(
        _ffn_out_kernel,
        out_shape=(jax.ShapeDtypeStruct((m, d), F32), jax.ShapeDtypeStruct((m, d), BF16)),
        grid=(m // tm,),
        in_specs=[pl.BlockSpec((tm, kf), lambda i: (i, 0)),
                  pl.BlockSpec((kf, d), lambda i: (0, 0), pipeline_mode=pl.Buffered(1)),
                  pl.BlockSpec((tm, d), lambda i: (i, 0)),
                  pl.BlockSpec((1, d), lambda i: (0, 0)),
                  pl.BlockSpec((1, d), lambda i: (0, 0))],
        out_specs=(pl.BlockSpec((tm, d), lambda i: (i, 0)),
                   pl.BlockSpec((tm, d), lambda i: (i, 0))),
        compiler_params=_cparams(("parallel",)),
        name="ffn_out_ln",
    )(act, w_down, resid, g.reshape(1, d), b.reshape(1, d))


DN_TS = 256
DN_HB = 8


def _dn_kernel(alog_ref, dtb_ref,
               q_ref, k_ref, v_ref, z_ref, qp_ref, kp_ref, vp_ref, hs_ref,
               cwq_ref, cwk_ref, cwv_ref, nw_ref, o_ref, state_ref, cbuf_ref):
    hg = pl.program_id(1)
    s = pl.program_id(2)
    c = DN_CHUNK
    d = HEAD_DIM

    @pl.when(s == 0)
    def _():
        state_ref[...] = jnp.zeros_like(state_ref)

    conv_slots = []

    def conv_silu(x_ref, xp_ref, cw_ref, hb):
        cols = slice(hb * d, (hb + 1) * d)
        buf = cbuf_ref.at[len(conv_slots)]
        conv_slots.append(None)
        buf[0:SUBLANES, :] = jnp.where(s == 0, 0.0, xp_ref[:, cols])
        buf[SUBLANES:, :] = x_ref[:, cols]
        cw = cw_ref[:, cols]
        y = None
        for i in range(DN_CONV):
            off = SUBLANES - (DN_CONV - 1) + i
            term = buf[off:off + DN_TS, :] * cw[i:i + 1]
            y = term if y is None else y + term
        return _silu(y)

    def l2norm(x):
        return x * lax.rsqrt(jnp.sum(x * x, axis=-1, keepdims=True) + NORM_EPS)

    row = lax.broadcasted_iota(jnp.int32, (c, c), 0)
    col = lax.broadcasted_iota(jnp.int32, (c, c), 1)
    causal = row >= col
    strict = row > col
    tri_b = jnp.where(causal, 1.0, 0.0).astype(BF16)
    eye = jnp.where(row == col, 1.0, 0.0).astype(F32)
    ones8_b = jnp.ones((SUBLANES, c), BF16)
    hs = hs_ref[...]
    nw = nw_ref[...]

    nck = DN_TS // c
    pairs = [(hb, ci) for hb in range(DN_HB) for ci in range(nck)]
    qs, ks, vs, betas, gbs = [], [], [], [], []
    for hb in range(DN_HB):
        h = hg * DN_HB + hb
        q_all = l2norm(conv_silu(q_ref, qp_ref, cwq_ref, hb)) * (d ** -0.5)
        k_all = l2norm(conv_silu(k_ref, kp_ref, cwk_ref, hb))
        v_all = conv_silu(v_ref, vp_ref, cwv_ref, hb)
        beta_all = _sigmoid(_lane_col(hs, SC_BETA + h))
        a_all = _lane_col(hs, SC_DECAY + h) + dtb_ref[h]
        softplus = jnp.maximum(a_all, 0.0) + jnp.log(1.0 + jnp.exp(-jnp.abs(a_all)))
        g_all = -jnp.exp(jnp.zeros_like(a_all) + alog_ref[h]) * softplus
        for ci in range(nck):
            sl = slice(ci * c, (ci + 1) * c)
            qs.append(q_all[sl])
            ks.append(k_all[sl])
            vs.append(v_all[sl])
            betas.append(beta_all[sl])
            gbs.append(jnp.broadcast_to(g_all[sl], (c, LANES)))
    n = len(pairs)
    gcs = [_dot_01(tri_b, gbs[i]) for i in range(n)]
    gc_rows = [_dot_01(ones8_b, jnp.where(row <= col, gbs[i][:, :c], 0.0))[0:1] for i in range(n)]
    decays = [jnp.exp(jnp.where(causal, gcs[i][:, :c] - gc_rows[i], NEG_INF)) for i in range(n)]
    kbs = [ks[i] * betas[i] for i in range(n)]
    kbfs = [ks[i].astype(BF16) for i in range(n)]
    negs = [jnp.where(strict, -(_dot_nt(kbs[i].astype(BF16), kbfs[i]) * decays[i]), 0.0)
            for i in range(n)]
    accs = [eye + negs[i] for i in range(n)]
    nbf = [negs[i].astype(BF16) for i in range(n)]
    pws = [_dot(nbf[i], nbf[i]) for i in range(n)]
    for _ in range(4):
        pbs = [pws[i].astype(BF16) for i in range(n)]
        accs = [accs[i] + _dot(pbs[i], accs[i].astype(BF16)) for i in range(n)]
        pws = [_dot(pbs[i], pbs[i]) for i in range(n)]
    tinvs = [(accs[i] + _dot(pws[i].astype(BF16), accs[i].astype(BF16))).astype(BF16)
             for i in range(n)]
    egs = [jnp.exp(gcs[i]) for i in range(n)]
    uws = [_dot(tinvs[i], jnp.concatenate([vs[i] * betas[i], kbs[i] * egs[i]], axis=1).astype(BF16))
           .astype(BF16) for i in range(n)]
    qks = [jnp.where(causal, _dot_nt(qs[i].astype(BF16), kbfs[i]) * decays[i], 0.0).astype(BF16)
           for i in range(n)]
    k_decs = [(ks[i] * jnp.exp(gcs[i][c - 1:c] - gcs[i])).astype(BF16) for i in range(n)]
    nbs = [_dot_tn(k_decs[i], uws[i]) for i in range(n)]
    prs = [_dot(qks[i], uws[i]) for i in range(n)]
    lhss = [jnp.concatenate([(qs[i] * egs[i] - prs[i][:, d:]).astype(BF16),
                             nbs[i][:, d:].astype(BF16)], axis=0) for i in range(n)]

    states = [state_ref[hb] for hb in range(DN_HB)]
    for ci in range(nck):
        sl = slice(ci * c, (ci + 1) * c)
        ress = [_dot(lhss[hb * nck + ci], states[hb].astype(BF16)) for hb in range(DN_HB)]
        for hb in range(DN_HB):
            i = hb * nck + ci
            cols = slice(hb * d, (hb + 1) * d)
            o = ress[hb][:c] + prs[i][:, :d]
            states[hb] = states[hb] * egs[i][c - 1:c] - ress[hb][c:] + nbs[i][:, :d]
            o = o * lax.rsqrt(jnp.mean(o * o, axis=-1, keepdims=True) + NORM_EPS) * nw
            o_ref[sl, cols] = (o * _silu(z_ref[sl, cols])).astype(o_ref.dtype)
    for hb in range(DN_HB):
        state_ref[hb] = states[hb]


def _deltanet(h_main, h_small, conv_w, a_log, dt_bias, norm_w):
    b_, s_len, _ = h_main.shape
    ts = DN_TS
    wd = DN_HB * HEAD_DIM
    ngrp = HEADS // DN_HB
    blk = lambda cb: pl.BlockSpec((None, ts, wd), lambda b, h, s: (b, s, cb // DN_HB + h))
    prev = lambda cb: pl.BlockSpec(
        (None, SUBLANES, wd),
        lambda b, h, s: (b, jnp.maximum(s * (ts // SUBLANES) - 1, 0), cb // DN_HB + h))
    cw = lambda cb: pl.BlockSpec((DN_CONV, wd), lambda b, h, s: (0, cb // DN_HB + h))
    smem = pl.BlockSpec(memory_space=pltpu.SMEM)
    return pl.pallas_call(
        _dn_kernel,
        out_shape=jax.ShapeDtypeStruct((b_, s_len, HEADS * HEAD_DIM), BF16),
        grid=(b_, ngrp, s_len // ts),
        in_specs=[smem, smem,
                  blk(CB_DN_Q), blk(CB_DN_K), blk(CB_DN_V), blk(CB_DN_Z),
                  prev(CB_DN_Q), prev(CB_DN_K), prev(CB_DN_V),
                  pl.BlockSpec((None, ts, LANES), lambda b, h, s: (b, s, 0)),
                  cw(CB_DN_Q), cw(CB_DN_K), cw(CB_DN_V),
                  pl.BlockSpec((1, LANES), lambda b, h, s: (0, 0))],
        out_specs=pl.BlockSpec((None, ts, wd), lambda b, h, s: (b, s, h)),
        scratch_shapes=[pltpu.VMEM((DN_HB, HEAD_DIM, HEAD_DIM), F32),
                        pltpu.VMEM((3 * DN_HB, SUBLANES + DN_TS, HEAD_DIM), F32)],
        compiler_params=_cparams(("parallel", "parallel", "arbitrary")),
        name="deltanet",
    )(a_log, dt_bias, h_main, h_main, h_main, h_main, h_main, h_main, h_main, h_small,
      conv_w, conv_w, conv_w, norm_w.reshape(1, LANES))


def _rope(x, cosf, sinf):
    return x * cosf + pltpu.roll(x, HEAD_DIM // 2, axis=1) * sinf


def _nsa_prep_kernel(kc_ref, vc_ref, ks_ref, vs_ref, kw_ref, vw_ref, cos_ref, sin_ref,
                     pek_ref, w1k_ref, w2k_ref, pev_ref, w1v_ref, w2v_ref,
                     kso_ref, vso_ref, kwo_ref, vwo_ref, kco_ref, vco_ref, buf_ref):
    cosf = cos_ref[...]
    sinf = sin_ref[...]
    s_len = ks_ref.shape[0]
    pos = lax.broadcasted_iota(jnp.int32, (s_len, LANES), 0)
    lane = lax.broadcasted_iota(jnp.int32, (s_len, LANES), 1)
    kso_ref[:, :HEAD_DIM] = _rope(ks_ref[...], cosf, sinf).astype(BF16)
    kso_ref[:, HEAD_DIM:] = jnp.where(pos // SEL_BLOCK == lane, NEG_INF, 0.0).astype(BF16)
    kwo_ref[...] = _rope(kw_ref[...], cosf, sinf).astype(BF16)
    ones = jnp.ones((s_len, HEAD_DIM), BF16)
    vso_ref[:, :HEAD_DIM] = vs_ref[...].astype(BF16)
    vso_ref[:, HEAD_DIM:] = ones
    vwo_ref[:, :HEAD_DIM] = vw_ref[...].astype(BF16)
    vwo_ref[:, HEAD_DIM:] = ones
    nch = buf_ref.shape[0] // CMP_STRIDE

    def compress(pe_ref, w1_ref, w2_ref, out_ref):
        a0 = jnp.zeros((nch, CMP_HIDDEN), F32)
        a1 = jnp.zeros((nch, CMP_HIDDEN), F32)
        for i in range(CMP_STRIDE):
            xi = buf_ref[pl.ds(i, nch, stride=CMP_STRIDE), :]
            lo = (xi + pe_ref[i:i + 1, :]).astype(BF16)
            hi = (xi + pe_ref[CMP_STRIDE + i:CMP_STRIDE + i + 1, :]).astype(BF16)
            a0 = a0 + _dot(lo, w1_ref[i * HEAD_DIM:(i + 1) * HEAD_DIM, :])
            a1 = a1 + _dot(hi, w1_ref[(CMP_STRIDE + i) * HEAD_DIM:(CMP_STRIDE + i + 1) * HEAD_DIM, :])
        hid = a0 + pltpu.roll(a1, nch - 1, axis=0)
        out_ref[...] = _dot(_silu(hid).astype(BF16), w2_ref[...]).astype(out_ref.dtype)

    buf_ref[...] = _rope(kc_ref[...], cosf, sinf)
    compress(pek_ref, w1k_ref, w2k_ref, kco_ref)
    buf_ref[...] = vc_ref[...]
    compress(pev_ref, w1v_ref, w2v_ref, vco_ref)


def _nsa_prep(h_main, cosf, sinf, pe_k, w1_k, w2_k, pe_v, w1_v, w2_v):
    b_, s_len, _ = h_main.shape
    nch = s_len // CMP_STRIDE
    kv = lambda i: pl.BlockSpec((None, s_len, LANES), lambda b, g: (b, 0, CB_KV + 2 * i + g))
    full = lambda shape: pl.BlockSpec(shape, lambda b, g: tuple(0 for _ in shape))
    assert s_len // SEL_BLOCK <= LANES
    seq_out = pl.BlockSpec((None, None, s_len, LANES), lambda b, g: (b, g, 0, 0))
    aug_out = pl.BlockSpec((None, None, s_len, 2 * LANES), lambda b, g: (b, g, 0, 0))
    cmp_out = pl.BlockSpec((None, None, nch, LANES), lambda b, g: (b, g, 0, 0))
    seq_shape = jax.ShapeDtypeStruct((b_, GROUPS, s_len, HEAD_DIM), BF16)
    aug_shape = jax.ShapeDtypeStruct((b_, GROUPS, s_len, 2 * HEAD_DIM), BF16)
    cmp_shape = jax.ShapeDtypeStruct((b_, GROUPS, nch, HEAD_DIM), BF16)
    return pl.pallas_call(
        _nsa_prep_kernel,
        out_shape=(aug_shape, aug_shape, seq_shape, aug_shape, cmp_shape, cmp_shape),
        grid=(b_, GROUPS),
        in_specs=[kv(0), kv(1), kv(2), kv(3), kv(4), kv(5),
                  full((s_len, LANES)), full((s_len, LANES)),
                  full((CMP_LEN, HEAD_DIM)), full((CMP_LEN * HEAD_DIM, CMP_HIDDEN)),
                  full((CMP_HIDDEN, HEAD_DIM)),
                  full((CMP_LEN, HEAD_DIM)), full((CMP_LEN * HEAD_DIM, CMP_HIDDEN)),
                  full((CMP_HIDDEN, HEAD_DIM))],
        out_specs=(aug_out, aug_out, seq_out, aug_out, cmp_out, cmp_out),
        scratch_shapes=[pltpu.VMEM((s_len, HEAD_DIM), F32)],
        compiler_params=_cparams(("parallel", "parallel")),
        name="nsa_prep",
    )(h_main, h_main, h_main, h_main, h_main, h_main, cosf, sinf,
      pe_k, w1_k, w2_k, pe_v, w1_v, w2_v)


NSA_TQ = 256
NSA_TK = 256


def _nsa_attn_kernel(q_ref, hs_ref, cos_ref, sin_ref, kc_ref, vc_ref, ks_ref, vs_ref,
                     kw_ref, vw_ref, ovt_ref, o_ref, os_ref):
    g = pl.program_id(1)
    qi = pl.program_id(2)
    tq, tk = NSA_TQ, NSA_TK
    rows = HPG * tq
    ns = ks_ref.shape[0] // SEL_BLOCK
    cosf = cos_ref[...]
    sinf = sin_ref[...]
    scale = HEAD_DIM ** -0.5
    qs = jnp.concatenate(
        [_rope(q_ref[:, hh * HEAD_DIM:(hh + 1) * HEAD_DIM], cosf, sinf) * scale
         for hh in range(HPG)], axis=0).astype(BF16)

    t_abs = qi * tq + lax.broadcasted_iota(jnp.int32, (tq, LANES), 0)
    lane = lax.broadcasted_iota(jnp.int32, (tq, LANES), 1)

    t_abs_k = qi * tq + lax.broadcasted_iota(jnp.int32, (tq, tk), 0)
    lane_k = lax.broadcasted_iota(jnp.int32, (tq, tk), 1)

    def add_bias(s_blk, bias):
        return (s_blk.reshape(HPG, tq, tk) + bias[None]).reshape(rows, tk)

    def split_rows(dot_fn, lhs, rhs):
        half = lhs.shape[0] // 2
        return jnp.concatenate([dot_fn(lhs[:half], rhs), dot_fn(lhs[half:], rhs)], axis=0)

    def fold(x):
        return x[:, :LANES], x[:, LANES:]

    neg_rows = jnp.full((rows, LANES), NEG_INF, F32)

    def running_max(score_blocks):
        mrun = neg_rows
        for s_blk in score_blocks:
            s0, s1 = fold(s_blk)
            mrun = jnp.maximum(mrun, jnp.maximum(s0, s1))
        return jnp.broadcast_to(jnp.max(mrun, axis=-1, keepdims=True), (rows, LANES))

    def probs(s_blk, m_b):
        s0, s1 = fold(s_blk)
        return jnp.exp(jnp.concatenate([(s0 - m_b).astype(BF16), (s1 - m_b).astype(BF16)], axis=1))

    def weighted_values(ps, value_rows):
        acc = _dot(jnp.concatenate(ps, axis=1), value_rows)
        return acc[:, :HEAD_DIM] / acc[:, HEAD_DIM:]

    kb_last = (qi * tq) // tk
    nwin = WIN // tk + 1
    win_start = pl.multiple_of(jnp.maximum(kb_last - (nwin - 1), 0) * tk, tk)
    win_raw = split_rows(_dot_nt, qs, kw_ref[pl.ds(win_start, nwin * tk), :])

    mask_c = jnp.concatenate([lane * CMP_STRIDE + (CMP_LEN - 1) <= t_abs] * HPG, axis=0)
    s_c = split_rows(_dot_nt, qs, kc_ref[...])
    m_c = jnp.max(jnp.where(mask_c, s_c, NEG_INF), axis=-1, keepdims=True)
    e_c = jnp.where(mask_c, jnp.exp(s_c - m_c), 0.0)
    l_c = jnp.sum(e_c, axis=-1, keepdims=True)
    p_c = jnp.where(l_c > 0.0, e_c / l_c, 0.0)

    win_s = []
    for d in range(nwin):
        diff = t_abs_k - (win_start + d * tk + lane_k)
        bias = jnp.where((diff >= 0) & (diff < WIN), 0.0, NEG_INF)
        win_s.append(add_bias(win_raw[:, d * tk:(d + 1) * tk], bias))
    win_m = running_max(win_s)

    o_c = split_rows(_dot, p_c.astype(BF16), vc_ref[...])
    psum = p_c[0:tq]
    for hh in range(1, HPG):
        psum = psum + p_c[hh * tq:(hh + 1) * tq]
    p_hi, p_lo = _split2(psum)
    ovt = ovt_ref[...]
    imp = (_dot_nt(ovt, p_hi) + _dot_nt(ovt, p_lo))[:ns]
    blk = lax.broadcasted_iota(jnp.int32, (ns, tq), 0)
    cur = (qi * tq + lax.broadcasted_iota(jnp.int32, (ns, tq), 1)) // SEL_BLOCK
    forced = (blk == 0) | (blk == cur) | (blk == cur - 1)
    imp = jnp.where(forced, jnp.inf, jnp.where(blk <= cur, imp, -jnp.inf))

    win_p = [probs(s_blk, win_m) for s_blk in win_s]

    rank = jnp.zeros((ns, tq), F32)
    for i in range(ns):
        ci = imp[i:i + 1, :]
        before = (ci > imp) | ((ci == imp) & (blk > i))
        rank = rank + jnp.where(before, 1.0, 0.0)
    unsel_t = jnp.where(rank < float(min(SEL_TOPK, ns)), 0.0, 1.0)
    unsel = jnp.concatenate([unsel_t, jnp.zeros((LANES - ns, tq), F32)], axis=0).T
    q_aug = jnp.concatenate([qs, jnp.concatenate([unsel.astype(BF16)] * HPG, axis=0)], axis=1)

    o_w = weighted_values(win_p, vw_ref[pl.ds(win_start, nwin * tk), :])

    def sel_variant(n_full):
        def run():
            n_keys = (n_full + 1) * tk
            s_all = split_rows(_dot_nt, q_aug, ks_ref[0:n_keys, :])
            s_blks = [s_all[:, j * tk:(j + 1) * tk] for j in range(n_full + 1)]
            bias = jnp.where(n_full * tk + lane_k <= t_abs_k, 0.0, NEG_INF)
            s_blks[n_full] = add_bias(s_blks[n_full], bias)
            m_b = running_max(s_blks)
            ps = [probs(s_blk, m_b) for s_blk in s_blks]
            os_ref[...] = weighted_values(ps, vs_ref[0:n_keys, :])
        return run

    for n_full in range(ks_ref.shape[0] // tk):
        pl.when(kb_last == n_full)(sel_variant(n_full))
    o_s = os_ref[...]

    hs = hs_ref[...]
    for hh in range(HPG):
        gbase = SC_GATE + (g * HPG + hh) * 3
        r = slice(hh * tq, (hh + 1) * tq)
        out = (_sigmoid(_lane_col(hs, gbase)) * o_c[r]
               + _sigmoid(_lane_col(hs, gbase + 1)) * o_s[r]
               + _sigmoid(_lane_col(hs, gbase + 2)) * o_w[r])
        o_ref[:, hh * HEAD_DIM:(hh + 1) * HEAD_DIM] = out.astype(o_ref.dtype)


def _nsa_attn(h_main, h_small, cosf, sinf, kc, vc, ks, vs, kw, vw, overlap_t):
    b_, s_len, _ = h_main.shape
    tq = NSA_TQ
    nch = kc.shape[2]
    assert nch == LANES, "the compressed-block axis is laid out on one vreg of lanes"
    assert s_len % NSA_TK == 0 and NSA_TK == 2 * LANES
    qw = HPG * HEAD_DIM
    seq = pl.BlockSpec((None, None, s_len, HEAD_DIM), lambda b, g, i: (b, g, 0, 0))
    aug = pl.BlockSpec((None, None, s_len, 2 * HEAD_DIM), lambda b, g, i: (b, g, 0, 0))
    cmp_ = pl.BlockSpec((None, None, nch, HEAD_DIM), lambda b, g, i: (b, g, 0, 0))
    return pl.pallas_call(
        _nsa_attn_kernel,
        out_shape=jax.ShapeDtypeStruct((b_, s_len, HEADS * HEAD_DIM), BF16),
        grid=(b_, GROUPS, s_len // tq),
        in_specs=[pl.BlockSpec((None, tq, qw), lambda b, g, i: (b, i, CB_NSA_Q * LANES // qw + g)),
                  pl.BlockSpec((None, tq, LANES), lambda b, g, i: (b, i, 0)),
                  pl.BlockSpec((tq, LANES), lambda b, g, i: (i, 0)),
                  pl.BlockSpec((tq, LANES), lambda b, g, i: (i, 0)),
                  cmp_, cmp_, aug, aug, seq, aug,
                  pl.BlockSpec(overlap_t.shape, lambda b, g, i: (0, 0))],
        out_specs=pl.BlockSpec((None, tq, qw), lambda b, g, i: (b, i, g)),
        scratch_shapes=[pltpu.VMEM((HPG * tq, HEAD_DIM), F32)],
        compiler_params=_cparams(("parallel", "parallel", "arbitrary")),
        name="nsa_attn",
    )(h_main, h_small, cosf, sinf, kc, vc, ks, vs, kw, vw, overlap_t)


def _nsa_constants(s_len):
    half = HEAD_DIM // 2
    inv_freq = ROPE_THETA ** (-jnp.arange(half, dtype=F32) / half)
    ang = jnp.arange(s_len, dtype=F32)[:, None] * inv_freq[None, :]
    cos, sin = jnp.cos(ang), jnp.sin(ang)
    cosf = jnp.concatenate([cos, cos], axis=-1)
    sinf = jnp.concatenate([-sin, sin], axis=-1)
    nch = s_len // CMP_STRIDE
    ns = s_len // SEL_BLOCK
    n = jnp.arange(nch)[:, None] * CMP_STRIDE
    j = jnp.arange(LANES)[None, :] * SEL_BLOCK
    overlap = ((n <= j + SEL_BLOCK - 1) & (n + CMP_LEN - 1 >= j)
               & (jnp.arange(nch)[:, None] < nch - CMP_LEN // CMP_STRIDE + 1)
               & (jnp.arange(LANES)[None, :] < ns)).astype(BF16)
    return cosf, sinf, overlap.T


def _layer(x, xb, p_i, w_in, conv_w, a_log, dt_bias, norm_w, pe_k, w1_k, w2_k, pe_v, w1_v, w2_v,
           w_a, w_b, w_out, ln1_g, ln1_b, w_gate, w_up, w_down, w_ple, w_ple_gate, ln2_g, ln2_b,
           consts):
    b_, s_len, d = x.shape
    t = b_ * s_len
    cosf, sinf, overlap_t = consts
    x2 = x.reshape(t, d)
    xb2 = xb.reshape(t, d)

    w_main = jnp.concatenate([w_in[:, :4096], w_in[:, 6696:], w_in[:, 4112:6672]], axis=1).astype(BF16)
    w_small = jnp.concatenate([w_in[:, 4096:4112], w_in[:, 6672:6696],
                               jnp.zeros((d, LANES - 40), w_in.dtype)], axis=1).astype(BF16)
    tm = min(1024, t)
    h_main = _matmul(xb2, w_main, F32, tm, 1536, "proj_main")
    h_small = _matmul(xb2, w_small, F32, tm, LANES, "proj_small")
    h_main3 = h_main.reshape(b_, s_len, N_MAIN)
    h_small3 = h_small.reshape(b_, s_len, LANES)

    o_a = _deltanet(h_main3, h_small3, conv_w, a_log, dt_bias, norm_w)
    ks, vs, kw, vw, kc, vc = _nsa_prep(h_main3, cosf, sinf, pe_k, w1_k.astype(BF16),
                                       w2_k.astype(BF16), pe_v, w1_v.astype(BF16),
                                       w2_v.astype(BF16))
    o_b = _nsa_attn(h_main3, h_small3, cosf, sinf, kc, vc, ks, vs, kw, vw, overlap_t)

    mixed = _merge(o_a.reshape(t, -1), o_b.reshape(t, -1), w_a.astype(BF16), w_b.astype(BF16),
                   h_main, min(512, t))
    x1, x1b = _outproj_ln(mixed, w_out.astype(BF16), x2, ln1_g, ln1_b, min(512, t))
    act = _ffn_act(x1b, w_gate.astype(BF16), w_up.astype(BF16), tm, 512)
    resid = _resid(x1, x1b, p_i.reshape(t, PLE_DIM).astype(BF16), w_ple.astype(BF16),
                   w_ple_gate.astype(BF16), tm, 1024)
    y, yb = _ffn_out(act, w_down.astype(BF16), resid, ln2_g, ln2_b, min(256, t))
    return y.reshape(b_, s_len, d), yb.reshape(b_, s_len, d)


def kernel(x, p, w_in, dn_conv_w, dn_a_log, dn_dt_bias, dn_norm_w, cmp_pe_k, cmp_w1_k, cmp_w2_k, cmp_pe_v, cmp_w1_v, cmp_w2_v, w_branch_a, w_branch_b, w_out, ln1_g, ln1_b, w_ffn_gate, w_ffn_up, w_ffn_down, w_ple, w_ple_gate, ln2_g, ln2_b):
    consts = _nsa_constants(x.shape[1])
    xb = x.astype(BF16)
    for i in range(DEPTH):
        x, xb = _layer(x, xb, p[i], w_in[i], dn_conv_w[i], dn_a_log[i], dn_dt_bias[i], dn_norm_w[i],
                       cmp_pe_k[i], cmp_w1_k[i], cmp_w2_k[i], cmp_pe_v[i], cmp_w1_v[i], cmp_w2_v[i],
                       w_branch_a[i], w_branch_b[i], w_out[i], ln1_g[i], ln1_b[i],
                       w_ffn_gate[i], w_ffn_up[i], w_ffn_down[i], w_ple[i], w_ple_gate[i],
                       ln2_g[i], ln2_b[i], consts)
    return x
```

```python
import jax
import jax.numpy as jnp
from jax import lax
from jax.experimental import pallas as pl
from jax.experimental.pallas import tpu as pltpu

D_MODEL = 2048
DEPTH = 2
HEAD_DIM = 128
HEADS = 8
DN_CONV = 4
DN_CHUNK = 64
GROUPS = 2
HPG = HEADS // GROUPS
CMP_LEN = 32
CMP_STRIDE = 16
CMP_HIDDEN = 256
SEL_BLOCK = 64
SEL_TOPK = 16
WIN = 512
ROPE_THETA = 10000.0
D_FF = 5632
PLE_DIM = 256
ALPHA = (2.0 * DEPTH) ** 0.25
LN_EPS = 1e-5
NORM_EPS = 1e-6
NEG_INF = -1e30

LANES = 128
SUBLANES = 8
VMEM_LIMIT = 56 * 1024 * 1024

CB_DN_Q, CB_DN_K, CB_DN_V, CB_DN_Z = 0, 8, 16, 24
CB_NSA_Q = 64
CB_KV = 72
N_MAIN = 10752
COL_MERGE_A = 4096
COL_MERGE_B = 6144
SC_BETA, SC_DECAY, SC_GATE = 0, 8, 16

F32 = jnp.float32
BF16 = jnp.bfloat16


def _cparams(sem):
    return pltpu.CompilerParams(dimension_semantics=sem, vmem_limit_bytes=VMEM_LIMIT)


def _dot(a, b):
    return jnp.dot(a, b, preferred_element_type=F32)


def _dot_nt(a, b):
    return lax.dot_general(a, b, (((1,), (1,)), ((), ())), preferred_element_type=F32)


def _dot_tn(a, b):
    return lax.dot_general(a, b, (((0,), (0,)), ((), ())), preferred_element_type=F32)


def _sigmoid(x):
    return 0.5 * jnp.tanh(0.5 * x) + 0.5


def _silu(x):
    return x * _sigmoid(x)


def _layer_norm(y, g, b):
    mu = jnp.mean(y, axis=-1, keepdims=True)
    d = y - mu
    var = jnp.mean(d * d, axis=-1, keepdims=True)
    return d * lax.rsqrt(var + LN_EPS) * g + b


def _lane_col(x, idx):
    lane = lax.broadcasted_iota(jnp.int32, x.shape, 1)
    return jnp.sum(jnp.where(lane == idx, x, 0.0), axis=1, keepdims=True)


def _split2(x):
    hi = x.astype(BF16)
    return hi, (x - hi.astype(F32)).astype(BF16)


def _split3(x):
    x1 = x.astype(BF16)
    r = x - x1.astype(F32)
    x2 = r.astype(BF16)
    return x1, x2, (r - x2.astype(F32)).astype(BF16)


def _dot_01(ones_b, x):
    x1, x2, x3 = _split3(x)
    return _dot(ones_b, x1) + _dot(ones_b, x2) + _dot(ones_b, x3)


def _mm_kernel(a_ref, w_ref, o_ref):
    o_ref[...] = _dot(a_ref[...], w_ref[...]).astype(o_ref.dtype)


def _matmul(a, w, out_dtype, tm, tn, name):
    m, k = a.shape
    n = w.shape[1]
    return pl.pallas_call(
        _mm_kernel,
        out_shape=jax.ShapeDtypeStruct((m, n), out_dtype),
        grid=(m // tm, n // tn),
        in_specs=[pl.BlockSpec((tm, k), lambda i, j: (i, 0)),
                  pl.BlockSpec((k, tn), lambda i, j: (0, j))],
        out_specs=pl.BlockSpec((tm, tn), lambda i, j: (i, j)),
        compiler_params=_cparams(("parallel", "arbitrary")),
        name=name,
    )(a, w)


MERGE_CHUNK = 512


def _merge_kernel(oa_ref, ob_ref, wa_ref, wb_ref, ma_ref, mb_ref, o_ref):
    oa = oa_ref[...]
    ob = ob_ref[...]
    for c in range(o_ref.shape[1] // MERGE_CHUNK):
        cols = slice(c * MERGE_CHUNK, (c + 1) * MERGE_CHUNK)
        ya = _dot(oa, wa_ref[:, cols])
        yb = _dot(ob, wb_ref[:, cols])
        o_ref[:, cols] = (_sigmoid(ma_ref[:, cols]) * ya
                          + _sigmoid(mb_ref[:, cols]) * yb).astype(o_ref.dtype)


def _merge(o_a, o_b, w_a, w_b, h_main, tm):
    m, k = o_a.shape
    n = w_a.shape[1]
    ca, cb = COL_MERGE_A // n, COL_MERGE_B // n
    return pl.pallas_call(
        _merge_kernel,
        out_shape=jax.ShapeDtypeStruct((m, n), BF16),
        grid=(m // tm,),
        in_specs=[pl.BlockSpec((tm, k), lambda i: (i, 0)),
                  pl.BlockSpec((tm, k), lambda i: (i, 0)),
                  pl.BlockSpec((k, n), lambda i: (0, 0), pipeline_mode=pl.Buffered(1)),
                  pl.BlockSpec((k, n), lambda i: (0, 0), pipeline_mode=pl.Buffered(1)),
                  pl.BlockSpec((tm, n), lambda i: (i, ca)),
                  pl.BlockSpec((tm, n), lambda i: (i, cb))],
        out_specs=pl.BlockSpec((tm, n), lambda i: (i, 0)),
        compiler_params=_cparams(("parallel",)),
        name="merge",
    )(o_a, o_b, w_a, w_b, h_main, h_main)


def _outproj_ln_kernel(mx_ref, w_ref, x_ref, g_ref, b_ref, o_ref, ob_ref):
    half = mx_ref.shape[0] // 2
    for r in (slice(0, half), slice(half, 2 * half)):
        y = ALPHA * x_ref[r, :] + _dot(mx_ref[r, :], w_ref[...])
        out = _layer_norm(y, g_ref[...], b_ref[...])
        o_ref[r, :] = out
        ob_ref[r, :] = out.astype(BF16)


def _outproj_ln(mixed, w_out, x, g, b, tm):
    m, d = x.shape
    return pl.pallas_call(
        _outproj_ln_kernel,
        out_shape=(jax.ShapeDtypeStruct((m, d), F32), jax.ShapeDtypeStruct((m, d), BF16)),
        grid=(m // tm,),
        in_specs=[pl.BlockSpec((tm, d), lambda i: (i, 0)),
                  pl.BlockSpec((d, d), lambda i: (0, 0), pipeline_mode=pl.Buffered(1)),
                  pl.BlockSpec((tm, d), lambda i: (i, 0)),
                  pl.BlockSpec((1, d), lambda i: (0, 0)),
                  pl.BlockSpec((1, d), lambda i: (0, 0))],
        out_specs=(pl.BlockSpec((tm, d), lambda i: (i, 0)),
                   pl.BlockSpec((tm, d), lambda i: (i, 0))),
        compiler_params=_cparams(("parallel",)),
        name="outproj_ln",
    )(mixed, w_out, x, g.reshape(1, d), b.reshape(1, d))


def _ffn_act_kernel(x_ref, wg_ref, wu_ref, o_ref):
    xv = x_ref[...]
    gate = _dot(xv, wg_ref[...].astype(BF16))
    up = _dot(xv, wu_ref[...].astype(BF16))
    o_ref[...] = (_silu(gate) * up).astype(o_ref.dtype)


def _ffn_act(xb, w_gate, w_up, tm, tn):
    m, k = xb.shape
    n = w_gate.shape[1]
    return pl.pallas_call(
        _ffn_act_kernel,
        out_shape=jax.ShapeDtypeStruct((m, n), BF16),
        grid=(m // tm, n // tn),
        in_specs=[pl.BlockSpec((tm, k), lambda i, j: (i, 0)),
                  pl.BlockSpec((k, tn), lambda i, j: (0, j)),
                  pl.BlockSpec((k, tn), lambda i, j: (0, j))],
        out_specs=pl.BlockSpec((tm, tn), lambda i, j: (i, j)),
        compiler_params=_cparams(("parallel", "arbitrary")),
        name="ffn_act",
    )(xb, w_gate, w_up)


def _resid_kernel(x_ref, xb_ref, p_ref, wp_ref, wpg_ref, o_ref):
    xb = xb_ref[...]
    pv = p_ref[...]
    for c in range(o_ref.shape[1] // MERGE_CHUNK):
        cols = slice(c * MERGE_CHUNK, (c + 1) * MERGE_CHUNK)
        ple = _dot(pv, wp_ref[:, cols]) * _sigmoid(_dot(xb, wpg_ref[:, cols]))
        o_ref[:, cols] = ALPHA * x_ref[:, cols] + ple


def _resid(x1, x1b, pb, w_ple, w_ple_gate, tm):
    m, d = x1.shape
    kp = pb.shape[1]
    return pl.pallas_call(
        _resid_kernel,
        out_shape=jax.ShapeDtypeStruct((m, d), F32),
        grid=(m // tm,),
        in_specs=[pl.BlockSpec((tm, d), lambda i: (i, 0)),
                  pl.BlockSpec((tm, d), lambda i: (i, 0)),
                  pl.BlockSpec((tm, kp), lambda i: (i, 0)),
                  pl.BlockSpec((kp, d), lambda i: (0, 0), pipeline_mode=pl.Buffered(1)),
                  pl.BlockSpec((d, d), lambda i: (0, 0), pipeline_mode=pl.Buffered(1))],
        out_specs=pl.BlockSpec((tm, d), lambda i: (i, 0)),
        compiler_params=_cparams(("parallel",)),
        name="ple_resid",
    )(x1, x1b, pb, w_ple, w_ple_gate)


def _ffn_out_kernel(act_ref, w_ref, r_ref, g_ref, b_ref, o_ref, ob_ref):
    half = act_ref.shape[0] // 2
    for r in (slice(0, half), slice(half, 2 * half)):
        y = r_ref[r, :] + _dot(act_ref[r, :], w_ref[...])
        out = _layer_norm(y, g_ref[...], b_ref[...])
        o_ref[r, :] = out
        ob_ref[r, :] = out.astype(BF16)


def _ffn_out(act, w_down, resid, g, b, tm):
    m, kf = act.shape
    d = w_down.shape[1]
    return pl.pallas_call(
        _ffn_out_kernel,
        out_shape=(jax.ShapeDtypeStruct((m, d), F32), jax.ShapeDtypeStruct((m, d), BF16)),
        grid=(m // tm,),
        in_specs=[pl.BlockSpec((tm, kf), lambda i: (i, 0)),
                  pl.BlockSpec((kf, d), lambda i: (0, 0), pipeline_mode=pl.Buffered(1)),
                  pl.BlockSpec((tm, d), lambda i: (i, 0)),
                  pl.BlockSpec((1, d), lambda i: (0, 0)),
                  pl.BlockSpec((1, d), lambda i: (0, 0))],
        out_specs=(pl.BlockSpec((tm, d), lambda i: (i, 0)),
                   pl.BlockSpec((tm, d), lambda i: (i, 0))),
        compiler_params=_cparams(("parallel",)),
        name="ffn_out_ln",
    )(act, w_down, resid, g.reshape(1, d), b.reshape(1, d))


DN_TS = 256
DN_HB = 8


def _dn_kernel(alog_ref, dtb_ref,
               q_ref, k_ref, v_ref, z_ref, qp_ref, kp_ref, vp_ref, hs_ref,
               cwq_ref, cwk_ref, cwv_ref, nw_ref, o_ref, state_ref, cbuf_ref):
    hg = pl.program_id(1)
    s = pl.program_id(2)
    c = DN_CHUNK
    d = HEAD_DIM

    @pl.when(s == 0)
    def _():
        state_ref[...] = jnp.zeros_like(state_ref)

    conv_slots = []

    def conv_silu(x_ref, xp_ref, cw_ref, hb):
        cols = slice(hb * d, (hb + 1) * d)
        buf = cbuf_ref.at[len(conv_slots)]
        conv_slots.append(None)
        buf[0:SUBLANES, :] = jnp.where(s == 0, 0.0, xp_ref[:, cols])
        buf[SUBLANES:, :] = x_ref[:, cols]
        cw = cw_ref[:, cols]
        y = None
        for i in range(DN_CONV):
            off = SUBLANES - (DN_CONV - 1) + i
            term = buf[off:off + DN_TS, :] * cw[i:i + 1]
            y = term if y is None else y + term
        return _silu(y)

    def l2norm(x):
        return x * lax.rsqrt(jnp.sum(x * x, axis=-1, keepdims=True) + NORM_EPS)

    row = lax.broadcasted_iota(jnp.int32, (c, c), 0)
    col = lax.broadcasted_iota(jnp.int32, (c, c), 1)
    causal = row >= col
    strict = row > col
    tri_b = jnp.where(causal, 1.0, 0.0).astype(BF16)
    eye = jnp.where(row == col, 1.0, 0.0).astype(F32)
    ones8_b = jnp.ones((SUBLANES, c), BF16)
    hs = hs_ref[...]
    nw = nw_ref[...]

    nck = DN_TS // c
    pairs = [(hb, ci) for hb in range(DN_HB) for ci in range(nck)]
    qs, ks, vs, betas, gbs = [], [], [], [], []
    for hb in range(DN_HB):
        h = hg * DN_HB + hb
        q_all = l2norm(conv_silu(q_ref, qp_ref, cwq_ref, hb)) * (d ** -0.5)
        k_all = l2norm(conv_silu(k_ref, kp_ref, cwk_ref, hb))
        v_all = conv_silu(v_ref, vp_ref, cwv_ref, hb)
        beta_all = _sigmoid(_lane_col(hs, SC_BETA + h))
        a_all = _lane_col(hs, SC_DECAY + h) + dtb_ref[h]
        softplus = jnp.maximum(a_all, 0.0) + jnp.log(1.0 + jnp.exp(-jnp.abs(a_all)))
        g_all = -jnp.exp(jnp.zeros_like(a_all) + alog_ref[h]) * softplus
        for ci in range(nck):
            sl = slice(ci * c, (ci + 1) * c)
            qs.append(q_all[sl])
            ks.append(k_all[sl])
            vs.append(v_all[sl])
            betas.append(beta_all[sl])
            gbs.append(jnp.broadcast_to(g_all[sl], (c, LANES)))
    n = len(pairs)
    gcs = [_dot_01(tri_b, gbs[i]) for i in range(n)]
    gc_rows = [_dot_01(ones8_b, jnp.where(row <= col, gbs[i][:, :c], 0.0))[0:1] for i in range(n)]
    decays = [jnp.exp(jnp.where(causal, gcs[i][:, :c] - gc_rows[i], NEG_INF)) for i in range(n)]
    kbs = [ks[i] * betas[i] for i in range(n)]
    kbfs = [ks[i].astype(BF16) for i in range(n)]
    negs = [jnp.where(strict, -(_dot_nt(kbs[i].astype(BF16), kbfs[i]) * decays[i]), 0.0)
            for i in range(n)]
    accs = [eye + negs[i] for i in range(n)]
    nbf = [negs[i].astype(BF16) for i in range(n)]
    pws = [_dot(nbf[i], nbf[i]) for i in range(n)]
    for _ in range(4):
        pbs = [pws[i].astype(BF16) for i in range(n)]
        accs = [accs[i] + _dot(pbs[i], accs[i].astype(BF16)) for i in range(n)]
        pws = [_dot(pbs[i], pbs[i]) for i in range(n)]
    tinvs = [(accs[i] + _dot(pws[i].astype(BF16), accs[i].astype(BF16))).astype(BF16)
             for i in range(n)]
    egs = [jnp.exp(gcs[i]) for i in range(n)]
    uws = [_dot(tinvs[i], jnp.concatenate([vs[i] * betas[i], kbs[i] * egs[i]], axis=1).astype(BF16))
           .astype(BF16) for i in range(n)]
    qks = [jnp.where(causal, _dot_nt(qs[i].astype(BF16), kbfs[i]) * decays[i], 0.0).astype(BF16)
           for i in range(n)]
    k_decs = [(ks[i] * jnp.exp(gcs[i][c - 1:c] - gcs[i])).astype(BF16) for i in range(n)]
    nbs = [_dot_tn(k_decs[i], uws[i]) for i in range(n)]
    prs = [_dot(qks[i], uws[i]) for i in range(n)]
    lhss = [jnp.concatenate([(qs[i] * egs[i] - prs[i][:, d:]).astype(BF16),
                             nbs[i][:, d:].astype(BF16)], axis=0) for i in range(n)]

    states = [state_ref[hb] for hb in range(DN_HB)]
    for ci in range(nck):
        sl = slice(ci * c, (ci + 1) * c)
        ress = [_dot(lhss[hb * nck + ci], states[hb].astype(BF16)) for hb in range(DN_HB)]
        for hb in range(DN_HB):
            i = hb * nck + ci
            cols = slice(hb * d, (hb + 1) * d)
            o = ress[hb][:c] + prs[i][:, :d]
            states[hb] = states[hb] * egs[i][c - 1:c] - ress[hb][c:] + nbs[i][:, :d]
            o = o * lax.rsqrt(jnp.mean(o * o, axis=-1, keepdims=True) + NORM_EPS) * nw
            o_ref[sl, cols] = (o * _silu(z_ref[sl, cols])).astype(o_ref.dtype)
    for hb in range(DN_HB):
        state_ref[hb] = states[hb]


def _deltanet(h_main, h_small, conv_w, a_log, dt_bias, norm_w):
    b_, s_len, _ = h_main.shape
    ts = DN_TS
    wd = DN_HB * HEAD_DIM
    ngrp = HEADS // DN_HB
    blk = lambda cb: pl.BlockSpec((None, ts, wd), lambda b, h, s: (b, s, cb // DN_HB + h))
    prev = lambda cb: pl.BlockSpec(
        (None, SUBLANES, wd),
        lambda b, h, s: (b, jnp.maximum(s * (ts // SUBLANES) - 1, 0), cb // DN_HB + h))
    cw = lambda cb: pl.BlockSpec((DN_CONV, wd), lambda b, h, s: (0, cb // DN_HB + h))
    smem = pl.BlockSpec(memory_space=pltpu.SMEM)
    return pl.pallas_call(
        _dn_kernel,
        out_shape=jax.ShapeDtypeStruct((b_, s_len, HEADS * HEAD_DIM), BF16),
        grid=(b_, ngrp, s_len // ts),
        in_specs=[smem, smem,
                  blk(CB_DN_Q), blk(CB_DN_K), blk(CB_DN_V), blk(CB_DN_Z),
                  prev(CB_DN_Q), prev(CB_DN_K), prev(CB_DN_V),
                  pl.BlockSpec((None, ts, LANES), lambda b, h, s: (b, s, 0)),
                  cw(CB_DN_Q), cw(CB_DN_K), cw(CB_DN_V),
                  pl.BlockSpec((1, LANES), lambda b, h, s: (0, 0))],
        out_specs=pl.BlockSpec((None, ts, wd), lambda b, h, s: (b, s, h)),
        scratch_shapes=[pltpu.VMEM((DN_HB, HEAD_DIM, HEAD_DIM), F32),
                        pltpu.VMEM((3 * DN_HB, SUBLANES + DN_TS, HEAD_DIM), F32)],
        compiler_params=_cparams(("parallel", "parallel", "arbitrary")),
        name="deltanet",
    )(a_log, dt_bias, h_main, h_main, h_main, h_main, h_main, h_main, h_main, h_small,
      conv_w, conv_w, conv_w, norm_w.reshape(1, LANES))


def _rope(x, cosf, sinf):
    return x * cosf + pltpu.roll(x, HEAD_DIM // 2, axis=1) * sinf


def _nsa_prep_kernel(kc_ref, vc_ref, ks_ref, vs_ref, kw_ref, vw_ref, cos_ref, sin_ref,
                     pek_ref, w1k_ref, w2k_ref, pev_ref, w1v_ref, w2v_ref,
                     kso_ref, vso_ref, kwo_ref, vwo_ref, kco_ref, vco_ref, buf_ref):
    cosf = cos_ref[...]
    sinf = sin_ref[...]
    s_len = ks_ref.shape[0]
    pos = lax.broadcasted_iota(jnp.int32, (s_len, LANES), 0)
    lane = lax.broadcasted_iota(jnp.int32, (s_len, LANES), 1)
    kso_ref[:, :HEAD_DIM] = _rope(ks_ref[...], cosf, sinf).astype(BF16)
    kso_ref[:, HEAD_DIM:] = jnp.where(pos // SEL_BLOCK == lane, NEG_INF, 0.0).astype(BF16)
    kwo_ref[...] = _rope(kw_ref[...], cosf, sinf).astype(BF16)
    ones = jnp.ones((s_len, HEAD_DIM), BF16)
    vso_ref[:, :HEAD_DIM] = vs_ref[...].astype(BF16)
    vso_ref[:, HEAD_DIM:] = ones
    vwo_ref[:, :HEAD_DIM] = vw_ref[...].astype(BF16)
    vwo_ref[:, HEAD_DIM:] = ones
    nch = buf_ref.shape[0] // CMP_STRIDE

    def compress(pe_ref, w1_ref, w2_ref, out_ref):
        a0 = jnp.zeros((nch, CMP_HIDDEN), F32)
        a1 = jnp.zeros((nch, CMP_HIDDEN), F32)
        for i in range(CMP_STRIDE):
            xi = buf_ref[pl.ds(i, nch, stride=CMP_STRIDE), :]
            lo = (xi + pe_ref[i:i + 1, :]).astype(BF16)
            hi = (xi + pe_ref[CMP_STRIDE + i:CMP_STRIDE + i + 1, :]).astype(BF16)
            a0 = a0 + _dot(lo, w1_ref[i * HEAD_DIM:(i + 1) * HEAD_DIM, :])
            a1 = a1 + _dot(hi, w1_ref[(CMP_STRIDE + i) * HEAD_DIM:(CMP_STRIDE + i + 1) * HEAD_DIM, :])
        hid = a0 + pltpu.roll(a1, nch - 1, axis=0)
        out_ref[...] = _dot(_silu(hid).astype(BF16), w2_ref[...]).astype(out_ref.dtype)

    buf_ref[...] = _rope(kc_ref[...], cosf, sinf)
    compress(pek_ref, w1k_ref, w2k_ref, kco_ref)
    buf_ref[...] = vc_ref[...]
    compress(pev_ref, w1v_ref, w2v_ref, vco_ref)


def _nsa_prep(h_main, cosf, sinf, pe_k, w1_k, w2_k, pe_v, w1_v, w2_v):
    b_, s_len, _ = h_main.shape
    nch = s_len // CMP_STRIDE
    kv = lambda i: pl.BlockSpec((None, s_len, LANES), lambda b, g: (b, 0, CB_KV + 2 * i + g))
    full = lambda shape: pl.BlockSpec(shape, lambda b, g: tuple(0 for _ in shape))
    assert s_len // SEL_BLOCK <= LANES
    seq_out = pl.BlockSpec((None, None, s_len, LANES), lambda b, g: (b, g, 0, 0))
    aug_out = pl.BlockSpec((None, None, s_len, 2 * LANES), lambda b, g: (b, g, 0, 0))
    cmp_out = pl.BlockSpec((None, None, nch, LANES), lambda b, g: (b, g, 0, 0))
    seq_shape = jax.ShapeDtypeStruct((b_, GROUPS, s_len, HEAD_DIM), BF16)
    aug_shape = jax.ShapeDtypeStruct((b_, GROUPS, s_len, 2 * HEAD_DIM), BF16)
    cmp_shape = jax.ShapeDtypeStruct((b_, GROUPS, nch, HEAD_DIM), BF16)
    return pl.pallas_call(
        _nsa_prep_kernel,
        out_shape=(aug_shape, aug_shape, seq_shape, aug_shape, cmp_shape, cmp_shape),
        grid=(b_, GROUPS),
        in_specs=[kv(0), kv(1), kv(2), kv(3), kv(4), kv(5),
                  full((s_len, LANES)), full((s_len, LANES)),
                  full((CMP_LEN, HEAD_DIM)), full((CMP_LEN * HEAD_DIM, CMP_HIDDEN)),
                  full((CMP_HIDDEN, HEAD_DIM)),
                  full((CMP_LEN, HEAD_DIM)), full((CMP_LEN * HEAD_DIM, CMP_HIDDEN)),
                  full((CMP_HIDDEN, HEAD_DIM))],
        out_specs=(aug_out, aug_out, seq_out, aug_out, cmp_out, cmp_out),
        scratch_shapes=[pltpu.VMEM((s_len, HEAD_DIM), F32)],
        compiler_params=_cparams(("parallel", "parallel")),
        name="nsa_prep",
    )(h_main, h_main, h_main, h_main, h_main, h_main, cosf, sinf,
      pe_k, w1_k, w2_k, pe_v, w1_v, w2_v)


NSA_TQ = 256
NSA_TK = 256


def _nsa_attn_kernel(q_ref, hs_ref, cos_ref, sin_ref, kc_ref, vc_ref, ks_ref, vs_ref,
                     kw_ref, vw_ref, ovt_ref, o_ref, os_ref):
    g = pl.program_id(1)
    qi = pl.program_id(2)
    tq, tk = NSA_TQ, NSA_TK
    rows = HPG * tq
    ns = ks_ref.shape[0] // SEL_BLOCK
    cosf = cos_ref[...]
    sinf = sin_ref[...]
    scale = HEAD_DIM ** -0.5
    qs = jnp.concatenate(
        [_rope(q_ref[:, hh * HEAD_DIM:(hh + 1) * HEAD_DIM], cosf, sinf) * scale
         for hh in range(HPG)], axis=0).astype(BF16)

    t_abs = qi * tq + lax.broadcasted_iota(jnp.int32, (tq, LANES), 0)
    lane = lax.broadcasted_iota(jnp.int32, (tq, LANES), 1)

    t_abs_k = qi * tq + lax.broadcasted_iota(jnp.int32, (tq, tk), 0)
    lane_k = lax.broadcasted_iota(jnp.int32, (tq, tk), 1)

    def add_bias(s_blk, bias):
        return (s_blk.reshape(HPG, tq, tk) + bias[None]).reshape(rows, tk)

    def split_rows(dot_fn, lhs, rhs):
        half = lhs.shape[0] // 2
        return jnp.concatenate([dot_fn(lhs[:half], rhs), dot_fn(lhs[half:], rhs)], axis=0)

    def fold(x):
        return x[:, :LANES], x[:, LANES:]

    neg_rows = jnp.full((rows, LANES), NEG_INF, F32)

    def running_max(score_blocks):
        mrun = neg_rows
        for s_blk in score_blocks:
            s0, s1 = fold(s_blk)
            mrun = jnp.maximum(mrun, jnp.maximum(s0, s1))
        return jnp.broadcast_to(jnp.max(mrun, axis=-1, keepdims=True), (rows, LANES))

    def probs(s_blk, m_b):
        s0, s1 = fold(s_blk)
        return jnp.exp(jnp.concatenate([(s0 - m_b).astype(BF16), (s1 - m_b).astype(BF16)], axis=1))

    def weighted_values(ps, value_rows):
        acc = _dot(jnp.concatenate(ps, axis=1), value_rows)
        return acc[:, :HEAD_DIM] / acc[:, HEAD_DIM:]

    kb_last = (qi * tq) // tk
    nwin = WIN // tk + 1
    win_start = pl.multiple_of(jnp.maximum(kb_last - (nwin - 1), 0) * tk, tk)
    win_raw = split_rows(_dot_nt, qs, kw_ref[pl.ds(win_start, nwin * tk), :])

    mask_c = jnp.concatenate([lane * CMP_STRIDE + (CMP_LEN - 1) <= t_abs] * HPG, axis=0)
    s_c = split_rows(_dot_nt, qs, kc_ref[...])
    m_c = jnp.max(jnp.where(mask_c, s_c, NEG_INF), axis=-1, keepdims=True)
    e_c = jnp.where(mask_c, jnp.exp(s_c - m_c), 0.0)
    l_c = jnp.sum(e_c, axis=-1, keepdims=True)
    p_c = jnp.where(l_c > 0.0, e_c / l_c, 0.0)

    win_s = []
    for d in range(nwin):
        diff = t_abs_k - (win_start + d * tk + lane_k)
        bias = jnp.where((diff >= 0) & (diff < WIN), 0.0, NEG_INF)
        win_s.append(add_bias(win_raw[:, d * tk:(d + 1) * tk], bias))
    win_m = running_max(win_s)

    o_c = split_rows(_dot, p_c.astype(BF16), vc_ref[...])
    psum = p_c[0:tq]
    for hh in range(1, HPG):
        psum = psum + p_c[hh * tq:(hh + 1) * tq]
    p_hi, p_lo = _split2(psum)
    ovt = ovt_ref[...]
    imp = (_dot_nt(ovt, p_hi) + _dot_nt(ovt, p_lo))[:ns]
    blk = lax.broadcasted_iota(jnp.int32, (ns, tq), 0)
    cur = (qi * tq + lax.broadcasted_iota(jnp.int32, (ns, tq), 1)) // SEL_BLOCK
    forced = (blk == 0) | (blk == cur) | (blk == cur - 1)
    imp = jnp.where(forced, jnp.inf, jnp.where(blk <= cur, imp, -jnp.inf))

    win_p = [probs(s_blk, win_m) for s_blk in win_s]

    rank = jnp.zeros((ns, tq), F32)
    for i in range(ns):
        ci = imp[i:i + 1, :]
        before = (ci > imp) | ((ci == imp) & (blk > i))
        rank = rank + jnp.where(before, 1.0, 0.0)
    unsel_t = jnp.where(rank < float(min(SEL_TOPK, ns)), 0.0, 1.0)
    unsel = jnp.concatenate([unsel_t, jnp.zeros((LANES - ns, tq), F32)], axis=0).T
    q_aug = jnp.concatenate([qs, jnp.concatenate([unsel.astype(BF16)] * HPG, axis=0)], axis=1)

    o_w = weighted_values(win_p, vw_ref[pl.ds(win_start, nwin * tk), :])

    def sel_variant(n_full):
        def run():
            n_keys = (n_full + 1) * tk
            s_all = split_rows(_dot_nt, q_aug, ks_ref[0:n_keys, :])
            s_blks = [s_all[:, j * tk:(j + 1) * tk] for j in range(n_full + 1)]
            bias = jnp.where(n_full * tk + lane_k <= t_abs_k, 0.0, NEG_INF)
            s_blks[n_full] = add_bias(s_blks[n_full], bias)
            m_b = running_max(s_blks)
            ps = [probs(s_blk, m_b) for s_blk in s_blks]
            os_ref[...] = weighted_values(ps, vs_ref[0:n_keys, :])
        return run

    for n_full in range(ks_ref.shape[0] // tk):
        pl.when(kb_last == n_full)(sel_variant(n_full))
    o_s = os_ref[...]

    hs = hs_ref[...]
    for hh in range(HPG):
        gbase = SC_GATE + (g * HPG + hh) * 3
        r = slice(hh * tq, (hh + 1) * tq)
        out = (_sigmoid(_lane_col(hs, gbase)) * o_c[r]
               + _sigmoid(_lane_col(hs, gbase + 1)) * o_s[r]
               + _sigmoid(_lane_col(hs, gbase + 2)) * o_w[r])
        o_ref[:, hh * HEAD_DIM:(hh + 1) * HEAD_DIM] = out.astype(o_ref.dtype)


def _nsa_attn(h_main, h_small, cosf, sinf, kc, vc, ks, vs, kw, vw, overlap_t):
    b_, s_len, _ = h_main.shape
    tq = NSA_TQ
    nch = kc.shape[2]
    assert nch == LANES, "the compressed-block axis is laid out on one vreg of lanes"
    assert s_len % NSA_TK == 0 and NSA_TK == 2 * LANES
    qw = HPG * HEAD_DIM
    seq = pl.BlockSpec((None, None, s_len, HEAD_DIM), lambda b, g, i: (b, g, 0, 0))
    aug = pl.BlockSpec((None, None, s_len, 2 * HEAD_DIM), lambda b, g, i: (b, g, 0, 0))
    cmp_ = pl.BlockSpec((None, None, nch, HEAD_DIM), lambda b, g, i: (b, g, 0, 0))
    return pl.pallas_call(
        _nsa_attn_kernel,
        out_shape=jax.ShapeDtypeStruct((b_, s_len, HEADS * HEAD_DIM), BF16),
        grid=(b_, GROUPS, s_len // tq),
        in_specs=[pl.BlockSpec((None, tq, qw), lambda b, g, i: (b, i, CB_NSA_Q * LANES // qw + g)),
                  pl.BlockSpec((None, tq, LANES), lambda b, g, i: (b, i, 0)),
                  pl.BlockSpec((tq, LANES), lambda b, g, i: (i, 0)),
                  pl.BlockSpec((tq, LANES), lambda b, g, i: (i, 0)),
                  cmp_, cmp_, aug, aug, seq, aug,
                  pl.BlockSpec(overlap_t.shape, lambda b, g, i: (0, 0))],
        out_specs=pl.BlockSpec((None, tq, qw), lambda b, g, i: (b, i, g)),
        scratch_shapes=[pltpu.VMEM((HPG * tq, HEAD_DIM), F32)],
        compiler_params=_cparams(("parallel", "parallel", "arbitrary")),
        name="nsa_attn",
    )(h_main, h_small, cosf, sinf, kc, vc, ks, vs, kw, vw, overlap_t)


def _nsa_constants(s_len):
    half = HEAD_DIM // 2
    inv_freq = ROPE_THETA ** (-jnp.arange(half, dtype=F32) / half)
    ang = jnp.arange(s_len, dtype=F32)[:, None] * inv_freq[None, :]
    cos, sin = jnp.cos(ang), jnp.sin(ang)
    cosf = jnp.concatenate([cos, cos], axis=-1)
    sinf = jnp.concatenate([-sin, sin], axis=-1)
    nch = s_len // CMP_STRIDE
    ns = s_len // SEL_BLOCK
    n = jnp.arange(nch)[:, None] * CMP_STRIDE
    j = jnp.arange(LANES)[None, :] * SEL_BLOCK
    overlap = ((n <= j + SEL_BLOCK - 1) & (n + CMP_LEN - 1 >= j)
               & (jnp.arange(nch)[:, None] < nch - CMP_LEN // CMP_STRIDE + 1)
               & (jnp.arange(LANES)[None, :] < ns)).astype(BF16)
    return cosf, sinf, overlap.T


def _layer(x, xb, p_i, w_in, conv_w, a_log, dt_bias, norm_w, pe_k, w1_k, w2_k, pe_v, w1_v, w2_v,
           w_a, w_b, w_out, ln1_g, ln1_b, w_gate, w_up, w_down, w_ple, w_ple_gate, ln2_g, ln2_b,
           consts):
    b_, s_len, d = x.shape
    t = b_ * s_len
    cosf, sinf, overlap_t = consts
    x2 = x.reshape(t, d)
    xb2 = xb.reshape(t, d)

    w_main = jnp.concatenate([w_in[:, :4096], w_in[:, 6696:], w_in[:, 4112:6672]], axis=1).astype(BF16)
    w_small = jnp.concatenate([w_in[:, 4096:4112], w_in[:, 6672:6696],
                               jnp.zeros((d, LANES - 40), w_in.dtype)], axis=1).astype(BF16)
    tm = min(1024, t)
    h_main = _matmul(xb2, w_main, F32, tm, 1536, "proj_main")
    h_small = _matmul(xb2, w_small, F32, tm, LANES, "proj_small")
    h_main3 = h_main.reshape(b_, s_len, N_MAIN)
    h_small3 = h_small.reshape(b_, s_len, LANES)

    o_a = _deltanet(h_main3, h_small3, conv_w, a_log, dt_bias, norm_w)
    ks, vs, kw, vw, kc, vc = _nsa_prep(h_main3, cosf, sinf, pe_k, w1_k.astype(BF16),
                                       w2_k.astype(BF16), pe_v, w1_v.astype(BF16),
                                       w2_v.astype(BF16))
    o_b = _nsa_attn(h_main3, h_small3, cosf, sinf, kc, vc, ks, vs, kw, vw, overlap_t)

    mixed = _merge(o_a.reshape(t, -1), o_b.reshape(t, -1), w_a.astype(BF16), w_b.astype(BF16),
                   h_main, min(512, t))
    x1, x1b = _outproj_ln(mixed, w_out.astype(BF16), x2, ln1_g, ln1_b, min(512, t))
    act = _ffn_act(x1b, w_gate, w_up, tm, 512)
    resid = _resid(x1, x1b, p_i.reshape(t, PLE_DIM).astype(BF16), w_ple.astype(BF16),
                   w_ple_gate.astype(BF16), min(512, t))
    y, yb = _ffn_out(act, w_down.astype(BF16), resid, ln2_g, ln2_b, min(256, t))
    return y.reshape(b_, s_len, d), yb.reshape(b_, s_len, d)


def kernel(x, p, w_in, dn_conv_w, dn_a_log, dn_dt_bias, dn_norm_w, cmp_pe_k, cmp_w1_k, cmp_w2_k, cmp_pe_v, cmp_w1_v, cmp_w2_v, w_branch_a, w_branch_b, w_out, ln1_g, ln1_b, w_ffn_gate, w_ffn_up, w_ffn_down, w_ple, w_ple_gate, ln2_g, ln2_b):
    consts = _nsa_constants(x.shape[1])
    xb = x.astype(BF16)
    for i in range(DEPTH):
        x, xb = _layer(x, xb, p[i], w_in[i], dn_conv_w[i], dn_a_log[i], dn_dt_bias[i], dn_norm_w[i],
                       cmp_pe_k[i], cmp_w1_k[i], cmp_w2_k[i], cmp_pe_v[i], cmp_w1_v[i], cmp_w2_v[i],
                       w_branch_a[i], w_branch_b[i], w_out[i], ln1_g[i], ln1_b[i],
                       w_ffn_gate[i], w_ffn_up[i], w_ffn_down[i], w_ple[i], w_ple_gate[i],
                       ln2_g[i], ln2_b[i], consts)
    return x
```

```python
import jax
import jax.numpy as jnp
from jax import lax
from jax.experimental import pallas as pl
from jax.experimental.pallas import tpu as pltpu

D_MODEL = 2048
DEPTH = 2
HEAD_DIM = 128
HEADS = 8
DN_CONV = 4
DN_CHUNK = 64
GROUPS = 2
HPG = HEADS // GROUPS
CMP_LEN = 32
CMP_STRIDE = 16
CMP_HIDDEN = 256
SEL_BLOCK = 64
SEL_TOPK = 16
WIN = 512
ROPE_THETA = 10000.0
D_FF = 5632
PLE_DIM = 256
ALPHA = (2.0 * DEPTH) ** 0.25
LN_EPS = 1e-5
NORM_EPS = 1e-6
NEG_INF = -1e30

LANES = 128
SUBLANES = 8
VMEM_LIMIT = 56 * 1024 * 1024

CB_DN_Q, CB_DN_K, CB_DN_V, CB_DN_Z = 0, 8, 16, 24
CB_NSA_Q = 64
CB_KV = 72
N_MAIN = 10752
COL_MERGE_A = 4096
COL_MERGE_B = 6144
SC_BETA, SC_DECAY, SC_GATE = 0, 8, 16

F32 = jnp.float32
BF16 = jnp.bfloat16


def _cparams(sem):
    return pltpu.CompilerParams(dimension_semantics=sem, vmem_limit_bytes=VMEM_LIMIT)


def _dot(a, b):
    return jnp.dot(a, b, preferred_element_type=F32)


def _dot_nt(a, b):
    return lax.dot_general(a, b, (((1,), (1,)), ((), ())), preferred_element_type=F32)


def _dot_tn(a, b):
    return lax.dot_general(a, b, (((0,), (0,)), ((), ())), preferred_element_type=F32)


def _sigmoid(x):
    return 0.5 * jnp.tanh(0.5 * x) + 0.5


def _silu(x):
    return x * _sigmoid(x)


def _layer_norm(y, g, b):
    mu = jnp.mean(y, axis=-1, keepdims=True)
    d = y - mu
    var = jnp.mean(d * d, axis=-1, keepdims=True)
    return d * lax.rsqrt(var + LN_EPS) * g + b


def _lane_col(x, idx):
    lane = lax.broadcasted_iota(jnp.int32, x.shape, 1)
    return jnp.sum(jnp.where(lane == idx, x, 0.0), axis=1, keepdims=True)


def _split2(x):
    hi = x.astype(BF16)
    return hi, (x - hi.astype(F32)).astype(BF16)


def _split3(x):
    x1 = x.astype(BF16)
    r = x - x1.astype(F32)
    x2 = r.astype(BF16)
    return x1, x2, (r - x2.astype(F32)).astype(BF16)


def _dot_01(ones_b, x):
    x1, x2, x3 = _split3(x)
    return _dot(ones_b, x1) + _dot(ones_b, x2) + _dot(ones_b, x3)


def _mm_kernel(a_ref, w_ref, o_ref):
    o_ref[...] = _dot(a_ref[...].astype(BF16), w_ref[...]).astype(o_ref.dtype)


def _matmul(a, w, out_dtype, tm, tn, name):
    m, k = a.shape
    n = w.shape[1]
    return pl.pallas_call(
        _mm_kernel,
        out_shape=jax.ShapeDtypeStruct((m, n), out_dtype),
        grid=(m // tm, n // tn),
        in_specs=[pl.BlockSpec((tm, k), lambda i, j: (i, 0)),
                  pl.BlockSpec((k, tn), lambda i, j: (0, j))],
        out_specs=pl.BlockSpec((tm, tn), lambda i, j: (i, j)),
        compiler_params=_cparams(("parallel", "arbitrary")),
        name=name,
    )(a, w)


MERGE_CHUNK = 512


def _merge_kernel(oa_ref, ob_ref, wa_ref, wb_ref, ma_ref, mb_ref, o_ref):
    oa = oa_ref[...]
    ob = ob_ref[...]
    for c in range(o_ref.shape[1] // MERGE_CHUNK):
        cols = slice(c * MERGE_CHUNK, (c + 1) * MERGE_CHUNK)
        ya = _dot(oa, wa_ref[:, cols])
        yb = _dot(ob, wb_ref[:, cols])
        o_ref[:, cols] = (_sigmoid(ma_ref[:, cols]) * ya
                          + _sigmoid(mb_ref[:, cols]) * yb).astype(o_ref.dtype)


def _merge(o_a, o_b, w_a, w_b, h_main, tm):
    m, k = o_a.shape
    n = w_a.shape[1]
    ca, cb = COL_MERGE_A // n, COL_MERGE_B // n
    return pl.pallas_call(
        _merge_kernel,
        out_shape=jax.ShapeDtypeStruct((m, n), BF16),
        grid=(m // tm,),
        in_specs=[pl.BlockSpec((tm, k), lambda i: (i, 0)),
                  pl.BlockSpec((tm, k), lambda i: (i, 0)),
                  pl.BlockSpec((k, n), lambda i: (0, 0), pipeline_mode=pl.Buffered(1)),
                  pl.BlockSpec((k, n), lambda i: (0, 0), pipeline_mode=pl.Buffered(1)),
                  pl.BlockSpec((tm, n), lambda i: (i, ca)),
                  pl.BlockSpec((tm, n), lambda i: (i, cb))],
        out_specs=pl.BlockSpec((tm, n), lambda i: (i, 0)),
        compiler_params=_cparams(("parallel",)),
        name="merge",
    )(o_a, o_b, w_a, w_b, h_main, h_main)


def _outproj_ln_kernel(mx_ref, w_ref, x_ref, g_ref, b_ref, o_ref, ob_ref):
    half = mx_ref.shape[0] // 2
    for r in (slice(0, half), slice(half, 2 * half)):
        y = ALPHA * x_ref[r, :] + _dot(mx_ref[r, :], w_ref[...])
        out = _layer_norm(y, g_ref[...], b_ref[...])
        o_ref[r, :] = out
        ob_ref[r, :] = out.astype(BF16)


def _outproj_ln(mixed, w_out, x, g, b, tm):
    m, d = x.shape
    return pl.pallas_call(
        _outproj_ln_kernel,
        out_shape=(jax.ShapeDtypeStruct((m, d), F32), jax.ShapeDtypeStruct((m, d), BF16)),
        grid=(m // tm,),
        in_specs=[pl.BlockSpec((tm, d), lambda i: (i, 0)),
                  pl.BlockSpec((d, d), lambda i: (0, 0), pipeline_mode=pl.Buffered(1)),
                  pl.BlockSpec((tm, d), lambda i: (i, 0)),
                  pl.BlockSpec((1, d), lambda i: (0, 0)),
                  pl.BlockSpec((1, d), lambda i: (0, 0))],
        out_specs=(pl.BlockSpec((tm, d), lambda i: (i, 0)),
                   pl.BlockSpec((tm, d), lambda i: (i, 0))),
        compiler_params=_cparams(("parallel",)),
        name="outproj_ln",
    )(mixed, w_out, x, g.reshape(1, d), b.reshape(1, d))


def _ffn_act_kernel(x_ref, wg_ref, wu_ref, o_ref):
    xv = x_ref[...]
    gate = _dot(xv, wg_ref[...].astype(BF16))
    up = _dot(xv, wu_ref[...].astype(BF16))
    o_ref[...] = (_silu(gate) * up).astype(o_ref.dtype)


def _ffn_act(xb, w_gate, w_up, layer, tm, tn):
    m, k = xb.shape
    n = w_gate.shape[2]
    return pl.pallas_call(
        _ffn_act_kernel,
        out_shape=jax.ShapeDtypeStruct((m, n), BF16),
        grid=(m // tm, n // tn),
        in_specs=[pl.BlockSpec((tm, k), lambda i, j: (i, 0)),
                  pl.BlockSpec((None, k, tn), lambda i, j: (layer, 0, j)),
                  pl.BlockSpec((None, k, tn), lambda i, j: (layer, 0, j))],
        out_specs=pl.BlockSpec((tm, tn), lambda i, j: (i, j)),
        compiler_params=_cparams(("parallel", "arbitrary")),
        name="ffn_act",
    )(xb, w_gate, w_up)


def _resid_kernel(x_ref, xb_ref, p_ref, wp_ref, wpg_ref, o_ref):
    xb = xb_ref[...]
    pv = p_ref[...]
    for c in range(o_ref.shape[1] // MERGE_CHUNK):
        cols = slice(c * MERGE_CHUNK, (c + 1) * MERGE_CHUNK)
        ple = _dot(pv, wp_ref[:, cols]) * _sigmoid(_dot(xb, wpg_ref[:, cols]))
        o_ref[:, cols] = ALPHA * x_ref[:, cols] + ple


def _resid(x1, x1b, pb, w_ple, w_ple_gate, tm):
    m, d = x1.shape
    kp = pb.shape[1]
    return pl.pallas_call(
        _resid_kernel,
        out_shape=jax.ShapeDtypeStruct((m, d), F32),
        grid=(m // tm,),
        in_specs=[pl.BlockSpec((tm, d), lambda i: (i, 0)),
                  pl.BlockSpec((tm, d), lambda i: (i, 0)),
                  pl.BlockSpec((tm, kp), lambda i: (i, 0)),
                  pl.BlockSpec((kp, d), lambda i: (0, 0), pipeline_mode=pl.Buffered(1)),
                  pl.BlockSpec((d, d), lambda i: (0, 0), pipeline_mode=pl.Buffered(1))],
        out_specs=pl.BlockSpec((tm, d), lambda i: (i, 0)),
        compiler_params=_cparams(("parallel",)),
        name="ple_resid",
    )(x1, x1b, pb, w_ple, w_ple_gate)


def _ffn_out_kernel(act_ref, w_ref, r_ref, g_ref, b_ref, o_ref, ob_ref):
    half = act_ref.shape[0] // 2
    for r in (slice(0, half), slice(half, 2 * half)):
        y = r_ref[r, :] + _dot(act_ref[r, :], w_ref[...])
        out = _layer_norm(y, g_ref[...], b_ref[...])
        o_ref[r, :] = out
        ob_ref[r, :] = out.astype(BF16)


def _ffn_out(act, w_down, resid, g, b, tm):
    m, kf = act.shape
    d = w_down.shape[1]
    return pl.pallas_call(
        _ffn_out_kernel,
        out_shape=(jax.ShapeDtypeStruct((m, d), F32), jax.ShapeDtypeStruct((m, d), BF16)),
        grid=(m // tm,),
        in_specs=[pl.BlockSpec((tm, kf), lambda i: (i, 0)),
                  pl.BlockSpec((kf, d), lambda i: (0, 0), pipeline_mode=pl.Buffered(1)),
                  pl.BlockSpec((tm, d), lambda i: (i, 0)),
                  pl.BlockSpec((1, d), lambda i: (0, 0)),
                  pl.BlockSpec((1, d), lambda i: (0, 0))],
        out_specs=(pl.BlockSpec((tm, d), lambda i: (i, 0)),
                   pl.BlockSpec((tm, d), lambda i: (i, 0))),
        compiler_params=_cparams(("parallel",)),
        name="ffn_out_ln",
    )(act, w_down, resid, g.reshape(1, d), b.reshape(1, d))


DN_TS = 256
DN_HB = 8


def _dn_kernel(alog_ref, dtb_ref,
               q_ref, k_ref, v_ref, z_ref, qp_ref, kp_ref, vp_ref, hs_ref,
               cwq_ref, cwk_ref, cwv_ref, nw_ref, o_ref, state_ref, cbuf_ref):
    hg = pl.program_id(1)
    s = pl.program_id(2)
    c = DN_CHUNK
    d = HEAD_DIM

    @pl.when(s == 0)
    def _():
        state_ref[...] = jnp.zeros_like(state_ref)

    conv_slots = []

    def conv_silu(x_ref, xp_ref, cw_ref, hb):
        cols = slice(hb * d, (hb + 1) * d)
        buf = cbuf_ref.at[len(conv_slots)]
        conv_slots.append(None)
        buf[0:SUBLANES, :] = jnp.where(s == 0, 0.0, xp_ref[:, cols])
        buf[SUBLANES:, :] = x_ref[:, cols]
        cw = cw_ref[:, cols]
        y = None
        for i in range(DN_CONV):
            off = SUBLANES - (DN_CONV - 1) + i
            term = buf[off:off + DN_TS, :] * cw[i:i + 1]
            y = term if y is None else y + term
        return _silu(y)

    def l2norm(x):
        return x * lax.rsqrt(jnp.sum(x * x, axis=-1, keepdims=True) + NORM_EPS)

    row = lax.broadcasted_iota(jnp.int32, (c, c), 0)
    col = lax.broadcasted_iota(jnp.int32, (c, c), 1)
    causal = row >= col
    strict = row > col
    tri_b = jnp.where(causal, 1.0, 0.0).astype(BF16)
    eye = jnp.where(row == col, 1.0, 0.0).astype(F32)
    ones8_b = jnp.ones((SUBLANES, c), BF16)
    hs = hs_ref[...]
    nw = nw_ref[...]

    nck = DN_TS // c
    pairs = [(hb, ci) for hb in range(DN_HB) for ci in range(nck)]
    qs, ks, vs, betas, gbs = [], [], [], [], []
    for hb in range(DN_HB):
        h = hg * DN_HB + hb
        q_all = l2norm(conv_silu(q_ref, qp_ref, cwq_ref, hb)) * (d ** -0.5)
        k_all = l2norm(conv_silu(k_ref, kp_ref, cwk_ref, hb))
        v_all = conv_silu(v_ref, vp_ref, cwv_ref, hb)
        beta_all = _sigmoid(_lane_col(hs, SC_BETA + h))
        a_all = _lane_col(hs, SC_DECAY + h) + dtb_ref[h]
        softplus = jnp.maximum(a_all, 0.0) + jnp.log(1.0 + jnp.exp(-jnp.abs(a_all)))
        g_all = -jnp.exp(jnp.zeros_like(a_all) + alog_ref[h]) * softplus
        for ci in range(nck):
            sl = slice(ci * c, (ci + 1) * c)
            qs.append(q_all[sl])
            ks.append(k_all[sl])
            vs.append(v_all[sl])
            betas.append(beta_all[sl])
            gbs.append(jnp.broadcast_to(g_all[sl], (c, LANES)))
    n = len(pairs)
    gcs = [_dot_01(tri_b, gbs[i]) for i in range(n)]
    gc_rows = [_dot_01(ones8_b, jnp.where(row <= col, gbs[i][:, :c], 0.0))[0:1] for i in range(n)]
    decays = [jnp.exp(jnp.where(causal, gcs[i][:, :c] - gc_rows[i], NEG_INF)) for i in range(n)]
    kbs = [ks[i] * betas[i] for i in range(n)]
    kbfs = [ks[i].astype(BF16) for i in range(n)]
    negs = [jnp.where(strict, -(_dot_nt(kbs[i].astype(BF16), kbfs[i]) * decays[i]), 0.0)
            for i in range(n)]
    accs = [eye + negs[i] for i in range(n)]
    nbf = [negs[i].astype(BF16) for i in range(n)]
    pws = [_dot(nbf[i], nbf[i]) for i in range(n)]
    for _ in range(4):
        pbs = [pws[i].astype(BF16) for i in range(n)]
        accs = [accs[i] + _dot(pbs[i], accs[i].astype(BF16)) for i in range(n)]
        pws = [_dot(pbs[i], pbs[i]) for i in range(n)]
    tinvs = [(accs[i] + _dot(pws[i].astype(BF16), accs[i].astype(BF16))).astype(BF16)
             for i in range(n)]
    egs = [jnp.exp(gcs[i]) for i in range(n)]
    uws = [_dot(tinvs[i], jnp.concatenate([vs[i] * betas[i], kbs[i] * egs[i]], axis=1).astype(BF16))
           .astype(BF16) for i in range(n)]
    qks = [jnp.where(causal, _dot_nt(qs[i].astype(BF16), kbfs[i]) * decays[i], 0.0).astype(BF16)
           for i in range(n)]
    k_decs = [(ks[i] * jnp.exp(gcs[i][c - 1:c] - gcs[i])).astype(BF16) for i in range(n)]
    nbs = [_dot_tn(k_decs[i], uws[i]) for i in range(n)]
    prs = [_dot(qks[i], uws[i]) for i in range(n)]
    lhss = [jnp.concatenate([(qs[i] * egs[i] - prs[i][:, d:]).astype(BF16),
                             nbs[i][:, d:].astype(BF16)], axis=0) for i in range(n)]

    states = [state_ref[hb] for hb in range(DN_HB)]
    for ci in range(nck):
        sl = slice(ci * c, (ci + 1) * c)
        ress = [_dot(lhss[hb * nck + ci], states[hb].astype(BF16)) for hb in range(DN_HB)]
        for hb in range(DN_HB):
            i = hb * nck + ci
            cols = slice(hb * d, (hb + 1) * d)
            o = ress[hb][:c] + prs[i][:, :d]
            states[hb] = states[hb] * egs[i][c - 1:c] - ress[hb][c:] + nbs[i][:, :d]
            o = o * lax.rsqrt(jnp.mean(o * o, axis=-1, keepdims=True) + NORM_EPS) * nw
            o_ref[sl, cols] = (o * _silu(z_ref[sl, cols])).astype(o_ref.dtype)
    for hb in range(DN_HB):
        state_ref[hb] = states[hb]


def _deltanet(h_main, h_small, conv_w, a_log, dt_bias, norm_w):
    b_, s_len, _ = h_main.shape
    ts = DN_TS
    wd = DN_HB * HEAD_DIM
    ngrp = HEADS // DN_HB
    blk = lambda cb: pl.BlockSpec((None, ts, wd), lambda b, h, s: (b, s, cb // DN_HB + h))
    prev = lambda cb: pl.BlockSpec(
        (None, SUBLANES, wd),
        lambda b, h, s: (b, jnp.maximum(s * (ts // SUBLANES) - 1, 0), cb // DN_HB + h))
    cw = lambda cb: pl.BlockSpec((DN_CONV, wd), lambda b, h, s: (0, cb // DN_HB + h))
    smem = pl.BlockSpec(memory_space=pltpu.SMEM)
    return pl.pallas_call(
        _dn_kernel,
        out_shape=jax.ShapeDtypeStruct((b_, s_len, HEADS * HEAD_DIM), BF16),
        grid=(b_, ngrp, s_len // ts),
        in_specs=[smem, smem,
                  blk(CB_DN_Q), blk(CB_DN_K), blk(CB_DN_V), blk(CB_DN_Z),
                  prev(CB_DN_Q), prev(CB_DN_K), prev(CB_DN_V),
                  pl.BlockSpec((None, ts, LANES), lambda b, h, s: (b, s, 0)),
                  cw(CB_DN_Q), cw(CB_DN_K), cw(CB_DN_V),
                  pl.BlockSpec((1, LANES), lambda b, h, s: (0, 0))],
        out_specs=pl.BlockSpec((None, ts, wd), lambda b, h, s: (b, s, h)),
        scratch_shapes=[pltpu.VMEM((DN_HB, HEAD_DIM, HEAD_DIM), F32),
                        pltpu.VMEM((3 * DN_HB, SUBLANES + DN_TS, HEAD_DIM), F32)],
        compiler_params=_cparams(("parallel", "parallel", "arbitrary")),
        name="deltanet",
    )(a_log, dt_bias, h_main, h_main, h_main, h_main, h_main, h_main, h_main, h_small,
      conv_w, conv_w, conv_w, norm_w.reshape(1, LANES))


def _rope(x, cosf, sinf):
    return x * cosf + pltpu.roll(x, HEAD_DIM // 2, axis=1) * sinf


def _nsa_prep_kernel(kc_ref, vc_ref, ks_ref, vs_ref, kw_ref, vw_ref, cos_ref, sin_ref,
                     pek_ref, w1k_ref, w2k_ref, pev_ref, w1v_ref, w2v_ref,
                     kso_ref, vso_ref, kwo_ref, vwo_ref, kco_ref, vco_ref, buf_ref):
    cosf = cos_ref[...]
    sinf = sin_ref[...]
    s_len = ks_ref.shape[0]
    pos = lax.broadcasted_iota(jnp.int32, (s_len, LANES), 0)
    lane = lax.broadcasted_iota(jnp.int32, (s_len, LANES), 1)
    kso_ref[:, :HEAD_DIM] = _rope(ks_ref[...], cosf, sinf).astype(BF16)
    kso_ref[:, HEAD_DIM:] = jnp.where(pos // SEL_BLOCK == lane, NEG_INF, 0.0).astype(BF16)
    kwo_ref[...] = _rope(kw_ref[...], cosf, sinf).astype(BF16)
    ones = jnp.ones((s_len, HEAD_DIM), BF16)
    vso_ref[:, :HEAD_DIM] = vs_ref[...].astype(BF16)
    vso_ref[:, HEAD_DIM:] = ones
    vwo_ref[:, :HEAD_DIM] = vw_ref[...].astype(BF16)
    vwo_ref[:, HEAD_DIM:] = ones
    nch = buf_ref.shape[0] // CMP_STRIDE

    def compress(pe_ref, w1_ref, w2_ref, out_ref):
        a0 = jnp.zeros((nch, CMP_HIDDEN), F32)
        a1 = jnp.zeros((nch, CMP_HIDDEN), F32)
        for i in range(CMP_STRIDE):
            xi = buf_ref[pl.ds(i, nch, stride=CMP_STRIDE), :]
            lo = (xi + pe_ref[i:i + 1, :]).astype(BF16)
            hi = (xi + pe_ref[CMP_STRIDE + i:CMP_STRIDE + i + 1, :]).astype(BF16)
            a0 = a0 + _dot(lo, w1_ref[i * HEAD_DIM:(i + 1) * HEAD_DIM, :])
            a1 = a1 + _dot(hi, w1_ref[(CMP_STRIDE + i) * HEAD_DIM:(CMP_STRIDE + i + 1) * HEAD_DIM, :])
        hid = a0 + pltpu.roll(a1, nch - 1, axis=0)
        out_ref[...] = _dot(_silu(hid).astype(BF16), w2_ref[...]).astype(out_ref.dtype)

    buf_ref[...] = _rope(kc_ref[...], cosf, sinf)
    compress(pek_ref, w1k_ref, w2k_ref, kco_ref)
    buf_ref[...] = vc_ref[...]
    compress(pev_ref, w1v_ref, w2v_ref, vco_ref)


def _nsa_prep(h_main, cosf, sinf, pe_k, w1_k, w2_k, pe_v, w1_v, w2_v):
    b_, s_len, _ = h_main.shape
    nch = s_len // CMP_STRIDE
    kv = lambda i: pl.BlockSpec((None, s_len, LANES), lambda b, g: (b, 0, CB_KV + 2 * i + g))
    full = lambda shape: pl.BlockSpec(shape, lambda b, g: tuple(0 for _ in shape))
    assert s_len // SEL_BLOCK <= LANES
    seq_out = pl.BlockSpec((None, None, s_len, LANES), lambda b, g: (b, g, 0, 0))
    aug_out = pl.BlockSpec((None, None, s_len, 2 * LANES), lambda b, g: (b, g, 0, 0))
    cmp_out = pl.BlockSpec((None, None, nch, LANES), lambda b, g: (b, g, 0, 0))
    seq_shape = jax.ShapeDtypeStruct((b_, GROUPS, s_len, HEAD_DIM), BF16)
    aug_shape = jax.ShapeDtypeStruct((b_, GROUPS, s_len, 2 * HEAD_DIM), BF16)
    cmp_shape = jax.ShapeDtypeStruct((b_, GROUPS, nch, HEAD_DIM), BF16)
    return pl.pallas_call(
        _nsa_prep_kernel,
        out_shape=(aug_shape, aug_shape, seq_shape, aug_shape, cmp_shape, cmp_shape),
        grid=(b_, GROUPS),
        in_specs=[kv(0), kv(1), kv(2), kv(3), kv(4), kv(5),
                  full((s_len, LANES)), full((s_len, LANES)),
                  full((CMP_LEN, HEAD_DIM)), full((CMP_LEN * HEAD_DIM, CMP_HIDDEN)),
                  full((CMP_HIDDEN, HEAD_DIM)),
                  full((CMP_LEN, HEAD_DIM)), full((CMP_LEN * HEAD_DIM, CMP_HIDDEN)),
                  full((CMP_HIDDEN, HEAD_DIM))],
        out_specs=(aug_out, aug_out, seq_out, aug_out, cmp_out, cmp_out),
        scratch_shapes=[pltpu.VMEM((s_len, HEAD_DIM), F32)],
        compiler_params=_cparams(("parallel", "parallel")),
        name="nsa_prep",
    )(h_main, h_main, h_main, h_main, h_main, h_main, cosf, sinf,
      pe_k, w1_k, w2_k, pe_v, w1_v, w2_v)


NSA_TQ = 256
NSA_TK = 256


def _nsa_attn_kernel(q_ref, hs_ref, cos_ref, sin_ref, kc_ref, vc_ref, ks_ref, vs_ref,
                     kw_ref, vw_ref, ovt_ref, o_ref, os_ref):
    g = pl.program_id(1)
    qi = pl.program_id(2)
    tq, tk = NSA_TQ, NSA_TK
    rows = HPG * tq
    ns = ks_ref.shape[0] // SEL_BLOCK
    cosf = cos_ref[...]
    sinf = sin_ref[...]
    scale = HEAD_DIM ** -0.5
    qs = jnp.concatenate(
        [_rope(q_ref[:, hh * HEAD_DIM:(hh + 1) * HEAD_DIM], cosf, sinf) * scale
         for hh in range(HPG)], axis=0).astype(BF16)

    t_abs = qi * tq + lax.broadcasted_iota(jnp.int32, (tq, LANES), 0)
    lane = lax.broadcasted_iota(jnp.int32, (tq, LANES), 1)

    t_abs_k = qi * tq + lax.broadcasted_iota(jnp.int32, (tq, tk), 0)
    lane_k = lax.broadcasted_iota(jnp.int32, (tq, tk), 1)

    def add_bias(s_blk, bias):
        return (s_blk.reshape(HPG, tq, tk) + bias[None]).reshape(rows, tk)

    def split_rows(dot_fn, lhs, rhs):
        half = lhs.shape[0] // 2
        return jnp.concatenate([dot_fn(lhs[:half], rhs), dot_fn(lhs[half:], rhs)], axis=0)

    def fold(x):
        return x[:, :LANES], x[:, LANES:]

    neg_rows = jnp.full((rows, LANES), NEG_INF, F32)

    def running_max(score_blocks):
        mrun = neg_rows
        for s_blk in score_blocks:
            s0, s1 = fold(s_blk)
            mrun = jnp.maximum(mrun, jnp.maximum(s0, s1))
        return jnp.broadcast_to(jnp.max(mrun, axis=-1, keepdims=True), (rows, LANES))

    def probs(s_blk, m_b):
        s0, s1 = fold(s_blk)
        return jnp.exp(jnp.concatenate([(s0 - m_b).astype(BF16), (s1 - m_b).astype(BF16)], axis=1))

    def weighted_values(ps, value_rows):
        acc = _dot(jnp.concatenate(ps, axis=1), value_rows)
        return acc[:, :HEAD_DIM] / acc[:, HEAD_DIM:]

    kb_last = (qi * tq) // tk
    nwin = WIN // tk + 1
    win_start = pl.multiple_of(jnp.maximum(kb_last - (nwin - 1), 0) * tk, tk)
    win_raw = split_rows(_dot_nt, qs, kw_ref[pl.ds(win_start, nwin * tk), :])

    mask_c = jnp.concatenate([lane * CMP_STRIDE + (CMP_LEN - 1) <= t_abs] * HPG, axis=0)
    s_c = split_rows(_dot_nt, qs, kc_ref[...])
    m_c = jnp.max(jnp.where(mask_c, s_c, NEG_INF), axis=-1, keepdims=True)
    e_c = jnp.where(mask_c, jnp.exp(s_c - m_c), 0.0)
    l_c = jnp.sum(e_c, axis=-1, keepdims=True)
    p_c = jnp.where(l_c > 0.0, e_c / l_c, 0.0)

    win_s = []
    for d in range(nwin):
        diff = t_abs_k - (win_start + d * tk + lane_k)
        bias = jnp.where((diff >= 0) & (diff < WIN), 0.0, NEG_INF)
        win_s.append(add_bias(win_raw[:, d * tk:(d + 1) * tk], bias))
    win_m = running_max(win_s)

    o_c = split_rows(_dot, p_c.astype(BF16), vc_ref[...])
    psum = p_c[0:tq]
    for hh in range(1, HPG):
        psum = psum + p_c[hh * tq:(hh + 1) * tq]
    p_hi, p_lo = _split2(psum)
    ovt = ovt_ref[...]
    imp = (_dot_nt(ovt, p_hi) + _dot_nt(ovt, p_lo))[:ns]
    blk = lax.broadcasted_iota(jnp.int32, (ns, tq), 0)
    cur = (qi * tq + lax.broadcasted_iota(jnp.int32, (ns, tq), 1)) // SEL_BLOCK
    forced = (blk == 0) | (blk == cur) | (blk == cur - 1)
    imp = jnp.where(forced, jnp.inf, jnp.where(blk <= cur, imp, -jnp.inf))

    win_p = [probs(s_blk, win_m) for s_blk in win_s]

    rank = jnp.zeros((ns, tq), F32)
    for i in range(ns):
        ci = imp[i:i + 1, :]
        before = (ci > imp) | ((ci == imp) & (blk > i))
        rank = rank + jnp.where(before, 1.0, 0.0)
    unsel_t = jnp.where(rank < float(min(SEL_TOPK, ns)), 0.0, 1.0)
    unsel = jnp.concatenate([unsel_t, jnp.zeros((LANES - ns, tq), F32)], axis=0).T
    q_aug = jnp.concatenate([qs, jnp.concatenate([unsel.astype(BF16)] * HPG, axis=0)], axis=1)

    o_w = weighted_values(win_p, vw_ref[pl.ds(win_start, nwin * tk), :])

    def sel_variant(n_full):
        def run():
            n_keys = (n_full + 1) * tk
            s_all = split_rows(_dot_nt, q_aug, ks_ref[0:n_keys, :])
            s_blks = [s_all[:, j * tk:(j + 1) * tk] for j in range(n_full + 1)]
            bias = jnp.where(n_full * tk + lane_k <= t_abs_k, 0.0, NEG_INF)
            s_blks[n_full] = add_bias(s_blks[n_full], bias)
            m_b = running_max(s_blks)
            ps = [probs(s_blk, m_b) for s_blk in s_blks]
            os_ref[...] = weighted_values(ps, vs_ref[0:n_keys, :])
        return run

    for n_full in range(ks_ref.shape[0] // tk):
        pl.when(kb_last == n_full)(sel_variant(n_full))
    o_s = os_ref[...]

    hs = hs_ref[...]
    for hh in range(HPG):
        gbase = SC_GATE + (g * HPG + hh) * 3
        r = slice(hh * tq, (hh + 1) * tq)
        out = (_sigmoid(_lane_col(hs, gbase)) * o_c[r]
               + _sigmoid(_lane_col(hs, gbase + 1)) * o_s[r]
               + _sigmoid(_lane_col(hs, gbase + 2)) * o_w[r])
        o_ref[:, hh * HEAD_DIM:(hh + 1) * HEAD_DIM] = out.astype(o_ref.dtype)


def _nsa_attn(h_main, h_small, cosf, sinf, kc, vc, ks, vs, kw, vw, overlap_t):
    b_, s_len, _ = h_main.shape
    tq = NSA_TQ
    nch = kc.shape[2]
    assert nch == LANES, "the compressed-block axis is laid out on one vreg of lanes"
    assert s_len % NSA_TK == 0 and NSA_TK == 2 * LANES
    qw = HPG * HEAD_DIM
    seq = pl.BlockSpec((None, None, s_len, HEAD_DIM), lambda b, g, i: (b, g, 0, 0))
    aug = pl.BlockSpec((None, None, s_len, 2 * HEAD_DIM), lambda b, g, i: (b, g, 0, 0))
    cmp_ = pl.BlockSpec((None, None, nch, HEAD_DIM), lambda b, g, i: (b, g, 0, 0))
    return pl.pallas_call(
        _nsa_attn_kernel,
        out_shape=jax.ShapeDtypeStruct((b_, s_len, HEADS * HEAD_DIM), BF16),
        grid=(b_, GROUPS, s_len // tq),
        in_specs=[pl.BlockSpec((None, tq, qw), lambda b, g, i: (b, i, CB_NSA_Q * LANES // qw + g)),
                  pl.BlockSpec((None, tq, LANES), lambda b, g, i: (b, i, 0)),
                  pl.BlockSpec((tq, LANES), lambda b, g, i: (i, 0)),
                  pl.BlockSpec((tq, LANES), lambda b, g, i: (i, 0)),
                  cmp_, cmp_, aug, aug, seq, aug,
                  pl.BlockSpec(overlap_t.shape, lambda b, g, i: (0, 0))],
        out_specs=pl.BlockSpec((None, tq, qw), lambda b, g, i: (b, i, g)),
        scratch_shapes=[pltpu.VMEM((HPG * tq, HEAD_DIM), F32)],
        compiler_params=_cparams(("parallel", "parallel", "arbitrary")),
        name="nsa_attn",
    )(h_main, h_small, cosf, sinf, kc, vc, ks, vs, kw, vw, overlap_t)


def _nsa_constants(s_len):
    half = HEAD_DIM // 2
    inv_freq = ROPE_THETA ** (-jnp.arange(half, dtype=F32) / half)
    ang = jnp.arange(s_len, dtype=F32)[:, None] * inv_freq[None, :]
    cos, sin = jnp.cos(ang), jnp.sin(ang)
    cosf = jnp.concatenate([cos, cos], axis=-1)
    sinf = jnp.concatenate([-sin, sin], axis=-1)
    nch = s_len // CMP_STRIDE
    ns = s_len // SEL_BLOCK
    n = jnp.arange(nch)[:, None] * CMP_STRIDE
    j = jnp.arange(LANES)[None, :] * SEL_BLOCK
    overlap = ((n <= j + SEL_BLOCK - 1) & (n + CMP_LEN - 1 >= j)
               & (jnp.arange(nch)[:, None] < nch - CMP_LEN // CMP_STRIDE + 1)
               & (jnp.arange(LANES)[None, :] < ns)).astype(BF16)
    return cosf, sinf, overlap.T


def _layer(x, xb, p_i, w_in, conv_w, a_log, dt_bias, norm_w, pe_k, w1_k, w2_k, pe_v, w1_v, w2_v,
           w_a, w_b, w_out, ln1_g, ln1_b, w_gate_all, w_up_all, layer, w_down, w_ple, w_ple_gate,
           ln2_g, ln2_b, consts):
    b_, s_len, d = x.shape
    t = b_ * s_len
    cosf, sinf, overlap_t = consts
    x2 = x.reshape(t, d)
    xb2 = xb.reshape(t, d)

    w_main = jnp.concatenate([w_in[:, :4096], w_in[:, 6696:], w_in[:, 4112:6672]], axis=1).astype(BF16)
    w_small = jnp.concatenate([w_in[:, 4096:4112], w_in[:, 6672:6696],
                               jnp.zeros((d, LANES - 40), w_in.dtype)], axis=1).astype(BF16)
    tm = min(1024, t)
    h_main = _matmul(xb2, w_main, F32, tm, 1536, "proj_main")
    h_small = _matmul(xb2, w_small, F32, tm, LANES, "proj_small")
    h_main3 = h_main.reshape(b_, s_len, N_MAIN)
    h_small3 = h_small.reshape(b_, s_len, LANES)

    o_a = _deltanet(h_main3, h_small3, conv_w, a_log, dt_bias, norm_w)
    ks, vs, kw, vw, kc, vc = _nsa_prep(h_main3, cosf, sinf, pe_k, w1_k.astype(BF16),
                                       w2_k.astype(BF16), pe_v, w1_v.astype(BF16),
                                       w2_v.astype(BF16))
    o_b = _nsa_attn(h_main3, h_small3, cosf, sinf, kc, vc, ks, vs, kw, vw, overlap_t)

    mixed = _merge(o_a.reshape(t, -1), o_b.reshape(t, -1), w_a.astype(BF16), w_b.astype(BF16),
                   h_main, min(512, t))
    x1, x1b = _outproj_ln(mixed, w_out.astype(BF16), x2, ln1_g, ln1_b, min(512, t))
    act = _ffn_act(x1b, w_gate_all, w_up_all, layer, tm, 512)
    resid = _resid(x1, x1b, p_i.reshape(t, PLE_DIM).astype(BF16), w_ple.astype(BF16),
                   w_ple_gate.astype(BF16), min(512, t))
    y, yb = _ffn_out(act, w_down.astype(BF16), resid, ln2_g, ln2_b, min(256, t))
    return y.reshape(b_, s_len, d), yb.reshape(b_, s_len, d)


def kernel(x, p, w_in, dn_conv_w, dn_a_log, dn_dt_bias, dn_norm_w, cmp_pe_k, cmp_w1_k, cmp_w2_k, cmp_pe_v, cmp_w1_v, cmp_w2_v, w_branch_a, w_branch_b, w_out, ln1_g, ln1_b, w_ffn_gate, w_ffn_up, w_ffn_down, w_ple, w_ple_gate, ln2_g, ln2_b):
    consts = _nsa_constants(x.shape[1])
    xb = x
    for i in range(DEPTH):
        x, xb = _layer(x, xb, p[i], w_in[i], dn_conv_w[i], dn_a_log[i], dn_dt_bias[i], dn_norm_w[i],
                       cmp_pe_k[i], cmp_w1_k[i], cmp_w2_k[i], cmp_pe_v[i], cmp_w1_v[i], cmp_w2_v[i],
                       w_branch_a[i], w_branch_b[i], w_out[i], ln1_g[i], ln1_b[i],
                       w_ffn_gate, w_ffn_up, i, w_ffn_down[i], w_ple[i], w_ple_gate[i],
                       ln2_g[i], ln2_b[i], consts)
    return x
```

```python
import jax
import jax.numpy as jnp
from jax import lax
from jax.experimental import pallas as pl
from jax.experimental.pallas import tpu as pltpu

D_MODEL = 2048
DEPTH = 2
HEAD_DIM = 128
HEADS = 8
DN_CONV = 4
DN_CHUNK = 64
GROUPS = 2
HPG = HEADS // GROUPS
CMP_LEN = 32
CMP_STRIDE = 16
CMP_HIDDEN = 256
SEL_BLOCK = 64
SEL_TOPK = 16
WIN = 512
ROPE_THETA = 10000.0
PLE_DIM = 256
ALPHA = (2.0 * DEPTH) ** 0.25
LN_EPS = 1e-5
NORM_EPS = 1e-6
NEG_INF = -1e30

LANES = 128
SUBLANES = 8
VMEM_LIMIT = 56 * 1024 * 1024

W_DN = 4 * HEADS * HEAD_DIM
W_NSA = (HEADS + 6 * GROUPS) * HEAD_DIM
IN_DN_SMALL = W_DN
IN_NSA = IN_DN_SMALL + 2 * HEADS
IN_GATE = IN_NSA + W_NSA
IN_MERGE = IN_GATE + 3 * HEADS
CB_DN_Q, CB_DN_K, CB_DN_V, CB_DN_Z = 0, HEADS, 2 * HEADS, 3 * HEADS
COL_MERGE_A = W_DN
COL_MERGE_B = W_DN + D_MODEL
CB_NSA_Q = (W_DN + 2 * D_MODEL) // LANES
CB_KV = CB_NSA_Q + HEADS
N_MAIN = W_DN + 2 * D_MODEL + W_NSA
SC_BETA, SC_DECAY, SC_GATE = 0, HEADS, 2 * HEADS

F32 = jnp.float32
BF16 = jnp.bfloat16


def _cparams(sem):
    return pltpu.CompilerParams(dimension_semantics=sem, vmem_limit_bytes=VMEM_LIMIT)


def _dot(a, b):
    return jnp.dot(a, b, preferred_element_type=F32)


def _dot_nt(a, b):
    return lax.dot_general(a, b, (((1,), (1,)), ((), ())), preferred_element_type=F32)


def _dot_tn(a, b):
    return lax.dot_general(a, b, (((0,), (0,)), ((), ())), preferred_element_type=F32)


def _sigmoid(x):
    return 0.5 * jnp.tanh(0.5 * x) + 0.5


def _silu(x):
    return x * _sigmoid(x)


def _layer_norm(y, g, b):
    mu = jnp.mean(y, axis=-1, keepdims=True)
    d = y - mu
    var = jnp.mean(d * d, axis=-1, keepdims=True)
    return d * lax.rsqrt(var + LN_EPS) * g + b


def _lane_col(x, idx):
    lane = lax.broadcasted_iota(jnp.int32, x.shape, 1)
    return jnp.sum(jnp.where(lane == idx, x, 0.0), axis=1, keepdims=True)


def _split2(x):
    hi = x.astype(BF16)
    return hi, (x - hi.astype(F32)).astype(BF16)


def _split3(x):
    x1 = x.astype(BF16)
    r = x - x1.astype(F32)
    x2 = r.astype(BF16)
    return x1, x2, (r - x2.astype(F32)).astype(BF16)


def _dot_01(ones_b, x):
    x1, x2, x3 = _split3(x)
    return _dot(ones_b, x1) + _dot(ones_b, x2) + _dot(ones_b, x3)


def _mm_kernel(a_ref, w_ref, o_ref):
    o_ref[...] = _dot(a_ref[...].astype(BF16), w_ref[...]).astype(o_ref.dtype)


def _matmul(a, w, out_dtype, tm, tn, name):
    m, k = a.shape
    n = w.shape[1]
    return pl.pallas_call(
        _mm_kernel,
        out_shape=jax.ShapeDtypeStruct((m, n), out_dtype),
        grid=(m // tm, n // tn),
        in_specs=[pl.BlockSpec((tm, k), lambda i, j: (i, 0)),
                  pl.BlockSpec((k, tn), lambda i, j: (0, j))],
        out_specs=pl.BlockSpec((tm, tn), lambda i, j: (i, j)),
        compiler_params=_cparams(("parallel", "arbitrary")),
        name=name,
    )(a, w)


MERGE_CHUNK = 512


def _merge_kernel(oa_ref, ob_ref, wa_ref, wb_ref, ma_ref, mb_ref, o_ref):
    oa = oa_ref[...]
    ob = ob_ref[...]
    for c in range(o_ref.shape[1] // MERGE_CHUNK):
        cols = slice(c * MERGE_CHUNK, (c + 1) * MERGE_CHUNK)
        ya = _dot(oa, wa_ref[:, cols].astype(BF16))
        yb = _dot(ob, wb_ref[:, cols].astype(BF16))
        o_ref[:, cols] = (_sigmoid(ma_ref[:, cols]) * ya
                          + _sigmoid(mb_ref[:, cols]) * yb).astype(o_ref.dtype)


def _resident_slab(w_all, layer):
    return pl.BlockSpec((None,) + tuple(w_all.shape[1:]), lambda i: (layer, 0, 0),
                        pipeline_mode=pl.Buffered(1))


def _merge(o_a, o_b, w_a, w_b, layer, h_main, tm):
    m, k = o_a.shape
    n = w_a.shape[2]
    ca, cb = COL_MERGE_A // n, COL_MERGE_B // n
    return pl.pallas_call(
        _merge_kernel,
        out_shape=jax.ShapeDtypeStruct((m, n), BF16),
        grid=(m // tm,),
        in_specs=[pl.BlockSpec((tm, k), lambda i: (i, 0)),
                  pl.BlockSpec((tm, k), lambda i: (i, 0)),
                  _resident_slab(w_a, layer), _resident_slab(w_b, layer),
                  pl.BlockSpec((tm, n), lambda i: (i, ca)),
                  pl.BlockSpec((tm, n), lambda i: (i, cb))],
        out_specs=pl.BlockSpec((tm, n), lambda i: (i, 0)),
        compiler_params=_cparams(("parallel",)),
        name="merge",
    )(o_a, o_b, w_a, w_b, h_main, h_main)


def _outproj_ln_kernel(mx_ref, w_ref, x_ref, g_ref, b_ref, o_ref, ob_ref):
    half = mx_ref.shape[0] // 2
    w = w_ref[...].astype(BF16)
    for r in (slice(0, half), slice(half, 2 * half)):
        y = ALPHA * x_ref[r, :] + _dot(mx_ref[r, :], w)
        out = _layer_norm(y, g_ref[...], b_ref[...])
        o_ref[r, :] = out
        ob_ref[r, :] = out.astype(BF16)


def _outproj_ln(mixed, w_out, layer, x, g, b, tm):
    m, d = x.shape
    return pl.pallas_call(
        _outproj_ln_kernel,
        out_shape=(jax.ShapeDtypeStruct((m, d), F32), jax.ShapeDtypeStruct((m, d), BF16)),
        grid=(m // tm,),
        in_specs=[pl.BlockSpec((tm, d), lambda i: (i, 0)),
                  _resident_slab(w_out, layer),
                  pl.BlockSpec((tm, d), lambda i: (i, 0)),
                  pl.BlockSpec((1, d), lambda i: (0, 0)),
                  pl.BlockSpec((1, d), lambda i: (0, 0))],
        out_specs=(pl.BlockSpec((tm, d), lambda i: (i, 0)),
                   pl.BlockSpec((tm, d), lambda i: (i, 0))),
        compiler_params=_cparams(("parallel",)),
        name="outproj_ln",
    )(mixed, w_out, x, g.reshape(1, d), b.reshape(1, d))


def _ffn_act_kernel(x_ref, wg_ref, wu_ref, o_ref):
    xv = x_ref[...]
    gate = _dot(xv, wg_ref[...].astype(BF16))
    up = _dot(xv, wu_ref[...].astype(BF16))
    o_ref[...] = (_silu(gate) * up).astype(o_ref.dtype)


def _ffn_act(xb, w_gate, w_up, layer, tm, tn):
    m, k = xb.shape
    n = w_gate.shape[2]
    return pl.pallas_call(
        _ffn_act_kernel,
        out_shape=jax.ShapeDtypeStruct((m, n), BF16),
        grid=(m // tm, n // tn),
        in_specs=[pl.BlockSpec((tm, k), lambda i, j: (i, 0)),
                  pl.BlockSpec((None, k, tn), lambda i, j: (layer, 0, j)),
                  pl.BlockSpec((None, k, tn), lambda i, j: (layer, 0, j))],
        out_specs=pl.BlockSpec((tm, tn), lambda i, j: (i, j)),
        compiler_params=_cparams(("parallel", "arbitrary")),
        name="ffn_act",
    )(xb, w_gate, w_up)


def _resid_kernel(x_ref, xb_ref, p_ref, wp_ref, wpg_ref, o_ref):
    xb = xb_ref[...]
    pv = p_ref[...]
    for c in range(o_ref.shape[1] // MERGE_CHUNK):
        cols = slice(c * MERGE_CHUNK, (c + 1) * MERGE_CHUNK)
        ple = (_dot(pv, wp_ref[:, cols].astype(BF16))
               * _sigmoid(_dot(xb, wpg_ref[:, cols].astype(BF16))))
        o_ref[:, cols] = ALPHA * x_ref[:, cols] + ple


def _resid(x1, x1b, pb, w_ple, w_ple_gate, layer, tm):
    m, d = x1.shape
    kp = pb.shape[1]
    return pl.pallas_call(
        _resid_kernel,
        out_shape=jax.ShapeDtypeStruct((m, d), F32),
        grid=(m // tm,),
        in_specs=[pl.BlockSpec((tm, d), lambda i: (i, 0)),
                  pl.BlockSpec((tm, d), lambda i: (i, 0)),
                  pl.BlockSpec((tm, kp), lambda i: (i, 0)),
                  _resident_slab(w_ple, layer), _resident_slab(w_ple_gate, layer)],
        out_specs=pl.BlockSpec((tm, d), lambda i: (i, 0)),
        compiler_params=_cparams(("parallel",)),
        name="ple_resid",
    )(x1, x1b, pb, w_ple, w_ple_gate)


def _ffn_out_kernel(act_ref, w_ref, r_ref, g_ref, b_ref, o_ref, ob_ref):
    half = act_ref.shape[0] // 2
    for r in (slice(0, half), slice(half, 2 * half)):
        y = r_ref[r, :] + _dot(act_ref[r, :], w_ref[...])
        out = _layer_norm(y, g_ref[...], b_ref[...])
        o_ref[r, :] = out
        ob_ref[r, :] = out.astype(BF16)


def _ffn_out(act, w_down, resid, g, b, tm):
    m, kf = act.shape
    d = w_down.shape[1]
    return pl.pallas_call(
        _ffn_out_kernel,
        out_shape=(jax.ShapeDtypeStruct((m, d), F32), jax.ShapeDtypeStruct((m, d), BF16)),
        grid=(m // tm,),
        in_specs=[pl.BlockSpec((tm, kf), lambda i: (i, 0)),
                  pl.BlockSpec((kf, d), lambda i: (0, 0), pipeline_mode=pl.Buffered(1)),
                  pl.BlockSpec((tm, d), lambda i: (i, 0)),
                  pl.BlockSpec((1, d), lambda i: (0, 0)),
                  pl.BlockSpec((1, d), lambda i: (0, 0))],
        out_specs=(pl.BlockSpec((tm, d), lambda i: (i, 0)),
                   pl.BlockSpec((tm, d), lambda i: (i, 0))),
        compiler_params=_cparams(("parallel",)),
        name="ffn_out_ln",
    )(act, w_down, resid, g.reshape(1, d), b.reshape(1, d))


DN_TS = 256
DN_HB = 8


def _dn_kernel(alog_ref, dtb_ref,
               q_ref, k_ref, v_ref, z_ref, qp_ref, kp_ref, vp_ref, hs_ref,
               cwq_ref, cwk_ref, cwv_ref, nw_ref, o_ref, state_ref, cbuf_ref):
    hg = pl.program_id(1)
    s = pl.program_id(2)
    c = DN_CHUNK
    d = HEAD_DIM

    @pl.when(s == 0)
    def _():
        state_ref[...] = jnp.zeros_like(state_ref)

    conv_slots = []

    def conv_silu(x_ref, xp_ref, cw_ref, hb):
        cols = slice(hb * d, (hb + 1) * d)
        buf = cbuf_ref.at[len(conv_slots)]
        conv_slots.append(None)
        buf[0:SUBLANES, :] = jnp.where(s == 0, 0.0, xp_ref[:, cols])
        buf[SUBLANES:, :] = x_ref[:, cols]
        cw = cw_ref[:, cols]
        y = None
        for i in range(DN_CONV):
            off = SUBLANES - (DN_CONV - 1) + i
            term = buf[off:off + DN_TS, :] * cw[i:i + 1]
            y = term if y is None else y + term
        return _silu(y)

    def l2norm(x):
        return x * lax.rsqrt(jnp.sum(x * x, axis=-1, keepdims=True) + NORM_EPS)

    row = lax.broadcasted_iota(jnp.int32, (c, c), 0)
    col = lax.broadcasted_iota(jnp.int32, (c, c), 1)
    causal = row >= col
    strict = row > col
    tri_b = jnp.where(causal, 1.0, 0.0).astype(BF16)
    eye = jnp.where(row == col, 1.0, 0.0).astype(F32)
    ones8_b = jnp.ones((SUBLANES, c), BF16)
    hs = hs_ref[...]
    nw = nw_ref[...]

    nck = DN_TS // c
    pairs = [(hb, ci) for hb in range(DN_HB) for ci in range(nck)]
    qs, ks, vs, betas, gbs = [], [], [], [], []
    for hb in range(DN_HB):
        h = hg * DN_HB + hb
        q_all = l2norm(conv_silu(q_ref, qp_ref, cwq_ref, hb)) * (d ** -0.5)
        k_all = l2norm(conv_silu(k_ref, kp_ref, cwk_ref, hb))
        v_all = conv_silu(v_ref, vp_ref, cwv_ref, hb)
        beta_all = _sigmoid(_lane_col(hs, SC_BETA + h))
        a_all = _lane_col(hs, SC_DECAY + h) + dtb_ref[h]
        softplus = jnp.maximum(a_all, 0.0) + jnp.log(1.0 + jnp.exp(-jnp.abs(a_all)))
        g_all = -jnp.exp(jnp.zeros_like(a_all) + alog_ref[h]) * softplus
        for ci in range(nck):
            sl = slice(ci * c, (ci + 1) * c)
            qs.append(q_all[sl])
            ks.append(k_all[sl])
            vs.append(v_all[sl])
            betas.append(beta_all[sl])
            gbs.append(jnp.broadcast_to(g_all[sl], (c, LANES)))
    n = len(pairs)
    gcs = [_dot_01(tri_b, gbs[i]) for i in range(n)]
    gc_rows = [_dot_01(ones8_b, jnp.where(row <= col, gbs[i][:, :c], 0.0))[0:1] for i in range(n)]
    decays = [jnp.exp(jnp.where(causal, gcs[i][:, :c] - gc_rows[i], NEG_INF)) for i in range(n)]
    kbs = [ks[i] * betas[i] for i in range(n)]
    kbfs = [ks[i].astype(BF16) for i in range(n)]
    negs = [jnp.where(strict, -(_dot_nt(kbs[i].astype(BF16), kbfs[i]) * decays[i]), 0.0)
            for i in range(n)]
    accs = [eye + negs[i] for i in range(n)]
    nbf = [negs[i].astype(BF16) for i in range(n)]
    pws = [_dot(nbf[i], nbf[i]) for i in range(n)]
    for _ in range(4):
        pbs = [pws[i].astype(BF16) for i in range(n)]
        accs = [accs[i] + _dot(pbs[i], accs[i].astype(BF16)) for i in range(n)]
        pws = [_dot(pbs[i], pbs[i]) for i in range(n)]
    tinvs = [(accs[i] + _dot(pws[i].astype(BF16), accs[i].astype(BF16))).astype(BF16)
             for i in range(n)]
    egs = [jnp.exp(gcs[i]) for i in range(n)]
    uws = [_dot(tinvs[i], jnp.concatenate([vs[i] * betas[i], kbs[i] * egs[i]], axis=1).astype(BF16))
           .astype(BF16) for i in range(n)]
    qks = [jnp.where(causal, _dot_nt(qs[i].astype(BF16), kbfs[i]) * decays[i], 0.0).astype(BF16)
           for i in range(n)]
    k_decs = [(ks[i] * jnp.exp(gcs[i][c - 1:c] - gcs[i])).astype(BF16) for i in range(n)]
    nbs = [_dot_tn(k_decs[i], uws[i]) for i in range(n)]
    prs = [_dot(qks[i], uws[i]) for i in range(n)]
    lhss = [jnp.concatenate([(qs[i] * egs[i] - prs[i][:, d:]).astype(BF16),
                             nbs[i][:, d:].astype(BF16)], axis=0) for i in range(n)]

    states = [state_ref[hb] for hb in range(DN_HB)]
    for ci in range(nck):
        sl = slice(ci * c, (ci + 1) * c)
        ress = [_dot(lhss[hb * nck + ci], states[hb].astype(BF16)) for hb in range(DN_HB)]
        for hb in range(DN_HB):
            i = hb * nck + ci
            cols = slice(hb * d, (hb + 1) * d)
            o = ress[hb][:c] + prs[i][:, :d]
            states[hb] = states[hb] * egs[i][c - 1:c] - ress[hb][c:] + nbs[i][:, :d]
            o = o * lax.rsqrt(jnp.mean(o * o, axis=-1, keepdims=True) + NORM_EPS) * nw
            o_ref[sl, cols] = (o * _silu(z_ref[sl, cols])).astype(o_ref.dtype)
    for hb in range(DN_HB):
        state_ref[hb] = states[hb]


def _deltanet(h_main, h_small, conv_w, a_log, dt_bias, norm_w):
    b_, s_len, _ = h_main.shape
    ts = DN_TS
    wd = DN_HB * HEAD_DIM
    ngrp = HEADS // DN_HB
    blk = lambda cb: pl.BlockSpec((None, ts, wd), lambda b, h, s: (b, s, cb // DN_HB + h))
    prev = lambda cb: pl.BlockSpec(
        (None, SUBLANES, wd),
        lambda b, h, s: (b, jnp.maximum(s * (ts // SUBLANES) - 1, 0), cb // DN_HB + h))
    cw = lambda cb: pl.BlockSpec((DN_CONV, wd), lambda b, h, s: (0, cb // DN_HB + h))
    smem = pl.BlockSpec(memory_space=pltpu.SMEM)
    return pl.pallas_call(
        _dn_kernel,
        out_shape=jax.ShapeDtypeStruct((b_, s_len, HEADS * HEAD_DIM), BF16),
        grid=(b_, ngrp, s_len // ts),
        in_specs=[smem, smem,
                  blk(CB_DN_Q), blk(CB_DN_K), blk(CB_DN_V), blk(CB_DN_Z),
                  prev(CB_DN_Q), prev(CB_DN_K), prev(CB_DN_V),
                  pl.BlockSpec((None, ts, LANES), lambda b, h, s: (b, s, 0)),
                  cw(CB_DN_Q), cw(CB_DN_K), cw(CB_DN_V),
                  pl.BlockSpec((1, LANES), lambda b, h, s: (0, 0))],
        out_specs=pl.BlockSpec((None, ts, wd), lambda b, h, s: (b, s, h)),
        scratch_shapes=[pltpu.VMEM((DN_HB, HEAD_DIM, HEAD_DIM), F32),
                        pltpu.VMEM((3 * DN_HB, SUBLANES + DN_TS, HEAD_DIM), F32)],
        compiler_params=_cparams(("parallel", "parallel", "arbitrary")),
        name="deltanet",
    )(a_log, dt_bias, h_main, h_main, h_main, h_main, h_main, h_main, h_main, h_small,
      conv_w, conv_w, conv_w, norm_w.reshape(1, LANES))


def _rope(x, cosf, sinf):
    return x * cosf + pltpu.roll(x, HEAD_DIM // 2, axis=1) * sinf


def _nsa_prep_kernel(kc_ref, vc_ref, ks_ref, vs_ref, kw_ref, vw_ref, cos_ref, sin_ref,
                     pek_ref, w1k_ref, w2k_ref, pev_ref, w1v_ref, w2v_ref,
                     kso_ref, vso_ref, kwo_ref, vwo_ref, kco_ref, vco_ref, buf_ref):
    cosf = cos_ref[...]
    sinf = sin_ref[...]
    s_len = ks_ref.shape[0]
    pos = lax.broadcasted_iota(jnp.int32, (s_len, LANES), 0)
    lane = lax.broadcasted_iota(jnp.int32, (s_len, LANES), 1)
    kso_ref[:, :HEAD_DIM] = _rope(ks_ref[...], cosf, sinf).astype(BF16)
    kso_ref[:, HEAD_DIM:] = jnp.where(pos // SEL_BLOCK == lane, NEG_INF, 0.0).astype(BF16)
    kwo_ref[...] = _rope(kw_ref[...], cosf, sinf).astype(BF16)
    ones = jnp.ones((s_len, HEAD_DIM), BF16)
    vso_ref[:, :HEAD_DIM] = vs_ref[...].astype(BF16)
    vso_ref[:, HEAD_DIM:] = ones
    vwo_ref[:, :HEAD_DIM] = vw_ref[...].astype(BF16)
    vwo_ref[:, HEAD_DIM:] = ones
    nch = buf_ref.shape[0] // CMP_STRIDE

    def compress(pe_ref, w1_ref, w2_ref, out_ref):
        a0 = jnp.zeros((nch, CMP_HIDDEN), F32)
        a1 = jnp.zeros((nch, CMP_HIDDEN), F32)
        for i in range(CMP_STRIDE):
            xi = buf_ref[pl.ds(i, nch, stride=CMP_STRIDE), :]
            lo = (xi + pe_ref[i:i + 1, :]).astype(BF16)
            hi = (xi + pe_ref[CMP_STRIDE + i:CMP_STRIDE + i + 1, :]).astype(BF16)
            a0 = a0 + _dot(lo, w1_ref[i * HEAD_DIM:(i + 1) * HEAD_DIM, :])
            a1 = a1 + _dot(hi, w1_ref[(CMP_STRIDE + i) * HEAD_DIM:(CMP_STRIDE + i + 1) * HEAD_DIM, :])
        hid = a0 + pltpu.roll(a1, nch - 1, axis=0)
        out_ref[...] = _dot(_silu(hid).astype(BF16), w2_ref[...]).astype(out_ref.dtype)

    buf_ref[...] = _rope(kc_ref[...], cosf, sinf)
    compress(pek_ref, w1k_ref, w2k_ref, kco_ref)
    buf_ref[...] = vc_ref[...]
    compress(pev_ref, w1v_ref, w2v_ref, vco_ref)


def _nsa_prep(h_main, cosf, sinf, pe_k, w1_k, w2_k, pe_v, w1_v, w2_v):
    b_, s_len, _ = h_main.shape
    nch = s_len // CMP_STRIDE
    kv = lambda i: pl.BlockSpec((None, s_len, LANES), lambda b, g: (b, 0, CB_KV + GROUPS * i + g))
    full = lambda shape: pl.BlockSpec(shape, lambda b, g: tuple(0 for _ in shape))
    assert s_len // SEL_BLOCK <= LANES
    seq_out = pl.BlockSpec((None, None, s_len, LANES), lambda b, g: (b, g, 0, 0))
    aug_out = pl.BlockSpec((None, None, s_len, 2 * LANES), lambda b, g: (b, g, 0, 0))
    cmp_out = pl.BlockSpec((None, None, nch, LANES), lambda b, g: (b, g, 0, 0))
    seq_shape = jax.ShapeDtypeStruct((b_, GROUPS, s_len, HEAD_DIM), BF16)
    aug_shape = jax.ShapeDtypeStruct((b_, GROUPS, s_len, 2 * HEAD_DIM), BF16)
    cmp_shape = jax.ShapeDtypeStruct((b_, GROUPS, nch, HEAD_DIM), BF16)
    return pl.pallas_call(
        _nsa_prep_kernel,
        out_shape=(aug_shape, aug_shape, seq_shape, aug_shape, cmp_shape, cmp_shape),
        grid=(b_, GROUPS),
        in_specs=[kv(0), kv(1), kv(2), kv(3), kv(4), kv(5),
                  full((s_len, LANES)), full((s_len, LANES)),
                  full((CMP_LEN, HEAD_DIM)), full((CMP_LEN * HEAD_DIM, CMP_HIDDEN)),
                  full((CMP_HIDDEN, HEAD_DIM)),
                  full((CMP_LEN, HEAD_DIM)), full((CMP_LEN * HEAD_DIM, CMP_HIDDEN)),
                  full((CMP_HIDDEN, HEAD_DIM))],
        out_specs=(aug_out, aug_out, seq_out, aug_out, cmp_out, cmp_out),
        scratch_shapes=[pltpu.VMEM((s_len, HEAD_DIM), F32)],
        compiler_params=_cparams(("parallel", "parallel")),
        name="nsa_prep",
    )(h_main, h_main, h_main, h_main, h_main, h_main, cosf, sinf,
      pe_k, w1_k, w2_k, pe_v, w1_v, w2_v)


NSA_TQ = 256
NSA_TK = 256


def _nsa_attn_kernel(q_ref, hs_ref, cos_ref, sin_ref, kc_ref, vc_ref, ks_ref, vs_ref,
                     kw_ref, vw_ref, ovt_ref, o_ref, os_ref):
    g = pl.program_id(1)
    qi = pl.program_id(2)
    tq, tk = NSA_TQ, NSA_TK
    rows = HPG * tq
    ns = ks_ref.shape[0] // SEL_BLOCK
    cosf = cos_ref[...]
    sinf = sin_ref[...]
    scale = HEAD_DIM ** -0.5
    qs = jnp.concatenate(
        [_rope(q_ref[:, hh * HEAD_DIM:(hh + 1) * HEAD_DIM], cosf, sinf) * scale
         for hh in range(HPG)], axis=0).astype(BF16)

    t_abs = qi * tq + lax.broadcasted_iota(jnp.int32, (tq, LANES), 0)
    lane = lax.broadcasted_iota(jnp.int32, (tq, LANES), 1)

    t_abs_k = qi * tq + lax.broadcasted_iota(jnp.int32, (tq, tk), 0)
    lane_k = lax.broadcasted_iota(jnp.int32, (tq, tk), 1)

    def add_bias(s_blk, bias):
        return (s_blk.reshape(HPG, tq, tk) + bias[None]).reshape(rows, tk)

    def split_rows(dot_fn, lhs, rhs):
        half = lhs.shape[0] // 2
        return jnp.concatenate([dot_fn(lhs[:half], rhs), dot_fn(lhs[half:], rhs)], axis=0)

    def fold(x):
        return x[:, :LANES], x[:, LANES:]

    neg_rows = jnp.full((rows, LANES), NEG_INF, F32)

    def running_max(score_blocks):
        mrun = neg_rows
        for s_blk in score_blocks:
            s0, s1 = fold(s_blk)
            mrun = jnp.maximum(mrun, jnp.maximum(s0, s1))
        return jnp.broadcast_to(jnp.max(mrun, axis=-1, keepdims=True), (rows, LANES))

    def probs(s_blk, m_b):
        s0, s1 = fold(s_blk)
        return jnp.exp(jnp.concatenate([(s0 - m_b).astype(BF16), (s1 - m_b).astype(BF16)], axis=1))

    def weighted_values(ps, value_rows):
        acc = _dot(jnp.concatenate(ps, axis=1), value_rows)
        return acc[:, :HEAD_DIM] / acc[:, HEAD_DIM:]

    kb_last = (qi * tq) // tk
    nwin = WIN // tk + 1
    win_start = pl.multiple_of(jnp.maximum(kb_last - (nwin - 1), 0) * tk, tk)
    win_raw = split_rows(_dot_nt, qs, kw_ref[pl.ds(win_start, nwin * tk), :])

    mask_c = jnp.concatenate([lane * CMP_STRIDE + (CMP_LEN - 1) <= t_abs] * HPG, axis=0)
    s_c = split_rows(_dot_nt, qs, kc_ref[...])
    m_c = jnp.max(jnp.where(mask_c, s_c, NEG_INF), axis=-1, keepdims=True)
    e_c = jnp.where(mask_c, jnp.exp(s_c - m_c), 0.0)
    l_c = jnp.sum(e_c, axis=-1, keepdims=True)
    p_c = jnp.where(l_c > 0.0, e_c / l_c, 0.0)

    win_s = []
    for d in range(nwin):
        diff = t_abs_k - (win_start + d * tk + lane_k)
        bias = jnp.where((diff >= 0) & (diff < WIN), 0.0, NEG_INF)
        win_s.append(add_bias(win_raw[:, d * tk:(d + 1) * tk], bias))
    win_m = running_max(win_s)

    o_c = split_rows(_dot, p_c.astype(BF16), vc_ref[...])
    psum = p_c[0:tq]
    for hh in range(1, HPG):
        psum = psum + p_c[hh * tq:(hh + 1) * tq]
    p_hi, p_lo = _split2(psum)
    ovt = ovt_ref[...]
    imp = (_dot_nt(ovt, p_hi) + _dot_nt(ovt, p_lo))[:ns]
    blk = lax.broadcasted_iota(jnp.int32, (ns, tq), 0)
    cur = (qi * tq + lax.broadcasted_iota(jnp.int32, (ns, tq), 1)) // SEL_BLOCK
    forced = (blk == 0) | (blk == cur) | (blk == cur - 1)
    imp = jnp.where(forced, jnp.inf, jnp.where(blk <= cur, imp, -jnp.inf))

    win_p = [probs(s_blk, win_m) for s_blk in win_s]

    rank = jnp.zeros((ns, tq), F32)
    for i in range(ns):
        ci = imp[i:i + 1, :]
        before = (ci > imp) | ((ci == imp) & (blk > i))
        rank = rank + jnp.where(before, 1.0, 0.0)
    unsel_t = jnp.where(rank < float(min(SEL_TOPK, ns)), 0.0, 1.0)
    unsel = jnp.concatenate([unsel_t, jnp.zeros((LANES - ns, tq), F32)], axis=0).T
    q_aug = jnp.concatenate([qs, jnp.concatenate([unsel.astype(BF16)] * HPG, axis=0)], axis=1)

    o_w = weighted_values(win_p, vw_ref[pl.ds(win_start, nwin * tk), :])

    def sel_variant(n_full):
        def run():
            n_keys = (n_full + 1) * tk
            s_all = split_rows(_dot_nt, q_aug, ks_ref[0:n_keys, :])
            s_blks = [s_all[:, j * tk:(j + 1) * tk] for j in range(n_full + 1)]
            bias = jnp.where(n_full * tk + lane_k <= t_abs_k, 0.0, NEG_INF)
            s_blks[n_full] = add_bias(s_blks[n_full], bias)
            m_b = running_max(s_blks)
            ps = [probs(s_blk, m_b) for s_blk in s_blks]
            os_ref[...] = weighted_values(ps, vs_ref[0:n_keys, :])
        return run

    for n_full in range(ks_ref.shape[0] // tk):
        pl.when(kb_last == n_full)(sel_variant(n_full))
    o_s = os_ref[...]

    hs = hs_ref[...]
    for hh in range(HPG):
        gbase = SC_GATE + (g * HPG + hh) * 3
        r = slice(hh * tq, (hh + 1) * tq)
        out = (_sigmoid(_lane_col(hs, gbase)) * o_c[r]
               + _sigmoid(_lane_col(hs, gbase + 1)) * o_s[r]
               + _sigmoid(_lane_col(hs, gbase + 2)) * o_w[r])
        o_ref[:, hh * HEAD_DIM:(hh + 1) * HEAD_DIM] = out.astype(o_ref.dtype)


def _nsa_attn(h_main, h_small, cosf, sinf, kc, vc, ks, vs, kw, vw, overlap_t):
    b_, s_len, _ = h_main.shape
    tq = NSA_TQ
    nch = kc.shape[2]
    assert nch == LANES, "the compressed-block axis is laid out on one vreg of lanes"
    assert s_len % NSA_TK == 0 and NSA_TK == 2 * LANES
    qw = HPG * HEAD_DIM
    seq = pl.BlockSpec((None, None, s_len, HEAD_DIM), lambda b, g, i: (b, g, 0, 0))
    aug = pl.BlockSpec((None, None, s_len, 2 * HEAD_DIM), lambda b, g, i: (b, g, 0, 0))
    cmp_ = pl.BlockSpec((None, None, nch, HEAD_DIM), lambda b, g, i: (b, g, 0, 0))
    return pl.pallas_call(
        _nsa_attn_kernel,
        out_shape=jax.ShapeDtypeStruct((b_, s_len, HEADS * HEAD_DIM), BF16),
        grid=(b_, GROUPS, s_len // tq),
        in_specs=[pl.BlockSpec((None, tq, qw), lambda b, g, i: (b, i, CB_NSA_Q * LANES // qw + g)),
                  pl.BlockSpec((None, tq, LANES), lambda b, g, i: (b, i, 0)),
                  pl.BlockSpec((tq, LANES), lambda b, g, i: (i, 0)),
                  pl.BlockSpec((tq, LANES), lambda b, g, i: (i, 0)),
                  cmp_, cmp_, aug, aug, seq, aug,
                  pl.BlockSpec(overlap_t.shape, lambda b, g, i: (0, 0))],
        out_specs=pl.BlockSpec((None, tq, qw), lambda b, g, i: (b, i, g)),
        scratch_shapes=[pltpu.VMEM((HPG * tq, HEAD_DIM), F32)],
        compiler_params=_cparams(("parallel", "parallel", "arbitrary")),
        name="nsa_attn",
    )(h_main, h_small, cosf, sinf, kc, vc, ks, vs, kw, vw, overlap_t)


def _nsa_constants(s_len):
    half = HEAD_DIM // 2
    inv_freq = ROPE_THETA ** (-jnp.arange(half, dtype=F32) / half)
    ang = jnp.arange(s_len, dtype=F32)[:, None] * inv_freq[None, :]
    cos, sin = jnp.cos(ang), jnp.sin(ang)
    cosf = jnp.concatenate([cos, cos], axis=-1)
    sinf = jnp.concatenate([-sin, sin], axis=-1)
    nch = s_len // CMP_STRIDE
    ns = s_len // SEL_BLOCK
    n = jnp.arange(nch)[:, None] * CMP_STRIDE
    j = jnp.arange(LANES)[None, :] * SEL_BLOCK
    overlap = ((n <= j + SEL_BLOCK - 1) & (n + CMP_LEN - 1 >= j)
               & (jnp.arange(nch)[:, None] < nch - CMP_LEN // CMP_STRIDE + 1)
               & (jnp.arange(LANES)[None, :] < ns)).astype(BF16)
    return cosf, sinf, overlap.T


def _layer(x, xb, p_i, w_in, conv_w, a_log, dt_bias, norm_w, pe_k, w1_k, w2_k, pe_v, w1_v, w2_v,
           w_a_all, w_b_all, w_out_all, ln1_g, ln1_b, w_gate_all, w_up_all, layer, w_down,
           w_ple_all, w_ple_gate_all, ln2_g, ln2_b, consts):
    b_, s_len, d = x.shape
    t = b_ * s_len
    cosf, sinf, overlap_t = consts
    x2 = x.reshape(t, d)
    xb2 = xb.reshape(t, d)

    w_main = jnp.concatenate([w_in[:, :IN_DN_SMALL], w_in[:, IN_MERGE:], w_in[:, IN_NSA:IN_GATE]],
                             axis=1).astype(BF16)
    n_small = IN_NSA - IN_DN_SMALL + IN_MERGE - IN_GATE
    w_small = jnp.concatenate([w_in[:, IN_DN_SMALL:IN_NSA], w_in[:, IN_GATE:IN_MERGE],
                               jnp.zeros((d, LANES - n_small), w_in.dtype)], axis=1).astype(BF16)
    tm = min(1024, t)
    h_main = _matmul(xb2, w_main, F32, tm, 1536, "proj_main")
    h_small = _matmul(xb2, w_small, F32, tm, LANES, "proj_small")
    h_main3 = h_main.reshape(b_, s_len, N_MAIN)
    h_small3 = h_small.reshape(b_, s_len, LANES)

    o_a = _deltanet(h_main3, h_small3, conv_w, a_log, dt_bias, norm_w)
    ks, vs, kw, vw, kc, vc = _nsa_prep(h_main3, cosf, sinf, pe_k, w1_k.astype(BF16),
                                       w2_k.astype(BF16), pe_v, w1_v.astype(BF16),
                                       w2_v.astype(BF16))
    o_b = _nsa_attn(h_main3, h_small3, cosf, sinf, kc, vc, ks, vs, kw, vw, overlap_t)

    mixed = _merge(o_a.reshape(t, -1), o_b.reshape(t, -1), w_a_all, w_b_all, layer,
                   h_main, min(512, t))
    x1, x1b = _outproj_ln(mixed, w_out_all, layer, x2, ln1_g, ln1_b, min(512, t))
    act = _ffn_act(x1b, w_gate_all, w_up_all, layer, tm, 512)
    resid = _resid(x1, x1b, p_i.reshape(t, PLE_DIM).astype(BF16), w_ple_all, w_ple_gate_all,
                   layer, min(512, t))
    y, yb = _ffn_out(act, w_down.astype(BF16), resid, ln2_g, ln2_b, min(256, t))
    return y.reshape(b_, s_len, d), yb.reshape(b_, s_len, d)


def kernel(x, p, w_in, dn_conv_w, dn_a_log, dn_dt_bias, dn_norm_w, cmp_pe_k, cmp_w1_k, cmp_w2_k, cmp_pe_v, cmp_w1_v, cmp_w2_v, w_branch_a, w_branch_b, w_out, ln1_g, ln1_b, w_ffn_gate, w_ffn_up, w_ffn_down, w_ple, w_ple_gate, ln2_g, ln2_b):
    consts = _nsa_constants(x.shape[1])
    xb = x
    for i in range(DEPTH):
        x, xb = _layer(x, xb, p[i], w_in[i], dn_conv_w[i], dn_a_log[i], dn_dt_bias[i], dn_norm_w[i],
                       cmp_pe_k[i], cmp_w1_k[i], cmp_w2_k[i], cmp_pe_v[i], cmp_w1_v[i], cmp_w2_v[i],
                       w_branch_a, w_branch_b, w_out, ln1_g[i], ln1_b[i],
                       w_ffn_gate, w_ffn_up, i, w_ffn_down[i], w_ple, w_ple_gate,
                       ln2_g[i], ln2_b[i], consts)
    return x
```

```python
import jax
import jax.numpy as jnp
from jax import lax
from jax.experimental import pallas as pl
from jax.experimental.pallas import tpu as pltpu

D_MODEL = 2048
DEPTH = 2
HEAD_DIM = 128
HEADS = 8
DN_CONV = 4
DN_CHUNK = 64
GROUPS = 2
HPG = HEADS // GROUPS
CMP_LEN = 32
CMP_STRIDE = 16
CMP_HIDDEN = 256
SEL_BLOCK = 64
SEL_TOPK = 16
WIN = 512
ROPE_THETA = 10000.0
PLE_DIM = 256
ALPHA = (2.0 * DEPTH) ** 0.25
LN_EPS = 1e-5
NORM_EPS = 1e-6
NEG_INF = -1e30

LANES = 128
SUBLANES = 8
VMEM_LIMIT = 60 * 1024 * 1024

W_DN = 4 * HEADS * HEAD_DIM
W_NSA = (HEADS + 6 * GROUPS) * HEAD_DIM
IN_DN_SMALL = W_DN
IN_NSA = IN_DN_SMALL + 2 * HEADS
IN_GATE = IN_NSA + W_NSA
IN_MERGE = IN_GATE + 3 * HEADS
CB_DN_Q, CB_DN_K, CB_DN_V, CB_DN_Z = 0, HEADS, 2 * HEADS, 3 * HEADS
COL_MERGE_A = W_DN
COL_MERGE_B = W_DN + D_MODEL
CB_NSA_Q = (W_DN + 2 * D_MODEL) // LANES
CB_KV = CB_NSA_Q + HEADS
N_MAIN = W_DN + 2 * D_MODEL + W_NSA
SC_BETA, SC_DECAY, SC_GATE = 0, HEADS, 2 * HEADS

F32 = jnp.float32
BF16 = jnp.bfloat16


def _cparams(sem):
    return pltpu.CompilerParams(dimension_semantics=sem, vmem_limit_bytes=VMEM_LIMIT)


def _dot(a, b):
    return jnp.dot(a, b, preferred_element_type=F32)


def _dot_nt(a, b):
    return lax.dot_general(a, b, (((1,), (1,)), ((), ())), preferred_element_type=F32)


def _dot_tn(a, b):
    return lax.dot_general(a, b, (((0,), (0,)), ((), ())), preferred_element_type=F32)


def _sigmoid(x):
    return 0.5 * jnp.tanh(0.5 * x) + 0.5


def _silu(x):
    return x * _sigmoid(x)


def _layer_norm(y, g, b):
    mu = jnp.mean(y, axis=-1, keepdims=True)
    d = y - mu
    var = jnp.mean(d * d, axis=-1, keepdims=True)
    return d * lax.rsqrt(var + LN_EPS) * g + b


def _lane_col(x, idx):
    lane = lax.broadcasted_iota(jnp.int32, x.shape, 1)
    return jnp.sum(jnp.where(lane == idx, x, 0.0), axis=1, keepdims=True)


def _split2(x):
    hi = x.astype(BF16)
    return hi, (x - hi.astype(F32)).astype(BF16)


def _split3(x):
    x1 = x.astype(BF16)
    r = x - x1.astype(F32)
    x2 = r.astype(BF16)
    return x1, x2, (r - x2.astype(F32)).astype(BF16)


def _dot_01(ones_b, x):
    x1, x2, x3 = _split3(x)
    return _dot(ones_b, x1) + _dot(ones_b, x2) + _dot(ones_b, x3)


def _mm_kernel(a_ref, w_ref, o_ref):
    o_ref[...] = _dot(a_ref[...].astype(BF16), w_ref[...]).astype(o_ref.dtype)


def _matmul(a, w, out_dtype, tm, tn, name):
    m, k = a.shape
    n = w.shape[1]
    return pl.pallas_call(
        _mm_kernel,
        out_shape=jax.ShapeDtypeStruct((m, n), out_dtype),
        grid=(m // tm, n // tn),
        in_specs=[pl.BlockSpec((tm, k), lambda i, j: (i, 0)),
                  pl.BlockSpec((k, tn), lambda i, j: (0, j))],
        out_specs=pl.BlockSpec((tm, tn), lambda i, j: (i, j)),
        compiler_params=_cparams(("parallel", "arbitrary")),
        name=name,
    )(a, w)


MERGE_CHUNK = 512


def _merge_kernel(oa_ref, ob_ref, wa_ref, wb_ref, ma_ref, mb_ref, o_ref):
    oa = oa_ref[...]
    ob = ob_ref[...]
    for c in range(o_ref.shape[1] // MERGE_CHUNK):
        cols = slice(c * MERGE_CHUNK, (c + 1) * MERGE_CHUNK)
        ya = _dot(oa, wa_ref[:, cols].astype(BF16))
        yb = _dot(ob, wb_ref[:, cols].astype(BF16))
        o_ref[:, cols] = (_sigmoid(ma_ref[:, cols]) * ya
                          + _sigmoid(mb_ref[:, cols]) * yb).astype(o_ref.dtype)


def _resident_slab(w_all, layer):
    return pl.BlockSpec((None,) + tuple(w_all.shape[1:]), lambda i: (layer, 0, 0),
                        pipeline_mode=pl.Buffered(1))


def _merge(o_a, o_b, w_a, w_b, layer, h_main, tm):
    m, k = o_a.shape
    n = w_a.shape[2]
    ca, cb = COL_MERGE_A // n, COL_MERGE_B // n
    return pl.pallas_call(
        _merge_kernel,
        out_shape=jax.ShapeDtypeStruct((m, n), BF16),
        grid=(m // tm,),
        in_specs=[pl.BlockSpec((tm, k), lambda i: (i, 0)),
                  pl.BlockSpec((tm, k), lambda i: (i, 0)),
                  _resident_slab(w_a, layer), _resident_slab(w_b, layer),
                  pl.BlockSpec((tm, n), lambda i: (i, ca)),
                  pl.BlockSpec((tm, n), lambda i: (i, cb))],
        out_specs=pl.BlockSpec((tm, n), lambda i: (i, 0)),
        compiler_params=_cparams(("parallel",)),
        name="merge",
    )(o_a, o_b, w_a, w_b, h_main, h_main)


def _outproj_ln_kernel(mx_ref, w_ref, x_ref, g_ref, b_ref, o_ref, ob_ref):
    half = mx_ref.shape[0] // 2
    w = w_ref[...].astype(BF16)
    for r in (slice(0, half), slice(half, 2 * half)):
        y = ALPHA * x_ref[r, :] + _dot(mx_ref[r, :], w)
        out = _layer_norm(y, g_ref[...], b_ref[...])
        o_ref[r, :] = out
        ob_ref[r, :] = out.astype(BF16)


def _outproj_ln(mixed, w_out, layer, x, g, b, tm):
    m, d = x.shape
    return pl.pallas_call(
        _outproj_ln_kernel,
        out_shape=(jax.ShapeDtypeStruct((m, d), F32), jax.ShapeDtypeStruct((m, d), BF16)),
        grid=(m // tm,),
        in_specs=[pl.BlockSpec((tm, d), lambda i: (i, 0)),
                  _resident_slab(w_out, layer),
                  pl.BlockSpec((tm, d), lambda i: (i, 0)),
                  pl.BlockSpec((1, d), lambda i: (0, 0)),
                  pl.BlockSpec((1, d), lambda i: (0, 0))],
        out_specs=(pl.BlockSpec((tm, d), lambda i: (i, 0)),
                   pl.BlockSpec((tm, d), lambda i: (i, 0))),
        compiler_params=_cparams(("parallel",)),
        name="outproj_ln",
    )(mixed, w_out, x, g.reshape(1, d), b.reshape(1, d))


def _ffn_act_kernel(x_ref, wg_ref, wu_ref, o_ref):
    xv = x_ref[...]
    gate = _dot(xv, wg_ref[...].astype(BF16))
    up = _dot(xv, wu_ref[...].astype(BF16))
    o_ref[...] = (_silu(gate) * up).astype(o_ref.dtype)


def _ffn_act(xb, w_gate, w_up, layer, tm, tn):
    m, k = xb.shape
    n = w_gate.shape[2]
    return pl.pallas_call(
        _ffn_act_kernel,
        out_shape=jax.ShapeDtypeStruct((m, n), BF16),
        grid=(m // tm, n // tn),
        in_specs=[pl.BlockSpec((tm, k), lambda i, j: (i, 0)),
                  pl.BlockSpec((None, k, tn), lambda i, j: (layer, 0, j)),
                  pl.BlockSpec((None, k, tn), lambda i, j: (layer, 0, j))],
        out_specs=pl.BlockSpec((tm, tn), lambda i, j: (i, j)),
        compiler_params=_cparams(("parallel", "arbitrary")),
        name="ffn_act",
    )(xb, w_gate, w_up)


def _resid_kernel(x_ref, xb_ref, p_ref, wp_ref, wpg_ref, o_ref):
    xb = xb_ref[...]
    pv = p_ref[...]
    for c in range(o_ref.shape[1] // MERGE_CHUNK):
        cols = slice(c * MERGE_CHUNK, (c + 1) * MERGE_CHUNK)
        ple = (_dot(pv, wp_ref[:, cols].astype(BF16))
               * _sigmoid(_dot(xb, wpg_ref[:, cols].astype(BF16))))
        o_ref[:, cols] = ALPHA * x_ref[:, cols] + ple


def _resid(x1, x1b, pb, w_ple, w_ple_gate, layer, tm):
    m, d = x1.shape
    kp = pb.shape[1]
    return pl.pallas_call(
        _resid_kernel,
        out_shape=jax.ShapeDtypeStruct((m, d), F32),
        grid=(m // tm,),
        in_specs=[pl.BlockSpec((tm, d), lambda i: (i, 0)),
                  pl.BlockSpec((tm, d), lambda i: (i, 0)),
                  pl.BlockSpec((tm, kp), lambda i: (i, 0)),
                  _resident_slab(w_ple, layer), _resident_slab(w_ple_gate, layer)],
        out_specs=pl.BlockSpec((tm, d), lambda i: (i, 0)),
        compiler_params=_cparams(("parallel",)),
        name="ple_resid",
    )(x1, x1b, pb, w_ple, w_ple_gate)


def _ffn_out_kernel(act_ref, w_ref, r_ref, g_ref, b_ref, o_ref):
    half = act_ref.shape[0] // 2
    for r in (slice(0, half), slice(half, 2 * half)):
        y = r_ref[r, :] + _dot(act_ref[r, :], w_ref[...])
        o_ref[r, :] = _layer_norm(y, g_ref[...], b_ref[...])


def _ffn_out(act, w_down, resid, g, b, tm):
    m, kf = act.shape
    d = w_down.shape[1]
    return pl.pallas_call(
        _ffn_out_kernel,
        out_shape=jax.ShapeDtypeStruct((m, d), F32),
        grid=(m // tm,),
        in_specs=[pl.BlockSpec((tm, kf), lambda i: (i, 0)),
                  pl.BlockSpec((kf, d), lambda i: (0, 0), pipeline_mode=pl.Buffered(1)),
                  pl.BlockSpec((tm, d), lambda i: (i, 0)),
                  pl.BlockSpec((1, d), lambda i: (0, 0)),
                  pl.BlockSpec((1, d), lambda i: (0, 0))],
        out_specs=pl.BlockSpec((tm, d), lambda i: (i, 0)),
        compiler_params=_cparams(("parallel",)),
        name="ffn_out_ln",
    )(act, w_down, resid, g.reshape(1, d), b.reshape(1, d))


DN_TS = 256
DN_HB = 8


def _dn_kernel(alog_ref, dtb_ref,
               q_ref, k_ref, v_ref, z_ref, qp_ref, kp_ref, vp_ref, hs_ref,
               cwq_ref, cwk_ref, cwv_ref, nw_ref, o_ref, state_ref, cbuf_ref):
    hg = pl.program_id(1)
    s = pl.program_id(2)
    c = DN_CHUNK
    d = HEAD_DIM

    @pl.when(s == 0)
    def _():
        state_ref[...] = jnp.zeros_like(state_ref)

    conv_slots = []

    def conv_silu(x_ref, xp_ref, cw_ref, hb):
        cols = slice(hb * d, (hb + 1) * d)
        buf = cbuf_ref.at[len(conv_slots)]
        conv_slots.append(None)
        buf[0:SUBLANES, :] = jnp.where(s == 0, 0.0, xp_ref[:, cols])
        buf[SUBLANES:, :] = x_ref[:, cols]
        cw = cw_ref[:, cols]
        y = None
        for i in range(DN_CONV):
            off = SUBLANES - (DN_CONV - 1) + i
            term = buf[off:off + DN_TS, :] * cw[i:i + 1]
            y = term if y is None else y + term
        return _silu(y)

    def l2norm(x):
        return x * lax.rsqrt(jnp.sum(x * x, axis=-1, keepdims=True) + NORM_EPS)

    row = lax.broadcasted_iota(jnp.int32, (c, c), 0)
    col = lax.broadcasted_iota(jnp.int32, (c, c), 1)
    causal = row >= col
    strict = row > col
    tri_b = jnp.where(causal, 1.0, 0.0).astype(BF16)
    eye = jnp.where(row == col, 1.0, 0.0).astype(F32)
    ones8_b = jnp.ones((SUBLANES, c), BF16)
    hs = hs_ref[...]
    nw = nw_ref[...]

    nck = DN_TS // c
    pairs = [(hb, ci) for hb in range(DN_HB) for ci in range(nck)]
    qs, ks, vs, betas, gbs = [], [], [], [], []
    for hb in range(DN_HB):
        h = hg * DN_HB + hb
        q_all = l2norm(conv_silu(q_ref, qp_ref, cwq_ref, hb)) * (d ** -0.5)
        k_all = l2norm(conv_silu(k_ref, kp_ref, cwk_ref, hb))
        v_all = conv_silu(v_ref, vp_ref, cwv_ref, hb)
        beta_all = _sigmoid(_lane_col(hs, SC_BETA + h))
        a_all = _lane_col(hs, SC_DECAY + h) + dtb_ref[h]
        softplus = jnp.maximum(a_all, 0.0) + jnp.log(1.0 + jnp.exp(-jnp.abs(a_all)))
        g_all = -jnp.exp(jnp.zeros_like(a_all) + alog_ref[h]) * softplus
        for ci in range(nck):
            sl = slice(ci * c, (ci + 1) * c)
            qs.append(q_all[sl])
            ks.append(k_all[sl])
            vs.append(v_all[sl])
            betas.append(beta_all[sl])
            gbs.append(jnp.broadcast_to(g_all[sl], (c, LANES)))
    n = len(pairs)
    gcs = [_dot_01(tri_b, gbs[i]) for i in range(n)]
    gc_rows = [_dot_01(ones8_b, jnp.where(row <= col, gbs[i][:, :c], 0.0))[0:1] for i in range(n)]
    decays = [jnp.exp(jnp.where(causal, gcs[i][:, :c] - gc_rows[i], NEG_INF)) for i in range(n)]
    kbs = [ks[i] * betas[i] for i in range(n)]
    kbfs = [ks[i].astype(BF16) for i in range(n)]
    negs = [jnp.where(strict, -(_dot_nt(kbs[i].astype(BF16), kbfs[i]) * decays[i]), 0.0)
            for i in range(n)]
    accs = [eye + negs[i] for i in range(n)]
    nbf = [negs[i].astype(BF16) for i in range(n)]
    pws = [_dot(nbf[i], nbf[i]) for i in range(n)]
    for _ in range(4):
        pbs = [pws[i].astype(BF16) for i in range(n)]
        accs = [accs[i] + _dot(pbs[i], accs[i].astype(BF16)) for i in range(n)]
        pws = [_dot(pbs[i], pbs[i]) for i in range(n)]
    tinvs = [(accs[i] + _dot(pws[i].astype(BF16), accs[i].astype(BF16))).astype(BF16)
             for i in range(n)]
    egs = [jnp.exp(gcs[i]) for i in range(n)]
    uws = [_dot(tinvs[i], jnp.concatenate([vs[i] * betas[i], kbs[i] * egs[i]], axis=1).astype(BF16))
           .astype(BF16) for i in range(n)]
    qks = [jnp.where(causal, _dot_nt(qs[i].astype(BF16), kbfs[i]) * decays[i], 0.0).astype(BF16)
           for i in range(n)]
    k_decs = [(ks[i] * jnp.exp(gcs[i][c - 1:c] - gcs[i])).astype(BF16) for i in range(n)]
    nbs = [_dot_tn(k_decs[i], uws[i]) for i in range(n)]
    prs = [_dot(qks[i], uws[i]) for i in range(n)]
    lhss = [jnp.concatenate([(qs[i] * egs[i] - prs[i][:, d:]).astype(BF16),
                             nbs[i][:, d:].astype(BF16)], axis=0) for i in range(n)]

    states = [state_ref[hb] for hb in range(DN_HB)]
    for ci in range(nck):
        sl = slice(ci * c, (ci + 1) * c)
        ress = [_dot(lhss[hb * nck + ci], states[hb].astype(BF16)) for hb in range(DN_HB)]
        for hb in range(DN_HB):
            i = hb * nck + ci
            cols = slice(hb * d, (hb + 1) * d)
            o = ress[hb][:c] + prs[i][:, :d]
            states[hb] = states[hb] * egs[i][c - 1:c] - ress[hb][c:] + nbs[i][:, :d]
            o = o * lax.rsqrt(jnp.mean(o * o, axis=-1, keepdims=True) + NORM_EPS) * nw
            o_ref[sl, cols] = (o * _silu(z_ref[sl, cols])).astype(o_ref.dtype)
    for hb in range(DN_HB):
        state_ref[hb] = states[hb]


def _deltanet(h_main, h_small, conv_w, a_log, dt_bias, norm_w):
    b_, s_len, _ = h_main.shape
    ts = DN_TS
    wd = DN_HB * HEAD_DIM
    ngrp = HEADS // DN_HB
    blk = lambda cb: pl.BlockSpec((None, ts, wd), lambda b, h, s: (b, s, cb // DN_HB + h))
    prev = lambda cb: pl.BlockSpec(
        (None, SUBLANES, wd),
        lambda b, h, s: (b, jnp.maximum(s * (ts // SUBLANES) - 1, 0), cb // DN_HB + h))
    cw = lambda cb: pl.BlockSpec((DN_CONV, wd), lambda b, h, s: (0, cb // DN_HB + h))
    smem = pl.BlockSpec(memory_space=pltpu.SMEM)
    return pl.pallas_call(
        _dn_kernel,
        out_shape=jax.ShapeDtypeStruct((b_, s_len, HEADS * HEAD_DIM), BF16),
        grid=(b_, ngrp, s_len // ts),
        in_specs=[smem, smem,
                  blk(CB_DN_Q), blk(CB_DN_K), blk(CB_DN_V), blk(CB_DN_Z),
                  prev(CB_DN_Q), prev(CB_DN_K), prev(CB_DN_V),
                  pl.BlockSpec((None, ts, LANES), lambda b, h, s: (b, s, 0)),
                  cw(CB_DN_Q), cw(CB_DN_K), cw(CB_DN_V),
                  pl.BlockSpec((1, LANES), lambda b, h, s: (0, 0))],
        out_specs=pl.BlockSpec((None, ts, wd), lambda b, h, s: (b, s, h)),
        scratch_shapes=[pltpu.VMEM((DN_HB, HEAD_DIM, HEAD_DIM), F32),
                        pltpu.VMEM((3 * DN_HB, SUBLANES + DN_TS, HEAD_DIM), F32)],
        compiler_params=_cparams(("parallel", "parallel", "arbitrary")),
        name="deltanet",
    )(a_log, dt_bias, h_main, h_main, h_main, h_main, h_main, h_main, h_main, h_small,
      conv_w, conv_w, conv_w, norm_w.reshape(1, LANES))


def _rope(x, cosf, sinf):
    return x * cosf + pltpu.roll(x, HEAD_DIM // 2, axis=1) * sinf


def _nsa_prep_kernel(kc_ref, vc_ref, ks_ref, vs_ref, kw_ref, vw_ref, cos_ref, sin_ref,
                     pek_ref, w1k_ref, w2k_ref, pev_ref, w1v_ref, w2v_ref,
                     kso_ref, vso_ref, kwo_ref, vwo_ref, kco_ref, vco_ref, buf_ref):
    cosf = cos_ref[...]
    sinf = sin_ref[...]
    s_len = ks_ref.shape[0]
    pos = lax.broadcasted_iota(jnp.int32, (s_len, LANES), 0)
    lane = lax.broadcasted_iota(jnp.int32, (s_len, LANES), 1)
    kso_ref[:, :HEAD_DIM] = _rope(ks_ref[...], cosf, sinf).astype(BF16)
    kso_ref[:, HEAD_DIM:] = jnp.where(pos // SEL_BLOCK == lane, NEG_INF, 0.0).astype(BF16)
    kwo_ref[...] = _rope(kw_ref[...], cosf, sinf).astype(BF16)
    ones = jnp.ones((s_len, HEAD_DIM), BF16)
    vso_ref[:, :HEAD_DIM] = vs_ref[...].astype(BF16)
    vso_ref[:, HEAD_DIM:] = ones
    vwo_ref[:, :HEAD_DIM] = vw_ref[...].astype(BF16)
    vwo_ref[:, HEAD_DIM:] = ones
    nch = buf_ref.shape[0] // CMP_STRIDE

    def compress(pe_ref, w1_ref, w2_ref, out_ref):
        a0 = jnp.zeros((nch, CMP_HIDDEN), F32)
        a1 = jnp.zeros((nch, CMP_HIDDEN), F32)
        for i in range(CMP_STRIDE):
            xi = buf_ref[pl.ds(i, nch, stride=CMP_STRIDE), :]
            lo = (xi + pe_ref[i:i + 1, :]).astype(BF16)
            hi = (xi + pe_ref[CMP_STRIDE + i:CMP_STRIDE + i + 1, :]).astype(BF16)
            a0 = a0 + _dot(lo, w1_ref[i * HEAD_DIM:(i + 1) * HEAD_DIM, :])
            a1 = a1 + _dot(hi, w1_ref[(CMP_STRIDE + i) * HEAD_DIM:(CMP_STRIDE + i + 1) * HEAD_DIM, :])
        hid = a0 + pltpu.roll(a1, nch - 1, axis=0)
        out_ref[...] = _dot(_silu(hid).astype(BF16), w2_ref[...]).astype(out_ref.dtype)

    buf_ref[...] = _rope(kc_ref[...], cosf, sinf)
    compress(pek_ref, w1k_ref, w2k_ref, kco_ref)
    buf_ref[...] = vc_ref[...]
    compress(pev_ref, w1v_ref, w2v_ref, vco_ref)


def _nsa_prep(h_main, cosf, sinf, pe_k, w1_k, w2_k, pe_v, w1_v, w2_v):
    b_, s_len, _ = h_main.shape
    nch = s_len // CMP_STRIDE
    kv = lambda i: pl.BlockSpec((None, s_len, LANES), lambda b, g: (b, 0, CB_KV + GROUPS * i + g))
    full = lambda shape: pl.BlockSpec(shape, lambda b, g: tuple(0 for _ in shape))
    assert s_len // SEL_BLOCK <= LANES
    seq_out = pl.BlockSpec((None, None, s_len, LANES), lambda b, g: (b, g, 0, 0))
    aug_out = pl.BlockSpec((None, None, s_len, 2 * LANES), lambda b, g: (b, g, 0, 0))
    cmp_out = pl.BlockSpec((None, None, nch, LANES), lambda b, g: (b, g, 0, 0))
    seq_shape = jax.ShapeDtypeStruct((b_, GROUPS, s_len, HEAD_DIM), BF16)
    aug_shape = jax.ShapeDtypeStruct((b_, GROUPS, s_len, 2 * HEAD_DIM), BF16)
    cmp_shape = jax.ShapeDtypeStruct((b_, GROUPS, nch, HEAD_DIM), BF16)
    return pl.pallas_call(
        _nsa_prep_kernel,
        out_shape=(aug_shape, aug_shape, seq_shape, aug_shape, cmp_shape, cmp_shape),
        grid=(b_, GROUPS),
        in_specs=[kv(0), kv(1), kv(2), kv(3), kv(4), kv(5),
                  full((s_len, LANES)), full((s_len, LANES)),
                  full((CMP_LEN, HEAD_DIM)), full((CMP_LEN * HEAD_DIM, CMP_HIDDEN)),
                  full((CMP_HIDDEN, HEAD_DIM)),
                  full((CMP_LEN, HEAD_DIM)), full((CMP_LEN * HEAD_DIM, CMP_HIDDEN)),
                  full((CMP_HIDDEN, HEAD_DIM))],
        out_specs=(aug_out, aug_out, seq_out, aug_out, cmp_out, cmp_out),
        scratch_shapes=[pltpu.VMEM((s_len, HEAD_DIM), F32)],
        compiler_params=_cparams(("parallel", "parallel")),
        name="nsa_prep",
    )(h_main, h_main, h_main, h_main, h_main, h_main, cosf, sinf,
      pe_k, w1_k, w2_k, pe_v, w1_v, w2_v)


NSA_TQ = 256
NSA_TK = 256


def _nsa_attn_kernel(q_ref, hs_ref, cos_ref, sin_ref, kc_ref, vc_ref, ks_ref, vs_ref,
                     kw_ref, vw_ref, ovt_ref, o_ref, os_ref):
    g = pl.program_id(1)
    qi = pl.program_id(2)
    tq, tk = NSA_TQ, NSA_TK
    rows = HPG * tq
    ns = ks_ref.shape[0] // SEL_BLOCK
    cosf = cos_ref[...]
    sinf = sin_ref[...]
    scale = HEAD_DIM ** -0.5
    qs = jnp.concatenate(
        [_rope(q_ref[:, hh * HEAD_DIM:(hh + 1) * HEAD_DIM], cosf, sinf) * scale
         for hh in range(HPG)], axis=0).astype(BF16)

    t_abs = qi * tq + lax.broadcasted_iota(jnp.int32, (tq, LANES), 0)
    lane = lax.broadcasted_iota(jnp.int32, (tq, LANES), 1)

    t_abs_k = qi * tq + lax.broadcasted_iota(jnp.int32, (tq, tk), 0)
    lane_k = lax.broadcasted_iota(jnp.int32, (tq, tk), 1)

    def add_bias(s_blk, bias):
        return (s_blk.reshape(HPG, tq, tk) + bias[None]).reshape(rows, tk)

    def split_rows(dot_fn, lhs, rhs):
        half = lhs.shape[0] // 2
        return jnp.concatenate([dot_fn(lhs[:half], rhs), dot_fn(lhs[half:], rhs)], axis=0)

    def fold(x):
        return x[:, :LANES], x[:, LANES:]

    neg_rows = jnp.full((rows, LANES), NEG_INF, F32)

    def running_max(score_blocks):
        mrun = neg_rows
        for s_blk in score_blocks:
            s0, s1 = fold(s_blk)
            mrun = jnp.maximum(mrun, jnp.maximum(s0, s1))
        return jnp.broadcast_to(jnp.max(mrun, axis=-1, keepdims=True), (rows, LANES))

    def probs(s_blk, m_b):
        s0, s1 = fold(s_blk)
        return jnp.exp(jnp.concatenate([(s0 - m_b).astype(BF16), (s1 - m_b).astype(BF16)], axis=1))

    def weighted_values(ps, value_rows):
        acc = _dot(jnp.concatenate(ps, axis=1), value_rows)
        return acc[:, :HEAD_DIM] / acc[:, HEAD_DIM:]

    kb_last = (qi * tq) // tk
    nwin = WIN // tk + 1
    win_start = pl.multiple_of(jnp.maximum(kb_last - (nwin - 1), 0) * tk, tk)
    win_raw = split_rows(_dot_nt, qs, kw_ref[pl.ds(win_start, nwin * tk), :])

    mask_c = jnp.concatenate([lane * CMP_STRIDE + (CMP_LEN - 1) <= t_abs] * HPG, axis=0)
    s_c = split_rows(_dot_nt, qs, kc_ref[...])
    m_c = jnp.max(jnp.where(mask_c, s_c, NEG_INF), axis=-1, keepdims=True)
    e_c = jnp.where(mask_c, jnp.exp(s_c - m_c), 0.0)
    l_c = jnp.sum(e_c, axis=-1, keepdims=True)
    p_c = jnp.where(l_c > 0.0, e_c / l_c, 0.0)

    win_s = []
    for d in range(nwin):
        diff = t_abs_k - (win_start + d * tk + lane_k)
        bias = jnp.where((diff >= 0) & (diff < WIN), 0.0, NEG_INF)
        win_s.append(add_bias(win_raw[:, d * tk:(d + 1) * tk], bias))
    win_m = running_max(win_s)

    o_c = split_rows(_dot, p_c.astype(BF16), vc_ref[...])
    psum = p_c[0:tq]
    for hh in range(1, HPG):
        psum = psum + p_c[hh * tq:(hh + 1) * tq]
    p_hi, p_lo = _split2(psum)
    ovt = ovt_ref[...]
    imp = (_dot_nt(ovt, p_hi) + _dot_nt(ovt, p_lo))[:ns]
    blk = lax.broadcasted_iota(jnp.int32, (ns, tq), 0)
    cur = (qi * tq + lax.broadcasted_iota(jnp.int32, (ns, tq), 1)) // SEL_BLOCK
    forced = (blk == 0) | (blk == cur) | (blk == cur - 1)
    imp = jnp.where(forced, jnp.inf, jnp.where(blk <= cur, imp, -jnp.inf))

    win_p = [probs(s_blk, win_m) for s_blk in win_s]

    rank = jnp.zeros((ns, tq), F32)
    for i in range(ns):
        ci = imp[i:i + 1, :]
        before = (ci > imp) | ((ci == imp) & (blk > i))
        rank = rank + jnp.where(before, 1.0, 0.0)
    unsel_t = jnp.where(rank < float(min(SEL_TOPK, ns)), 0.0, 1.0)
    unsel = jnp.concatenate([unsel_t, jnp.zeros((LANES - ns, tq), F32)], axis=0).T
    q_aug = jnp.concatenate([qs, jnp.concatenate([unsel.astype(BF16)] * HPG, axis=0)], axis=1)

    o_w = weighted_values(win_p, vw_ref[pl.ds(win_start, nwin * tk), :])

    def sel_variant(n_full):
        def run():
            n_keys = (n_full + 1) * tk
            s_all = split_rows(_dot_nt, q_aug, ks_ref[0:n_keys, :])
            s_blks = [s_all[:, j * tk:(j + 1) * tk] for j in range(n_full + 1)]
            bias = jnp.where(n_full * tk + lane_k <= t_abs_k, 0.0, NEG_INF)
            s_blks[n_full] = add_bias(s_blks[n_full], bias)
            m_b = running_max(s_blks)
            ps = [probs(s_blk, m_b) for s_blk in s_blks]
            os_ref[...] = weighted_values(ps, vs_ref[0:n_keys, :])
        return run

    for n_full in range(ks_ref.shape[0] // tk):
        pl.when(kb_last == n_full)(sel_variant(n_full))
    o_s = os_ref[...]

    hs = hs_ref[...]
    for hh in range(HPG):
        gbase = SC_GATE + (g * HPG + hh) * 3
        r = slice(hh * tq, (hh + 1) * tq)
        out = (_sigmoid(_lane_col(hs, gbase)) * o_c[r]
               + _sigmoid(_lane_col(hs, gbase + 1)) * o_s[r]
               + _sigmoid(_lane_col(hs, gbase + 2)) * o_w[r])
        o_ref[:, hh * HEAD_DIM:(hh + 1) * HEAD_DIM] = out.astype(o_ref.dtype)


def _nsa_attn(h_main, h_small, cosf, sinf, kc, vc, ks, vs, kw, vw, overlap_t):
    b_, s_len, _ = h_main.shape
    tq = NSA_TQ
    nch = kc.shape[2]
    assert nch == LANES, "the compressed-block axis is laid out on one vreg of lanes"
    assert s_len % NSA_TK == 0 and NSA_TK == 2 * LANES
    qw = HPG * HEAD_DIM
    seq = pl.BlockSpec((None, None, s_len, HEAD_DIM), lambda b, g, i: (b, g, 0, 0))
    aug = pl.BlockSpec((None, None, s_len, 2 * HEAD_DIM), lambda b, g, i: (b, g, 0, 0))
    cmp_ = pl.BlockSpec((None, None, nch, HEAD_DIM), lambda b, g, i: (b, g, 0, 0))
    return pl.pallas_call(
        _nsa_attn_kernel,
        out_shape=jax.ShapeDtypeStruct((b_, s_len, HEADS * HEAD_DIM), BF16),
        grid=(b_, GROUPS, s_len // tq),
        in_specs=[pl.BlockSpec((None, tq, qw), lambda b, g, i: (b, i, CB_NSA_Q * LANES // qw + g)),
                  pl.BlockSpec((None, tq, LANES), lambda b, g, i: (b, i, 0)),
                  pl.BlockSpec((tq, LANES), lambda b, g, i: (i, 0)),
                  pl.BlockSpec((tq, LANES), lambda b, g, i: (i, 0)),
                  cmp_, cmp_, aug, aug, seq, aug,
                  pl.BlockSpec(overlap_t.shape, lambda b, g, i: (0, 0))],
        out_specs=pl.BlockSpec((None, tq, qw), lambda b, g, i: (b, i, g)),
        scratch_shapes=[pltpu.VMEM((HPG * tq, HEAD_DIM), F32)],
        compiler_params=_cparams(("parallel", "parallel", "arbitrary")),
        name="nsa_attn",
    )(h_main, h_small, cosf, sinf, kc, vc, ks, vs, kw, vw, overlap_t)


def _nsa_constants(s_len):
    half = HEAD_DIM // 2
    inv_freq = ROPE_THETA ** (-jnp.arange(half, dtype=F32) / half)
    ang = jnp.arange(s_len, dtype=F32)[:, None] * inv_freq[None, :]
    cos, sin = jnp.cos(ang), jnp.sin(ang)
    cosf = jnp.concatenate([cos, cos], axis=-1)
    sinf = jnp.concatenate([-sin, sin], axis=-1)
    nch = s_len // CMP_STRIDE
    ns = s_len // SEL_BLOCK
    n = jnp.arange(nch)[:, None] * CMP_STRIDE
    j = jnp.arange(LANES)[None, :] * SEL_BLOCK
    overlap = ((n <= j + SEL_BLOCK - 1) & (n + CMP_LEN - 1 >= j)
               & (jnp.arange(nch)[:, None] < nch - CMP_LEN // CMP_STRIDE + 1)
               & (jnp.arange(LANES)[None, :] < ns)).astype(BF16)
    return cosf, sinf, overlap.T


def _layer(x, p_i, w_in, conv_w, a_log, dt_bias, norm_w, pe_k, w1_k, w2_k, pe_v, w1_v, w2_v,
           w_a_all, w_b_all, w_out_all, ln1_g, ln1_b, w_gate_all, w_up_all, layer, w_down,
           w_ple_all, w_ple_gate_all, ln2_g, ln2_b, consts):
    b_, s_len, d = x.shape
    t = b_ * s_len
    cosf, sinf, overlap_t = consts
    x2 = x.reshape(t, d)

    w_main = jnp.concatenate([w_in[:, :IN_DN_SMALL], w_in[:, IN_MERGE:], w_in[:, IN_NSA:IN_GATE]],
                             axis=1).astype(BF16)
    n_small = IN_NSA - IN_DN_SMALL + IN_MERGE - IN_GATE
    w_small = jnp.concatenate([w_in[:, IN_DN_SMALL:IN_NSA], w_in[:, IN_GATE:IN_MERGE],
                               jnp.zeros((d, LANES - n_small), w_in.dtype)], axis=1).astype(BF16)
    tm = min(1024, t)
    h_main = _matmul(x2, w_main, F32, tm, 1536, "proj_main")
    h_small = _matmul(x2, w_small, F32, tm, LANES, "proj_small")
    h_main3 = h_main.reshape(b_, s_len, N_MAIN)
    h_small3 = h_small.reshape(b_, s_len, LANES)

    o_a = _deltanet(h_main3, h_small3, conv_w, a_log, dt_bias, norm_w)
    ks, vs, kw, vw, kc, vc = _nsa_prep(h_main3, cosf, sinf, pe_k, w1_k.astype(BF16),
                                       w2_k.astype(BF16), pe_v, w1_v.astype(BF16),
                                       w2_v.astype(BF16))
    o_b = _nsa_attn(h_main3, h_small3, cosf, sinf, kc, vc, ks, vs, kw, vw, overlap_t)

    mixed = _merge(o_a.reshape(t, -1), o_b.reshape(t, -1), w_a_all, w_b_all, layer,
                   h_main, min(512, t))
    x1, x1b = _outproj_ln(mixed, w_out_all, layer, x2, ln1_g, ln1_b, min(512, t))
    act = _ffn_act(x1b, w_gate_all, w_up_all, layer, tm, 512)
    resid = _resid(x1, x1b, p_i.reshape(t, PLE_DIM).astype(BF16), w_ple_all, w_ple_gate_all,
                   layer, min(512, t))
    y = _ffn_out(act, w_down.astype(BF16), resid, ln2_g, ln2_b, min(512, t))
    return y.reshape(b_, s_len, d)


def kernel(x, p, w_in, dn_conv_w, dn_a_log, dn_dt_bias, dn_norm_w, cmp_pe_k, cmp_w1_k, cmp_w2_k, cmp_pe_v, cmp_w1_v, cmp_w2_v, w_branch_a, w_branch_b, w_out, ln1_g, ln1_b, w_ffn_gate, w_ffn_up, w_ffn_down, w_ple, w_ple_gate, ln2_g, ln2_b):
    consts = _nsa_constants(x.shape[1])
    for i in range(DEPTH):
        x = _layer(x, p[i], w_in[i], dn_conv_w[i], dn_a_log[i], dn_dt_bias[i], dn_norm_w[i],
                       cmp_pe_k[i], cmp_w1_k[i], cmp_w2_k[i], cmp_pe_v[i], cmp_w1_v[i], cmp_w2_v[i],
                       w_branch_a, w_branch_b, w_out, ln1_g[i], ln1_b[i],
                       w_ffn_gate, w_ffn_up, i, w_ffn_down[i], w_ple, w_ple_gate,
                       ln2_g[i], ln2_b[i], consts)
    return x
```
